```python
import math, functools
import jax, jax.numpy as jnp
from jax import lax
import numpy as np

D_MODEL = 4096
BATCH = 1
SEQ = 8192
DEPTH = 1
DEC_BATCH = 128
DEC_SEQ = 1
PAST_LEN = 8192
PAGE_SIZE = 128

ATTN_HEADS = 32
ATTN_KV_HEADS = 8
ATTN_HEAD_DIM = 64
ATTN_GROUP = ATTN_HEADS // ATTN_KV_HEADS
ATTN_WIDTH = ATTN_HEADS * ATTN_HEAD_DIM
KV_WIDTH = ATTN_KV_HEADS * ATTN_HEAD_DIM
WINDOW = 128
ROPE_THETA = 10000.0
DN_HEADS = 16
DN_KEY_DIM = 128
DN_VALUE_DIM = 128
DN_KEY_WIDTH = DN_HEADS * DN_KEY_DIM
DN_VALUE_WIDTH = DN_HEADS * DN_VALUE_DIM
CONV_WIDTH = 4
CONV_DIM = 2 * DN_KEY_WIDTH + DN_VALUE_WIDTH
DN_CHUNK = 64
MIX_WIDTH = ATTN_WIDTH + DN_VALUE_WIDTH
ATTN_QKV_WIDTH = ATTN_WIDTH + 2 * KV_WIDTH
IN_WIDTH = ATTN_QKV_WIDTH + CONV_DIM + DN_VALUE_WIDTH + 2 * DN_HEADS
FFN_HIDDEN = ((8 * D_MODEL + 3 * 256 - 1) // (3 * 256)) * 256
DEEPNORM_ALPHA = (2 * DEPTH) ** 0.25
DEEPNORM_BETA = (8 * DEPTH) ** -0.25
LN_EPS = 1e-5
RMS_EPS = 1e-6

kernel_name = 'hymba_swa_sink_gated_deltanet_deepnorm_step'


def layer_norm(x, g, b):
    xf = x.astype(jnp.float32)
    mu = jnp.mean(xf, axis=-1, keepdims=True)
    var = jnp.mean(jnp.square(xf - mu), axis=-1, keepdims=True)
    return ((xf - mu) * lax.rsqrt(var + LN_EPS) * g + b).astype(x.dtype)


def l2_normalize(x):
    xf = x.astype(jnp.float32)
    return xf * lax.rsqrt(jnp.sum(xf * xf, axis=-1, keepdims=True) + RMS_EPS)


def rope(x, pos):
    half = x.shape[-1] // 2
    inv_freq = ROPE_THETA ** (-jnp.arange(half, dtype=jnp.float32) / half)
    ang = pos[..., None].astype(jnp.float32) * inv_freq
    cos, sin = jnp.cos(ang)[:, :, None, :], jnp.sin(ang)[:, :, None, :]
    xf = x.astype(jnp.float32)
    x1, x2 = xf[..., :half], xf[..., half:]
    return jnp.concatenate([x1 * cos - x2 * sin, x2 * cos + x1 * sin], axis=-1).astype(x.dtype)


def sink_window_attention(q, k, v, pos_q, pos_k, sinks):
    n, tq = q.shape[:2]
    qg = q.reshape(n, tq, ATTN_KV_HEADS, ATTN_GROUP, ATTN_HEAD_DIM).astype(jnp.float32)
    s = jnp.einsum('nqhgd,nshd->nhgqs', qg, k.astype(jnp.float32)) * (ATTN_HEAD_DIM ** -0.5)
    diff = pos_q[:, :, None] - pos_k[:, None, :]
    valid = (diff >= 0) & (diff < WINDOW) & (pos_k[:, None, :] >= 0)
    s = jnp.where(valid[:, None, None], s, -jnp.inf)
    sink = jnp.broadcast_to(
        sinks.reshape(ATTN_KV_HEADS, ATTN_GROUP)[None, :, :, None, None].astype(jnp.float32),
        s.shape[:-1] + (1,))
    p = jax.nn.softmax(jnp.concatenate([s, sink], axis=-1), axis=-1)[..., :-1]
    o = jnp.einsum('nhgqs,nshd->nqhgd', p, v.astype(jnp.float32))
    return o.reshape(n, tq, ATTN_WIDTH).astype(q.dtype)


def prompt_window_attention(q, k, v, sinks):
    n, t = q.shape[:2]
    nb = t // WINDOW
    qb = q.reshape(n * nb, WINDOW, ATTN_HEADS, ATTN_HEAD_DIM)

    def band(a):
        ab = a.reshape(n, nb, WINDOW, ATTN_KV_HEADS, ATTN_HEAD_DIM)
        prev = jnp.pad(ab[:, :-1], ((0, 0), (1, 0), (0, 0), (0, 0), (0, 0)))
        return jnp.concatenate([prev, ab], axis=2).reshape(n * nb, 2 * WINDOW, ATTN_KV_HEADS, ATTN_HEAD_DIM)

    pos = jnp.arange(t, dtype=jnp.int32).reshape(nb, WINDOW)
    pos_q = jnp.tile(pos, (n, 1))
    pos_k = jnp.tile(jnp.concatenate([pos - WINDOW, pos], axis=-1), (n, 1))
    out = sink_window_attention(qb, band(k), band(v), pos_q, pos_k, sinks).reshape(n, t, ATTN_WIDTH)
    return out, k[:, -WINDOW:], v[:, -WINDOW:]


def sample_window_attention(q, k, v, sinks, k_buf, v_buf):
    t = q.shape[1]
    lb = k_buf.shape[1]
    k_all = jnp.concatenate([k_buf, k.astype(k_buf.dtype)], axis=1)
    v_all = jnp.concatenate([v_buf, v.astype(v_buf.dtype)], axis=1)
    pos_q = (PAST_LEN + jnp.arange(t, dtype=jnp.int32))[None]
    pos_k = (PAST_LEN - lb + jnp.arange(lb + t, dtype=jnp.int32))[None]
    out = sink_window_attention(q, k_all, v_all, pos_q, pos_k, sinks)
    return out, k_all[:, -lb:], v_all[:, -lb:]


def causal_short_conv(xc, buf, w_conv):
    t = xc.shape[1]
    xp = jnp.concatenate([buf.astype(xc.dtype), xc], axis=1)
    y = sum(w_conv[j] * xp[:, j:j + t] for j in range(CONV_WIDTH))
    return jax.nn.silu(y), xp[:, -(CONV_WIDTH - 1):]


def gated_delta_rule(q, k, v, g, beta, s0, chunk):
    n, t, h, dk = q.shape
    dv = v.shape[-1]
    nc = t // chunk

    def blocks(a):
        return jnp.moveaxis(a.astype(jnp.float32).reshape((n, nc, chunk, h) + a.shape[3:]), 3, 1)

    q, k, v, g, beta = blocks(q), blocks(k), blocks(v), blocks(g), blocks(beta)
    cum = jnp.cumsum(g, axis=-1)
    idx = jnp.arange(chunk)
    causal = idx[:, None] >= idx[None, :]
    strict = idx[:, None] > idx[None, :]
    decay = jnp.exp(jnp.where(causal, cum[..., :, None] - cum[..., None, :], -jnp.inf))
    kk = jnp.einsum('nhcid,nhcjd->nhcij', k, k)
    lower = jnp.where(strict, beta[..., :, None] * kk * decay, 0.0) + jnp.eye(chunk, dtype=jnp.float32)
    gamma = jnp.exp(cum)
    rhs = jnp.concatenate([k * (beta * gamma)[..., None], v * beta[..., None]], axis=-1)
    sol = lax.linalg.triangular_solve(lower, rhs, left_side=True, lower=True, unit_diagonal=True)
    w, u = sol[..., :dk], sol[..., dk:]
    att = jnp.einsum('nhcid,nhcjd->nhcij', q, k) * decay
    q_dec = q * gamma[..., None]
    k_dec = k * jnp.exp(cum[..., -1:] - cum)[..., None]
    g_last = jnp.exp(cum[..., -1])

    def step(s, xs):
        w_c, u_c, att_c, q_c, k_c, gl_c = xs
        u_new = u_c - jnp.einsum('nhcd,nhde->nhce', w_c, s)
        o = jnp.einsum('nhcd,nhde->nhce', q_c, s) + jnp.einsum('nhij,nhje->nhie', att_c, u_new)
        s = gl_c[..., None, None] * s + jnp.einsum('nhcd,nhce->nhde', k_c, u_new)
        return s, o

    xs = tuple(jnp.moveaxis(a, 2, 0) for a in (w, u, att, q_dec, k_dec, g_last))
    s_fin, o = lax.scan(step, s0.astype(jnp.float32), xs)
    o = jnp.transpose(o, (1, 0, 3, 2, 4)).reshape(n, t, h, dv)
    return o, s_fin.astype(s0.dtype)


def gated_deltanet(qkv_pre, z, a, b, conv_buf, s0, chunk, w_conv, dn_a_log, dn_dt_bias, dn_norm_w):
    n, t, _ = qkv_pre.shape
    qkv, conv_new = causal_short_conv(qkv_pre, conv_buf, w_conv)
    q = l2_normalize(qkv[..., :DN_KEY_WIDTH].reshape(n, t, DN_HEADS, DN_KEY_DIM)) * (DN_KEY_DIM ** -0.5)
    k = l2_normalize(qkv[..., DN_KEY_WIDTH:2 * DN_KEY_WIDTH].reshape(n, t, DN_HEADS, DN_KEY_DIM))
    v = qkv[..., 2 * DN_KEY_WIDTH:].reshape(n, t, DN_HEADS, DN_VALUE_DIM)
    g = -jnp.exp(dn_a_log.astype(jnp.float32)) * jax.nn.softplus((a + dn_dt_bias).astype(jnp.float32))
    beta = jax.nn.sigmoid(b.astype(jnp.float32))
    o, s_new = gated_delta_rule(q, k, v, g, beta, s0, chunk)
    o = o * lax.rsqrt(jnp.mean(o * o, axis=-1, keepdims=True) + RMS_EPS) * dn_norm_w
    o = o * jax.nn.silu(z.reshape(n, t, DN_HEADS, DN_VALUE_DIM).astype(jnp.float32))
    return o.reshape(n, t, DN_VALUE_WIDTH).astype(qkv_pre.dtype), conv_new, s_new


def hybrid_layer(x, pos, attn_fn, conv_buf, s0, chunk, w_in, b_attn, attn_sinks, w_conv,
                 dn_a_log, dn_dt_bias, dn_norm_w, w_out, ln1_g, ln1_b,
                 w_gate, w_up, w_down, ln2_g, ln2_b):
    n, t, _ = x.shape
    proj = jnp.einsum('btd,de->bte', x, w_in)
    qkv_a = proj[..., :ATTN_QKV_WIDTH] + b_attn
    q_a = rope(qkv_a[..., :ATTN_WIDTH].reshape(n, t, ATTN_HEADS, ATTN_HEAD_DIM), pos)
    k_a = rope(qkv_a[..., ATTN_WIDTH:ATTN_WIDTH + KV_WIDTH].reshape(n, t, ATTN_KV_HEADS, ATTN_HEAD_DIM), pos)
    v_a = qkv_a[..., ATTN_WIDTH + KV_WIDTH:].reshape(n, t, ATTN_KV_HEADS, ATTN_HEAD_DIM)
    attn_out, k_buf, v_buf = attn_fn(q_a, k_a, v_a, attn_sinks)
    off = ATTN_QKV_WIDTH
    qkv_pre = proj[..., off:off + CONV_DIM]
    off += CONV_DIM
    z = proj[..., off:off + DN_VALUE_WIDTH]
    off += DN_VALUE_WIDTH
    a = proj[..., off:off + DN_HEADS]
    off += DN_HEADS
    b = proj[..., off:off + DN_HEADS]
    dn_out, conv_new, s_new = gated_deltanet(qkv_pre, z, a, b, conv_buf, s0, chunk,
                                             w_conv, dn_a_log, dn_dt_bias, dn_norm_w)
    mixed = jnp.einsum('bte,ed->btd', jnp.concatenate([attn_out, dn_out], axis=-1), w_out)
    h = layer_norm(DEEPNORM_ALPHA * x + mixed, ln1_g, ln1_b)
    ffn = jax.nn.silu(jnp.einsum('btd,df->btf', h, w_gate)) * jnp.einsum('btd,df->btf', h, w_up)
    y = layer_norm(DEEPNORM_ALPHA * h + jnp.einsum('btf,fd->btd', ffn, w_down), ln2_g, ln2_b)
    return y, k_buf, v_buf, conv_new, s_new


def setup_inputs(seed: int = 0) -> dict:
    key = jax.random.key(seed)
    ks = jax.random.split(key, 24)

    def nrm(k, shape, scale):
        return jax.random.normal(k, shape, jnp.float32) * scale

    buf_len = min(WINDOW, PAST_LEN)
    dt = jnp.exp(jax.random.uniform(ks[11], (DEPTH, DN_HEADS), jnp.float32,
                                    minval=math.log(1e-3), maxval=math.log(1e-1)))
    return {
        'x_prompt': nrm(ks[0], (BATCH, SEQ, D_MODEL), 1.0),
        'x_sample': nrm(ks[1], (DEC_BATCH, DEC_SEQ, D_MODEL), 1.0),
        'cache_swa_k': nrm(ks[2], (DEPTH, DEC_BATCH, buf_len, ATTN_KV_HEADS, ATTN_HEAD_DIM), 1.0),
        'cache_swa_v': nrm(ks[3], (DEPTH, DEC_BATCH, buf_len, ATTN_KV_HEADS, ATTN_HEAD_DIM), 1.0),
        'state_conv': nrm(ks[4], (DEPTH, DEC_BATCH, CONV_WIDTH - 1, CONV_DIM), 1.0),
        'state_delta': nrm(ks[5], (DEPTH, DEC_BATCH, DN_HEADS, DN_KEY_DIM, DN_VALUE_DIM), DN_KEY_DIM ** -0.5),
        'w_in': nrm(ks[6], (DEPTH, D_MODEL, IN_WIDTH), D_MODEL ** -0.5),
        'b_attn': nrm(ks[7], (DEPTH, ATTN_QKV_WIDTH), 0.02),
        'attn_sinks': nrm(ks[8], (DEPTH, ATTN_HEADS), 0.5),
        'w_conv': nrm(ks[9], (DEPTH, CONV_WIDTH, CONV_DIM), CONV_WIDTH ** -0.5),
        'dn_a_log': jnp.log(jax.random.uniform(ks[10], (DEPTH, DN_HEADS), jnp.float32, minval=1.0, maxval=16.0)),
        'dn_dt_bias': dt + jnp.log(-jnp.expm1(-dt)),
        'dn_norm_w': 1.0 + nrm(ks[12], (DEPTH, DN_VALUE_DIM), 0.02),
        'w_out': nrm(ks[13], (DEPTH, MIX_WIDTH, D_MODEL), MIX_WIDTH ** -0.5 * DEEPNORM_BETA),
        'ln1_g': 1.0 + nrm(ks[14], (DEPTH, D_MODEL), 0.02),
        'ln1_b': nrm(ks[15], (DEPTH, D_MODEL), 0.02),
        'w_gate': nrm(ks[16], (DEPTH, D_MODEL, FFN_HIDDEN), D_MODEL ** -0.5),
        'w_up': nrm(ks[17], (DEPTH, D_MODEL, FFN_HIDDEN), D_MODEL ** -0.5),
        'w_down': nrm(ks[18], (DEPTH, FFN_HIDDEN, D_MODEL), FFN_HIDDEN ** -0.5 * DEEPNORM_BETA),
        'ln2_g': 1.0 + nrm(ks[19], (DEPTH, D_MODEL), 0.02),
        'ln2_b': nrm(ks[20], (DEPTH, D_MODEL), 0.02),
    }


def reference(x_prompt, x_sample, cache_swa_k, cache_swa_v, state_conv, state_delta,
              w_in, b_attn, attn_sinks, w_conv, dn_a_log, dn_dt_bias, dn_norm_w, w_out,
              ln1_g, ln1_b, w_gate, w_up, w_down, ln2_g, ln2_b):
    pos_prompt = jnp.arange(x_prompt.shape[1], dtype=jnp.int32)[None]
    pos_sample = (PAST_LEN + jnp.arange(x_sample.shape[1], dtype=jnp.int32))[None]
    y_p, y_s = x_prompt, x_sample
    pk, pv, pc, ps, sk, sv, sc, ss = [], [], [], [], [], [], [], []
    for l in range(DEPTH):
        lw = dict(w_in=w_in[l], b_attn=b_attn[l], attn_sinks=attn_sinks[l], w_conv=w_conv[l],
                  dn_a_log=dn_a_log[l], dn_dt_bias=dn_dt_bias[l], dn_norm_w=dn_norm_w[l],
                  w_out=w_out[l], ln1_g=ln1_g[l], ln1_b=ln1_b[l], w_gate=w_gate[l],
                  w_up=w_up[l], w_down=w_down[l], ln2_g=ln2_g[l], ln2_b=ln2_b[l])
        n_p = y_p.shape[0]
        conv0 = jnp.zeros((n_p, CONV_WIDTH - 1, CONV_DIM), y_p.dtype)
        s0 = jnp.zeros((n_p, DN_HEADS, DN_KEY_DIM, DN_VALUE_DIM), state_delta.dtype)
        y_p, k1, v1, c1, s1 = hybrid_layer(y_p, pos_prompt, prompt_window_attention,
                                           conv0, s0, DN_CHUNK, **lw)
        attn_sample = functools.partial(sample_window_attention,
                                        k_buf=cache_swa_k[l], v_buf=cache_swa_v[l])
        y_s, k2, v2, c2, s2 = hybrid_layer(y_s, pos_sample, attn_sample, state_conv[l],
                                           state_delta[l], y_s.shape[1], **lw)
        pk.append(k1); pv.append(v1); pc.append(c1); ps.append(s1)
        sk.append(k2); sv.append(v2); sc.append(c2); ss.append(s2)
    return (y_p, y_s, jnp.stack(pk), jnp.stack(pv), jnp.stack(pc), jnp.stack(ps),
            jnp.stack(sk), jnp.stack(sv), jnp.stack(sc), jnp.stack(ss))
```

```python
import functools
import math

import jax
import jax.numpy as jnp
from jax import lax
from jax.experimental import pallas as pl
from jax.experimental.pallas import tpu as pltpu

D_MODEL = 4096
SEQ = 8192
DEC_BATCH = 128
PAST_LEN = 8192
ROWS = SEQ + DEC_BATCH

ATTN_HEADS = 32
ATTN_KV_HEADS = 8
ATTN_HEAD_DIM = 64
ATTN_GROUP = ATTN_HEADS // ATTN_KV_HEADS
ATTN_WIDTH = ATTN_HEADS * ATTN_HEAD_DIM
KV_WIDTH = ATTN_KV_HEADS * ATTN_HEAD_DIM
WINDOW = 128
ROPE_THETA = 10000.0
DN_HEADS = 16
DN_DIM = 128
DN_WIDTH = DN_HEADS * DN_DIM
CONV_WIDTH = 4
CONV_DIM = 3 * DN_WIDTH
DN_CHUNK = 64
ATTN_QKV_WIDTH = ATTN_WIDTH + 2 * KV_WIDTH
MAIN_WIDTH = ATTN_QKV_WIDTH + CONV_DIM + DN_WIDTH
FFN_HIDDEN = 11008
DEEPNORM_ALPHA = 2.0 ** 0.25
LN_EPS = 1e-5
RMS_EPS = 1e-6

LANES = 128
VMEM_LIMIT = 56 * 1024 * 1024

F32 = jnp.float32
BF16 = jnp.bfloat16
HI = lax.Precision.HIGHEST


def _cparams(sem, vmem=VMEM_LIMIT):
    return pltpu.CompilerParams(dimension_semantics=sem, vmem_limit_bytes=vmem)


def _dot(a, b, precision=None):
    return jnp.dot(a, b, preferred_element_type=F32, precision=precision)


def _dot_nt(a, b, precision=None):
    return lax.dot_general(a, b, (((1,), (1,)), ((), ())), preferred_element_type=F32,
                           precision=precision)


def _dot_tn(a, b, precision=None):
    return lax.dot_general(a, b, (((0,), (0,)), ((), ())), preferred_element_type=F32,
                           precision=precision)


def _sigmoid(x):
    return 1.0 / (1.0 + jnp.exp(-x))


def _silu(x):
    return x * _sigmoid(x)


def _softplus(x):
    return jnp.maximum(x, 0.0) + jnp.log(1.0 + jnp.exp(-jnp.abs(x)))


def _rope(x, cos, sin_signed):
    lane = lax.broadcasted_iota(jnp.int32, x.shape, 1)
    first_half = (lane % ATTN_HEAD_DIM) < (ATTN_HEAD_DIM // 2)
    partner = jnp.where(first_half, pltpu.roll(x, LANES - ATTN_HEAD_DIM // 2, 1),
                        pltpu.roll(x, ATTN_HEAD_DIM // 2, 1))
    return x * cos + partner * sin_signed


def _mm_kernel(x_ref, w_ref, o_ref):
    o_ref[...] = _dot(x_ref[...], w_ref[...]).astype(o_ref.dtype)


def _matmul(x, w, bm, bn, out_dtype=F32):
    m, k = x.shape
    n = w.shape[1]
    return pl.pallas_call(
        _mm_kernel,
        grid=(m // bm, n // bn),
        in_specs=[pl.BlockSpec((bm, k), lambda i, j: (i, 0)),
                  pl.BlockSpec((k, bn), lambda i, j: (0, j))],
        out_specs=pl.BlockSpec((bm, bn), lambda i, j: (i, j)),
        out_shape=jax.ShapeDtypeStruct((m, n), out_dtype),
        compiler_params=_cparams(("parallel", "parallel")),
        name="matmul",
    )(x, w)


def _out_proj_kernel(a_ref, d_ref, w_ref, o_ref):
    ka = a_ref.shape[1]
    o_ref[...] = _dot(a_ref[...], w_ref[:ka, :]) + _dot(d_ref[...], w_ref[ka:, :])


def _out_proj(a, d, w, bm, bn):
    m, ka = a.shape
    kd = d.shape[1]
    n = w.shape[1]
    return pl.pallas_call(
        _out_proj_kernel,
        grid=(m // bm, n // bn),
        in_specs=[pl.BlockSpec((bm, ka), lambda i, j: (i, 0)),
                  pl.BlockSpec((bm, kd), lambda i, j: (i, 0)),
                  pl.BlockSpec((ka + kd, bn), lambda i, j: (0, j))],
        out_specs=pl.BlockSpec((bm, bn), lambda i, j: (i, j)),
        out_shape=jax.ShapeDtypeStruct((m, n), F32),
        compiler_params=_cparams(("parallel", "parallel")),
        name="out_proj",
    )(a, d, w)


def _ffn_kernel(h_ref, wg_ref, wu_ref, wd_ref, o_ref):
    f = pl.program_id(1)
    h = h_ref[...]
    g = _dot(h, wg_ref[...])
    u = _dot(h, wu_ref[...])
    act = (_silu(g) * u).astype(BF16)
    part = _dot(act, wd_ref[...])

    @pl.when(f == 0)
    def _():
        o_ref[...] = part

    @pl.when(f > 0)
    def _():
        o_ref[...] += part


def _ffn(h, wg, wu, wd, bm, bf):
    m, d = h.shape
    hid = wg.shape[1]
    return pl.pallas_call(
        _ffn_kernel,
        grid=(m // bm, hid // bf),
        in_specs=[pl.BlockSpec((bm, d), lambda i, f: (i, 0)),
                  pl.BlockSpec((d, bf), lambda i, f: (0, f)),
                  pl.BlockSpec((d, bf), lambda i, f: (0, f)),
                  pl.BlockSpec((bf, d), lambda i, f: (f, 0))],
        out_specs=pl.BlockSpec((bm, d), lambda i, f: (i, 0)),
        out_shape=jax.ShapeDtypeStruct((m, d), F32),
        compiler_params=_cparams(("parallel", "arbitrary")),
        name="ffn",
    )(h, wg, wu, wd)


def _ln_kernel(x_ref, m_ref, g_ref, b_ref, o_ref, *maybe_bf16_ref):
    v = DEEPNORM_ALPHA * x_ref[...] + m_ref[...]
    mu = jnp.mean(v, axis=-1, keepdims=True)
    c = v - mu
    var = jnp.mean(c * c, axis=-1, keepdims=True)
    y = c * lax.rsqrt(var + LN_EPS) * g_ref[...] + b_ref[...]
    o_ref[...] = y
    for r in maybe_bf16_ref:
        r[...] = y.astype(BF16)


def _deepnorm_ln(x, mixed, g, b, br, with_bf16):
    m, d = x.shape
    row = pl.BlockSpec((br, d), lambda i: (i, 0))
    vec = pl.BlockSpec((1, d), lambda i: (0, 0))
    out_shape = [jax.ShapeDtypeStruct((m, d), F32)]
    out_specs = [row]
    if with_bf16:
        out_shape.append(jax.ShapeDtypeStruct((m, d), BF16))
        out_specs.append(row)
    return pl.pallas_call(
        _ln_kernel,
        grid=(m // br,),
        in_specs=[row, row, vec, vec],
        out_specs=out_specs,
        out_shape=out_shape,
        compiler_params=_cparams(("parallel",)),
        name="deepnorm_ln",
    )(x, mixed, g.reshape(1, d), b.reshape(1, d))


def _attn_prompt_kernel(sink_ref, q_ref, kc_ref, kp_ref, vc_ref, vp_ref, b_ref,
                        cc_ref, sc_ref, cp_ref, sp_ref, o_ref, ko_ref, vo_ref):
    i = pl.program_id(0)
    w = WINDOW
    cos_c, sin_c = cc_ref[...], sc_ref[...]
    cos_p, sin_p = cp_ref[...], sp_ref[...]
    n_kchunk = KV_WIDTH // LANES

    k_cur, k_prev, v_cur, v_prev = [], [], [], []
    for c in range(n_kchunk):
        sl = slice(c * LANES, (c + 1) * LANES)
        bk = b_ref[:, ATTN_WIDTH + c * LANES:ATTN_WIDTH + (c + 1) * LANES]
        bv = b_ref[:, ATTN_WIDTH + KV_WIDTH + c * LANES:ATTN_WIDTH + KV_WIDTH + (c + 1) * LANES]
        kc = _rope(kc_ref[:, sl] + bk, cos_c, sin_c)
        kp = _rope(kp_ref[:, sl] + bk, cos_p, sin_p)
        vc = vc_ref[:, sl] + bv
        vp = vp_ref[:, sl] + bv
        ko_ref[:, sl] = kc
        vo_ref[:, sl] = vc
        k_cur.append(kc)
        k_prev.append(kp)
        v_cur.append(vc)
        v_prev.append(vp)

    rows = ATTN_GROUP * w
    r = lax.broadcasted_iota(jnp.int32, (rows, 2 * w), 0) % w
    col = lax.broadcasted_iota(jnp.int32, (rows, 2 * w), 1)
    valid = (col > r) & (col <= r + w) & ((col >= w) | (i > 0))
    row_id = lax.broadcasted_iota(jnp.int32, (rows, 1), 0)
    lane_k = lax.broadcasted_iota(jnp.int32, (2 * w, LANES), 1)
    lane_o = lax.broadcasted_iota(jnp.int32, (w, LANES), 1)
    scale = ATTN_HEAD_DIM ** -0.5

    for hk in range(ATTN_KV_HEADS):
        kchunk, khalf = hk // 2, hk % 2
        k2 = jnp.concatenate([k_prev[kchunk], k_cur[kchunk]], axis=0)
        in_half = (lane_k // ATTN_HEAD_DIM) == khalf
        k2 = jnp.where(in_half, k2, 0.0).astype(BF16)
        v2 = jnp.concatenate([v_prev[kchunk], v_cur[kchunk]], axis=0).astype(BF16)
        qs = []
        for g in range(ATTN_GROUP):
            hq = hk * ATTN_GROUP + g
            qchunk, qhalf = hq // 2, hq % 2
            sl = slice(qchunk * LANES, (qchunk + 1) * LANES)
            qc = _rope(q_ref[:, sl] + b_ref[:, sl], cos_c, sin_c) * scale
            if qhalf != khalf:
                qc = pltpu.roll(qc, ATTN_HEAD_DIM, 1)
            qs.append(qc)
        q4 = jnp.concatenate(qs, axis=0).astype(BF16)
        s = _dot_nt(q4, k2)
        s = jnp.where(valid, s, -jnp.inf)
        sink = jnp.zeros((rows, 1), F32)
        for g in range(ATTN_GROUP):
            sink = jnp.where(row_id // w == g, sink_ref[hk * ATTN_GROUP + g], sink)
        m = jnp.maximum(jnp.max(s, axis=-1, keepdims=True), sink)
        e = jnp.exp(s - m)
        den = jnp.sum(e, axis=-1, keepdims=True) + jnp.exp(sink - m)
        p = (e / den).astype(BF16)
        pv = _dot(p, v2)
        outs = []
        for g in range(ATTN_GROUP):
            og = pv[g * w:(g + 1) * w, :]
            if (g % 2) != khalf:
                og = pltpu.roll(og, ATTN_HEAD_DIM, 1)
            outs.append(og)
        for j in range(ATTN_GROUP // 2):
            oc = jnp.where(lane_o < ATTN_HEAD_DIM, outs[2 * j], outs[2 * j + 1])
            c = hk * (ATTN_GROUP // 2) + j
            o_ref[:, c * LANES:(c + 1) * LANES] = oc.astype(o_ref.dtype)


def _attn_prompt(proj, b_attn, sinks, cos_t, sin_t):
    nb = SEQ // WINDOW
    kcol = ATTN_WIDTH // KV_WIDTH
    prev = lambda i: jnp.maximum(i - 1, 0)
    return pl.pallas_call(
        _attn_prompt_kernel,
        grid=(nb,),
        in_specs=[pl.BlockSpec(memory_space=pltpu.SMEM),
                  pl.BlockSpec((WINDOW, ATTN_WIDTH), lambda i: (i, 0)),
                  pl.BlockSpec((WINDOW, KV_WIDTH), lambda i: (i, kcol)),
                  pl.BlockSpec((WINDOW, KV_WIDTH), lambda i: (prev(i), kcol)),
                  pl.BlockSpec((WINDOW, KV_WIDTH), lambda i: (i, kcol + 1)),
                  pl.BlockSpec((WINDOW, KV_WIDTH), lambda i: (prev(i), kcol + 1)),
                  pl.BlockSpec((1, ATTN_QKV_WIDTH), lambda i: (0, 0)),
                  pl.BlockSpec((WINDOW, LANES), lambda i: (i, 0)),
                  pl.BlockSpec((WINDOW, LANES), lambda i: (i, 0)),
                  pl.BlockSpec((WINDOW, LANES), lambda i: (prev(i), 0)),
                  pl.BlockSpec((WINDOW, LANES), lambda i: (prev(i), 0))],
        out_specs=[pl.BlockSpec((WINDOW, ATTN_WIDTH), lambda i: (i, 0)),
                   pl.BlockSpec((WINDOW, KV_WIDTH), lambda i: (0, 0)),
                   pl.BlockSpec((WINDOW, KV_WIDTH), lambda i: (0, 0))],
        out_shape=[jax.ShapeDtypeStruct((SEQ, ATTN_WIDTH), BF16),
                   jax.ShapeDtypeStruct((WINDOW, KV_WIDTH), F32),
                   jax.ShapeDtypeStruct((WINDOW, KV_WIDTH), F32)],
        compiler_params=_cparams(("arbitrary",)),
        name="attn_prompt",
    )(sinks, proj, proj, proj, proj, proj, b_attn.reshape(1, ATTN_QKV_WIDTH),
      cos_t, sin_t, cos_t, sin_t)


ATTN_SB = 8


def _attn_sample_kernel(q_ref, k_ref, v_ref, bq_ref, bk_ref, bv_ref, cos_ref, sin_ref,
                        sink_ref, e_ref, et_ref, ck_ref, cv_ref, o_ref, ko_ref, vo_ref):
    cos, sin = cos_ref[...], sin_ref[...]
    scale = ATTN_HEAD_DIM ** -0.5
    qr = []
    for c in range(ATTN_WIDTH // LANES):
        sl = slice(c * LANES, (c + 1) * LANES)
        qr.append(_rope(q_ref[:, sl] + bq_ref[:, sl], cos, sin) * scale)
    q = jnp.concatenate(qr, axis=1)
    kn = jnp.concatenate(
        [_rope(k_ref[:, c * LANES:(c + 1) * LANES] + bk_ref[:, c * LANES:(c + 1) * LANES], cos, sin)
         for c in range(KV_WIDTH // LANES)], axis=1)
    vn = v_ref[...] + bv_ref[...]
    lb = WINDOW
    for b in range(ATTN_SB):
        ko_ref[b, 0:lb - 1, :] = ck_ref[b, 1:lb, :]
        ko_ref[b, lb - 1:lb, :] = kn[b:b + 1, :]
        vo_ref[b, 0:lb - 1, :] = cv_ref[b, 1:lb, :]
        vo_ref[b, lb - 1:lb, :] = vn[b:b + 1, :]
        kb = ko_ref[b]
        vb = vo_ref[b]
        prod = jnp.concatenate(
            [kb * q[b:b + 1, g * KV_WIDTH:(g + 1) * KV_WIDTH] for g in range(ATTN_GROUP)], axis=0)
        s = _dot(prod, e_ref[...], HI)
        ps = []
        for g in range(ATTN_GROUP):
            sg = s[g * lb:(g + 1) * lb, :]
            sink = sink_ref[g:g + 1, :]
            m = jnp.maximum(jnp.max(sg, axis=0, keepdims=True), sink)
            e = jnp.exp(sg - m)
            den = jnp.sum(e, axis=0, keepdims=True) + jnp.exp(sink - m)
            ps.append(e / den)
        pe = _dot(jnp.concatenate(ps, axis=0), et_ref[...], HI)
        for g in range(ATTN_GROUP):
            og = jnp.sum(pe[g * lb:(g + 1) * lb, :] * vb, axis=0, keepdims=True)
            o_ref[b:b + 1, g * KV_WIDTH:(g + 1) * KV_WIDTH] = og


def _attn_sample(q_perm, k_new, v_new, bq_perm, bk, bv, cos_row, sin_row, sinks_gk, cache_k, cache_v):
    n = DEC_BATCH
    sb = ATTN_SB
    head_of_lane = jnp.arange(KV_WIDTH) // ATTN_HEAD_DIM
    e_mat = (head_of_lane[:, None] == jnp.arange(LANES)[None, :]).astype(F32)
    row = lambda w: pl.BlockSpec((sb, w), lambda i: (i, 0))
    vec = lambda w: pl.BlockSpec((1, w), lambda i: (0, 0))
    full = lambda a: pl.BlockSpec(a.shape, lambda i: (0,) * a.ndim)
    cache = pl.BlockSpec((sb, WINDOW, KV_WIDTH), lambda i: (i, 0, 0))
    return pl.pallas_call(
        _attn_sample_kernel,
        grid=(n // sb,),
        in_specs=[row(ATTN_WIDTH), row(KV_WIDTH), row(KV_WIDTH),
                  vec(ATTN_WIDTH), vec(KV_WIDTH), vec(KV_WIDTH), vec(LANES), vec(LANES),
                  full(sinks_gk), full(e_mat), full(e_mat.T), cache, cache],
        out_specs=[row(ATTN_WIDTH), cache, cache],
        out_shape=[jax.ShapeDtypeStruct((n, ATTN_WIDTH), F32),
                   jax.ShapeDtypeStruct(cache_k.shape, F32),
                   jax.ShapeDtypeStruct(cache_v.shape, F32)],
        compiler_params=_cparams(("parallel",)),
        name="attn_sample",
    )(q_perm, k_new, v_new, bq_perm, bk, bv, cos_row, sin_row, sinks_gk, e_mat, e_mat.T,
      cache_k, cache_v)


DN_CB = 8
DN_ROWS = DN_CB * DN_CHUNK
CONV_PAD = 8


def _dn_gates(ab, alog, dtb, h):
    lane = lax.broadcasted_iota(jnp.int32, ab.shape, 1)
    g_all = -jnp.exp(alog) * _softplus(ab + dtb)
    g = jnp.sum(jnp.where(lane == h, g_all, 0.0), axis=1, keepdims=True)
    beta = jnp.sum(jnp.where(lane == h + DN_HEADS, _sigmoid(ab), 0.0), axis=1, keepdims=True)
    return jnp.broadcast_to(g, ab.shape), jnp.broadcast_to(beta, ab.shape)


def _dn_prompt_kernel(q_ref, qp_ref, k_ref, kp_ref, v_ref, vp_ref, z_ref, ab_ref,
                      wq_ref, wk_ref, wv_ref, alog_ref, dtb_ref, nw_ref,
                      o_ref, so_ref,
                      xpad_s, qn_s, kn_s, vn_s, g_s, b_s, s_s):
    h = pl.program_id(0)
    i = pl.program_id(1)
    n_steps = pl.num_programs(1)
    rows = DN_ROWS
    c_len = DN_CHUNK

    @pl.when(i == 0)
    def _():
        s_s[...] = jnp.zeros_like(s_s)

    def conv(x_ref, prev_ref, w_ref):
        xpad_s[0:CONV_PAD, :] = jnp.where(i > 0, prev_ref[...], 0.0)
        xpad_s[CONV_PAD:, :] = x_ref[...]
        y = w_ref[CONV_WIDTH - 1:CONV_WIDTH, :] * x_ref[...]
        for j in range(CONV_WIDTH - 1):
            back = CONV_WIDTH - 1 - j
            y = y + w_ref[j:j + 1, :] * xpad_s[CONV_PAD - back:CONV_PAD - back + rows, :]
        return _silu(y)

    def l2n(x):
        return x * lax.rsqrt(jnp.sum(x * x, axis=-1, keepdims=True) + RMS_EPS)

    qn_s[...] = l2n(conv(q_ref, qp_ref, wq_ref)) * (DN_DIM ** -0.5)
    kn_s[...] = l2n(conv(k_ref, kp_ref, wk_ref))
    vn_s[...] = conv(v_ref, vp_ref, wv_ref)
    g_b, beta_b = _dn_gates(ab_ref[...], alog_ref[...], dtb_ref[...], h)
    g_s[...] = g_b
    b_s[...] = beta_b

    ri = lax.broadcasted_iota(jnp.int32, (c_len, c_len), 0)
    ci = lax.broadcasted_iota(jnp.int32, (c_len, c_len), 1)
    causal = ri >= ci
    strict = ri > ci
    tri = causal.astype(F32)
    ones = jnp.ones((c_len, c_len), F32)
    eye = (ri == ci).astype(F32)
    nw = nw_ref[...]

    def chunk(c, carry):
        sl = pl.ds(pl.multiple_of(c * c_len, c_len), c_len)
        qc, kc, vc = qn_s[sl, :], kn_s[sl, :], vn_s[sl, :]
        gb, bb = g_s[sl, :], b_s[sl, :]
        cum_i = _dot(tri, gb, HI)
        cum_j = _dot(ones, jnp.where(ri <= ci, gb[:, :c_len], 0.0), HI)
        decay = jnp.where(causal, jnp.exp(cum_i[:, :c_len] - cum_j), 0.0)
        kk = _dot_nt(kc, kc, HI)
        x = -jnp.where(strict, bb[:, :c_len] * kk * decay, 0.0)
        t = eye + x
        xp = x
        for _ in range(int(math.log2(c_len)) - 1):
            xp = _dot(xp, xp, HI)
            t = t + _dot(xp, t, HI)
        gamma = jnp.exp(cum_i)
        rhs = jnp.concatenate([kc * (bb * gamma), vc * bb], axis=1)
        sol = _dot(t, rhs, HI)
        w_c, u_c = sol[:, :DN_DIM], sol[:, DN_DIM:]
        att = _dot_nt(qc, kc, HI) * decay
        cum_last = cum_i[c_len - 1:c_len, :]
        q_dec = qc * gamma
        k_dec = kc * jnp.exp(cum_last - cum_i)
        s = s_s[...]
        u_new = u_c - _dot(w_c, s, HI)
        o = _dot(q_dec, s, HI) + _dot(att, u_new, HI)
        s_s[...] = jnp.exp(cum_last) * s + _dot_tn(k_dec, u_new, HI)
        o = o * lax.rsqrt(jnp.mean(o * o, axis=-1, keepdims=True) + RMS_EPS) * nw
        o_ref[sl, :] = (o * _silu(z_ref[sl, :])).astype(o_ref.dtype)
        return carry

    lax.fori_loop(0, DN_CB, chunk, 0)

    @pl.when(i == n_steps - 1)
    def _():
        so_ref[0] = s_s[...]


def _dn_prompt(proj, proj_ab, w_conv, alog_pad, dtb_pad, norm_w):
    rows = DN_ROWS
    q0 = ATTN_QKV_WIDTH // LANES
    k0 = q0 + DN_HEADS
    v0 = k0 + DN_HEADS
    z0 = v0 + DN_HEADS
    per = rows // CONV_PAD

    def blk(c0):
        return pl.BlockSpec((rows, LANES), lambda h, i: (i, c0 + h))

    def prev(c0):
        return pl.BlockSpec((CONV_PAD, LANES), lambda h, i: (jnp.maximum(i * per - 1, 0), c0 + h))

    def wblk(c0):
        return pl.BlockSpec((CONV_WIDTH, LANES), lambda h, i: (0, c0 + h))

    vec = pl.BlockSpec((1, LANES), lambda h, i: (0, 0))
    return pl.pallas_call(
        _dn_prompt_kernel,
        grid=(DN_HEADS, SEQ // rows),
        in_specs=[blk(q0), prev(q0), blk(k0), prev(k0), blk(v0), prev(v0), blk(z0),
                  pl.BlockSpec((rows, LANES), lambda h, i: (i, 0)),
                  wblk(0), wblk(DN_HEADS), wblk(2 * DN_HEADS), vec, vec, vec],
        out_specs=[pl.BlockSpec((rows, LANES), lambda h, i: (i, h)),
                   pl.BlockSpec((1, DN_DIM, DN_DIM), lambda h, i: (h, 0, 0))],
        out_shape=[jax.ShapeDtypeStruct((SEQ, DN_WIDTH), BF16),
                   jax.ShapeDtypeStruct((DN_HEADS, DN_DIM, DN_DIM), F32)],
        scratch_shapes=[pltpu.VMEM((rows + CONV_PAD, LANES), F32),
                        pltpu.VMEM((rows, LANES), F32), pltpu.VMEM((rows, LANES), F32),
                        pltpu.VMEM((rows, LANES), F32), pltpu.VMEM((rows, LANES), F32),
                        pltpu.VMEM((rows, LANES), F32), pltpu.VMEM((DN_DIM, DN_DIM), F32)],
        compiler_params=_cparams(("parallel", "arbitrary")),
        name="dn_prompt",
    )(proj, proj, proj, proj, proj, proj, proj, proj_ab, w_conv, w_conv, w_conv,
      alog_pad, dtb_pad, norm_w.reshape(1, DN_DIM))


DN_SB = 8


def _dn_sample_prep_kernel(x_ref, sc_ref, ab_ref, w_ref, alog_ref, dtb_ref,
                           qt_ref, kt_ref, vb_ref, bg_ref, gam_ref):
    ab = ab_ref[...]
    alog, dtb = alog_ref[...], dtb_ref[...]

    def conv(col):
        sl = slice(col, col + LANES)
        y = w_ref[CONV_WIDTH - 1:CONV_WIDTH, sl] * x_ref[:, sl]
        for j in range(CONV_WIDTH - 1):
            y = y + w_ref[j:j + 1, sl] * sc_ref[:, j * CONV_DIM + col:j * CONV_DIM + col + LANES]
        return _silu(y)

    def l2n(x):
        return x * lax.rsqrt(jnp.sum(x * x, axis=-1, keepdims=True) + RMS_EPS)

    for h in range(DN_HEADS):
        q = l2n(conv(h * DN_DIM)) * (DN_DIM ** -0.5)
        k = l2n(conv(DN_WIDTH + h * DN_DIM))
        v = conv(2 * DN_WIDTH + h * DN_DIM)
        g_b, beta_b = _dn_gates(ab, alog, dtb, h)
        gamma = jnp.exp(g_b)
        qt_ref[h] = q.T
        kt_ref[h] = k.T
        vb_ref[h] = v * beta_b
        bg_ref[h] = beta_b * gamma
        gam_ref[h] = gamma


def _dn_sample_prep(qkv_s, conv_state, ab_s, w_conv, alog_pad, dtb_pad):
    n = DEC_BATCH
    full = lambda a: pl.BlockSpec(a.shape, lambda i: (0,) * a.ndim)
    hm = jax.ShapeDtypeStruct((DN_HEADS, n, DN_DIM), F32)
    hm_spec = pl.BlockSpec((DN_HEADS, n, DN_DIM), lambda i: (0, 0, 0))
    args = (qkv_s, conv_state, ab_s, w_conv, alog_pad, dtb_pad)
    return pl.pallas_call(
        _dn_sample_prep_kernel,
        grid=(1,),
        in_specs=[full(a) for a in args],
        out_specs=[hm_spec] * 5,
        out_shape=[hm] * 5,
        compiler_params=_cparams(("arbitrary",)),
        name="dn_sample_prep",
    )(*args)


def _dn_sample_kernel(s_ref, qt_ref, kt_ref, vb_ref, bg_ref, gam_ref, z_ref, nw_ref,
                      so_ref, o_ref, o_s):
    step = pl.program_id(0)
    lane = lax.broadcasted_iota(jnp.int32, (DN_DIM, DEC_BATCH), 1)
    for bb in range(DN_SB):
        b = step * DN_SB + bb
        pick = lane == b

        def head(h, carry):
            kcol = jnp.sum(jnp.where(pick, kt_ref[h], 0.0), axis=1, keepdims=True)
            qcol = jnp.sum(jnp.where(pick, qt_ref[h], 0.0), axis=1, keepdims=True)
            s = s_ref[bb, h]
            ks = jnp.sum(kcol * s, axis=0, keepdims=True)
            u = vb_ref[h, pl.ds(b, 1), :] - bg_ref[h, pl.ds(b, 1), :] * ks
            s_new = gam_ref[h, pl.ds(b, 1), :] * s + kcol * u
            so_ref[bb, h] = s_new
            o_s[h, pl.ds(bb, 1), :] = jnp.sum(qcol * s_new, axis=0, keepdims=True)
            return carry

        lax.fori_loop(0, DN_HEADS, head, 0)
    nw = nw_ref[...]
    for h in range(DN_HEADS):
        o = o_s[h]
        o = o * lax.rsqrt(jnp.mean(o * o, axis=-1, keepdims=True) + RMS_EPS) * nw
        sl = slice(h * DN_DIM, (h + 1) * DN_DIM)
        o_ref[:, sl] = o * _silu(z_ref[:, sl])


def _dn_sample(state, qt, kt, vb, bg, gam, z_s, norm_w):
    n = DEC_BATCH
    sb = DN_SB
    sblk = pl.BlockSpec((sb, DN_HEADS, DN_DIM, DN_DIM), lambda i: (i, 0, 0, 0))
    hm = pl.BlockSpec((DN_HEADS, n, DN_DIM), lambda i: (0, 0, 0))
    row = pl.BlockSpec((sb, DN_WIDTH), lambda i: (i, 0))
    return pl.pallas_call(
        _dn_sample_kernel,
        grid=(n // sb,),
        in_specs=[sblk, hm, hm, hm, hm, hm, row, pl.BlockSpec((1, DN_DIM), lambda i: (0, 0))],
        out_specs=[sblk, row],
        out_shape=[jax.ShapeDtypeStruct(state.shape, F32), jax.ShapeDtypeStruct((n, DN_WIDTH), F32)],
        scratch_shapes=[pltpu.VMEM((DN_HEADS, sb, DN_DIM), F32)],
        compiler_params=_cparams(("parallel",)),
        name="dn_sample",
    )(state, qt, kt, vb, bg, gam, z_s, norm_w.reshape(1, DN_DIM))


BM = 1040
BM_FFN = 640
BF_FFN = 256
BR_LN = 320


def _rope_tables(pos):
    half = ATTN_HEAD_DIM // 2
    inv_freq = ROPE_THETA ** (-jnp.arange(half, dtype=F32) / half)
    ang = pos.astype(F32)[:, None] * inv_freq
    cos, sin = jnp.cos(ang), jnp.sin(ang)
    reps = LANES // ATTN_HEAD_DIM
    return jnp.tile(cos, (1, 2 * reps)), jnp.tile(jnp.concatenate([-sin, sin], axis=1), (1, reps))


def _group_major(a):
    lead = a.shape[:-1]
    a = a.reshape(lead + (ATTN_KV_HEADS, ATTN_GROUP, ATTN_HEAD_DIM))
    return jnp.swapaxes(a, -3, -2).reshape(lead + (ATTN_WIDTH,))


def _layer(x32, cache_k, cache_v, state_conv, state_delta, w_in, b_attn, attn_sinks, w_conv,
           dn_a_log, dn_dt_bias, dn_norm_w, w_out, ln1_g, ln1_b, w_gate, w_up, w_down, ln2_g, ln2_b):
    n_s = DEC_BATCH
    xb = x32.astype(BF16)
    w_main = w_in[:, :MAIN_WIDTH].astype(BF16)
    w_ab = jnp.pad(w_in[:, MAIN_WIDTH:], ((0, 0), (0, LANES - 2 * DN_HEADS))).astype(BF16)
    proj = _matmul(xb, w_main, BM, 1024)
    proj_ab = _matmul(xb, w_ab, BM, LANES)

    pad16 = lambda v: jnp.pad(v, (0, LANES - DN_HEADS)).reshape(1, LANES)
    alog_pad, dtb_pad = pad16(dn_a_log), pad16(dn_dt_bias)

    cos_t, sin_t = _rope_tables(jnp.arange(SEQ, dtype=jnp.int32))
    attn_p, pk, pv = _attn_prompt(proj, b_attn, attn_sinks, cos_t, sin_t)
    dn_p, ps = _dn_prompt(proj, proj_ab, w_conv, alog_pad, dtb_pad, dn_norm_w)
    pc = proj[SEQ - (CONV_WIDTH - 1):SEQ, ATTN_QKV_WIDTH:ATTN_QKV_WIDTH + CONV_DIM]

    proj_s = proj[SEQ:]
    cos_r, sin_r = _rope_tables(jnp.full((1,), PAST_LEN, jnp.int32))
    sinks_gk = jnp.pad(attn_sinks.reshape(ATTN_KV_HEADS, ATTN_GROUP).T,
                       ((0, 0), (0, LANES - ATTN_KV_HEADS)))
    attn_s_perm, sk, sv = _attn_sample(
        _group_major(proj_s[:, :ATTN_WIDTH]),
        proj_s[:, ATTN_WIDTH:ATTN_WIDTH + KV_WIDTH],
        proj_s[:, ATTN_WIDTH + KV_WIDTH:ATTN_QKV_WIDTH],
        _group_major(b_attn[:ATTN_WIDTH]).reshape(1, ATTN_WIDTH),
        b_attn[ATTN_WIDTH:ATTN_WIDTH + KV_WIDTH].reshape(1, KV_WIDTH),
        b_attn[ATTN_WIDTH + KV_WIDTH:].reshape(1, KV_WIDTH),
        cos_r, sin_r, sinks_gk, cache_k.reshape(n_s, WINDOW, KV_WIDTH),
        cache_v.reshape(n_s, WINDOW, KV_WIDTH))
    attn_s = jnp.swapaxes(attn_s_perm.reshape(n_s, ATTN_GROUP, ATTN_KV_HEADS, ATTN_HEAD_DIM), 1, 2)
    attn_s = attn_s.reshape(n_s, ATTN_WIDTH)

    qkv_s = proj_s[:, ATTN_QKV_WIDTH:ATTN_QKV_WIDTH + CONV_DIM]
    z_s = proj_s[:, ATTN_QKV_WIDTH + CONV_DIM:]
    qt, kt, vb, bg, gam = _dn_sample_prep(
        qkv_s, state_conv.reshape(n_s, (CONV_WIDTH - 1) * CONV_DIM), proj_ab[SEQ:],
        w_conv, alog_pad, dtb_pad)
    ss, dn_s = _dn_sample(state_delta, qt, kt, vb, bg, gam, z_s, dn_norm_w)
    sc = jnp.concatenate([state_conv[:, 1:], qkv_s[:, None, :]], axis=1)

    attn_all = jnp.concatenate([attn_p, attn_s.astype(BF16)], axis=0)
    dn_all = jnp.concatenate([dn_p, dn_s.astype(BF16)], axis=0)
    mixed = _out_proj(attn_all, dn_all, w_out.astype(BF16), BM, 1024)
    h32, hb = _deepnorm_ln(x32, mixed, ln1_g, ln1_b, BR_LN, True)
    ffn = _ffn(hb, w_gate.astype(BF16), w_up.astype(BF16), w_down.astype(BF16), BM_FFN, BF_FFN)
    (y,) = _deepnorm_ln(h32, ffn, ln2_g, ln2_b, BR_LN, False)
    return (y, pk.reshape(WINDOW, ATTN_KV_HEADS, ATTN_HEAD_DIM), pv.reshape(WINDOW, ATTN_KV_HEADS, ATTN_HEAD_DIM),
            pc, ps, sk.reshape(n_s, WINDOW, ATTN_KV_HEADS, ATTN_HEAD_DIM),
            sv.reshape(n_s, WINDOW, ATTN_KV_HEADS, ATTN_HEAD_DIM), sc, ss)


def kernel(x_prompt, x_sample, cache_swa_k, cache_swa_v, state_conv, state_delta, w_in, b_attn,
           attn_sinks, w_conv, dn_a_log, dn_dt_bias, dn_norm_w, w_out, ln1_g, ln1_b, w_gate, w_up,
           w_down, ln2_g, ln2_b):
    assert x_prompt.shape == (1, SEQ, D_MODEL) and x_sample.shape == (DEC_BATCH, 1, D_MODEL)
    assert w_in.shape[0] == 1, "one layer"
    x32 = jnp.concatenate([x_prompt[0], x_sample[:, 0]], axis=0)
    y, pk, pv, pc, ps, sk, sv, sc, ss = _layer(
        x32, cache_swa_k[0], cache_swa_v[0], state_conv[0], state_delta[0], w_in[0], b_attn[0],
        attn_sinks[0], w_conv[0], dn_a_log[0], dn_dt_bias[0], dn_norm_w[0], w_out[0], ln1_g[0],
        ln1_b[0], w_gate[0], w_up[0], w_down[0], ln2_g[0], ln2_b[0])
    return (y[:SEQ][None], y[SEQ:][:, None], pk[None, None], pv[None, None], pc[None, None],
            ps[None, None], sk[None], sv[None], sc[None], ss[None])
```

```python
import math

import jax
import jax.numpy as jnp
from jax import lax
from jax.experimental import pallas as pl
from jax.experimental.pallas import tpu as pltpu

D_MODEL = 4096
SEQ = 8192
DEC_BATCH = 128
PAST_LEN = 8192
ROWS = SEQ + DEC_BATCH

ATTN_HEADS = 32
ATTN_KV_HEADS = 8
ATTN_HEAD_DIM = 64
ATTN_GROUP = ATTN_HEADS // ATTN_KV_HEADS
ATTN_WIDTH = ATTN_HEADS * ATTN_HEAD_DIM
KV_WIDTH = ATTN_KV_HEADS * ATTN_HEAD_DIM
WINDOW = 128
ROPE_THETA = 10000.0
DN_HEADS = 16
DN_DIM = 128
DN_WIDTH = DN_HEADS * DN_DIM
CONV_WIDTH = 4
CONV_DIM = 3 * DN_WIDTH
DN_CHUNK = 64
ATTN_QKV_WIDTH = ATTN_WIDTH + 2 * KV_WIDTH
MAIN_WIDTH = ATTN_QKV_WIDTH + CONV_DIM + DN_WIDTH
FFN_HIDDEN = 11008
DEEPNORM_ALPHA = 2.0 ** 0.25
LN_EPS = 1e-5
RMS_EPS = 1e-6

LANES = 128
VMEM_LIMIT = 56 * 1024 * 1024

F32 = jnp.float32
BF16 = jnp.bfloat16


def _cparams(sem, vmem=VMEM_LIMIT):
    return pltpu.CompilerParams(dimension_semantics=sem, vmem_limit_bytes=vmem)


def _dot(a, b):
    return jnp.dot(a, b, preferred_element_type=F32)


def _dot_nt(a, b):
    return lax.dot_general(a, b, (((1,), (1,)), ((), ())), preferred_element_type=F32)


def _sigmoid(x):
    return 1.0 / (1.0 + jnp.exp(-x))


def _silu(x):
    return x * _sigmoid(x)


def _softplus(x):
    return jnp.maximum(x, 0.0) + jnp.log(1.0 + jnp.exp(-jnp.abs(x)))


def _rope(x, cos, sin_signed):
    lane = lax.broadcasted_iota(jnp.int32, x.shape, 1)
    first_half = (lane % ATTN_HEAD_DIM) < (ATTN_HEAD_DIM // 2)
    partner = jnp.where(first_half, pltpu.roll(x, LANES - ATTN_HEAD_DIM // 2, 1),
                        pltpu.roll(x, ATTN_HEAD_DIM // 2, 1))
    return x * cos + partner * sin_signed


def _col_blocks(w, bn):
    k, n = w.shape
    return jnp.swapaxes(w.reshape(k, n // bn, bn), 0, 1)


def _mm_kernel(x_ref, w_ref, o_ref):
    o_ref[...] = _dot(x_ref[...], w_ref[0]).astype(o_ref.dtype)


def _matmul(x, w_blocks, bm, out_dtype=F32):
    m, k = x.shape
    nb, _, bn = w_blocks.shape
    return pl.pallas_call(
        _mm_kernel,
        grid=(m // bm, nb),
        in_specs=[pl.BlockSpec((bm, k), lambda i, j: (i, 0)),
                  pl.BlockSpec((1, k, bn), lambda i, j: (j, 0, 0))],
        out_specs=pl.BlockSpec((bm, bn), lambda i, j: (i, j)),
        out_shape=jax.ShapeDtypeStruct((m, nb * bn), out_dtype),
        compiler_params=_cparams(("parallel", "parallel")),
        name="matmul",
    )(x, w_blocks)


def _out_proj_kernel(a_ref, d_ref, w_ref, o_ref):
    ka = a_ref.shape[1]
    o_ref[...] = _dot(a_ref[...], w_ref[0, :ka, :]) + _dot(d_ref[...], w_ref[0, ka:, :])


def _out_proj(a, d, w_blocks, bm):
    m, ka = a.shape
    kd = d.shape[1]
    nb, _, bn = w_blocks.shape
    return pl.pallas_call(
        _out_proj_kernel,
        grid=(m // bm, nb),
        in_specs=[pl.BlockSpec((bm, ka), lambda i, j: (i, 0)),
                  pl.BlockSpec((bm, kd), lambda i, j: (i, 0)),
                  pl.BlockSpec((1, ka + kd, bn), lambda i, j: (j, 0, 0))],
        out_specs=pl.BlockSpec((bm, bn), lambda i, j: (i, j)),
        out_shape=jax.ShapeDtypeStruct((m, nb * bn), F32),
        compiler_params=_cparams(("parallel", "parallel")),
        name="out_proj",
    )(a, d, w_blocks)


FFN_NCHUNK = 1024


def _ffn_kernel(h_ref, wgu_ref, wd_ref, o_ref):
    f = pl.program_id(1)
    bf = wd_ref.shape[1]

    @pl.when(f == 0)
    def _():
        o_ref[...] = jnp.zeros_like(o_ref)

    r = _dot(h_ref[...], wgu_ref[0])
    act = (_silu(r[:, :bf]) * r[:, bf:]).astype(BF16)
    for n in range(0, o_ref.shape[1], FFN_NCHUNK):
        o_ref[:, n:n + FFN_NCHUNK] += _dot(act, wd_ref[0, :, n:n + FFN_NCHUNK])


def _ffn(h, wgu_blocks, wd_blocks, bm):
    m, d = h.shape
    nf, _, bf2 = wgu_blocks.shape
    bf = bf2 // 2
    once = pl.Buffered(1)
    return pl.pallas_call(
        _ffn_kernel,
        grid=(m // bm, nf),
        in_specs=[pl.BlockSpec((bm, d), lambda i, f: (i, 0), pipeline_mode=once),
                  pl.BlockSpec((1, d, bf2), lambda i, f: (f, 0, 0)),
                  pl.BlockSpec((1, bf, d), lambda i, f: (f, 0, 0))],
        out_specs=pl.BlockSpec((bm, d), lambda i, f: (i, 0), pipeline_mode=once),
        out_shape=jax.ShapeDtypeStruct((m, d), F32),
        compiler_params=_cparams(("parallel", "arbitrary")),
        name="ffn",
    )(h, wgu_blocks, wd_blocks)


def _ln_kernel(x_ref, m_ref, g_ref, b_ref, o_ref, *maybe_bf16_ref):
    v = DEEPNORM_ALPHA * x_ref[...] + m_ref[...]
    mu = jnp.mean(v, axis=-1, keepdims=True)
    c = v - mu
    var = jnp.mean(c * c, axis=-1, keepdims=True)
    y = c * lax.rsqrt(var + LN_EPS) * g_ref[...] + b_ref[...]
    o_ref[...] = y
    for r in maybe_bf16_ref:
        r[...] = y.astype(BF16)


def _deepnorm_ln(x, mixed, g, b, br, with_bf16):
    m, d = x.shape
    row = pl.BlockSpec((br, d), lambda i: (i, 0))
    vec = pl.BlockSpec((1, d), lambda i: (0, 0))
    out_shape = [jax.ShapeDtypeStruct((m, d), F32)]
    out_specs = [row]
    if with_bf16:
        out_shape.append(jax.ShapeDtypeStruct((m, d), BF16))
        out_specs.append(row)
    return pl.pallas_call(
        _ln_kernel,
        grid=(m // br,),
        in_specs=[row, row, vec, vec],
        out_specs=out_specs,
        out_shape=out_shape,
        compiler_params=_cparams(("parallel",)),
        name="deepnorm_ln",
    )(x, mixed, g.reshape(1, d), b.reshape(1, d))


def _attn_prompt_kernel(sink_ref, q_ref, kc_ref, kp_ref, vc_ref, vp_ref, b_ref,
                        cc_ref, sc_ref, cp_ref, sp_ref, o_ref, ko_ref, vo_ref):
    i = pl.program_id(0)
    w = WINDOW
    cos_c, sin_c = cc_ref[...], sc_ref[...]
    cos_p, sin_p = cp_ref[...], sp_ref[...]
    n_kchunk = KV_WIDTH // LANES

    k_cur, k_prev, v_cur, v_prev = [], [], [], []
    for c in range(n_kchunk):
        sl = slice(c * LANES, (c + 1) * LANES)
        bk = b_ref[:, ATTN_WIDTH + c * LANES:ATTN_WIDTH + (c + 1) * LANES]
        bv = b_ref[:, ATTN_WIDTH + KV_WIDTH + c * LANES:ATTN_WIDTH + KV_WIDTH + (c + 1) * LANES]
        kc = _rope(kc_ref[:, sl] + bk, cos_c, sin_c)
        kp = _rope(kp_ref[:, sl] + bk, cos_p, sin_p)
        vc = vc_ref[:, sl] + bv
        vp = vp_ref[:, sl] + bv
        ko_ref[:, sl] = kc
        vo_ref[:, sl] = vc
        k_cur.append(kc)
        k_prev.append(kp)
        v_cur.append(vc)
        v_prev.append(vp)

    rows = ATTN_GROUP * w
    r = lax.broadcasted_iota(jnp.int32, (rows, 2 * w), 0) % w
    col = lax.broadcasted_iota(jnp.int32, (rows, 2 * w), 1)
    valid = (col > r) & (col <= r + w) & ((col >= w) | (i > 0))
    row_id = lax.broadcasted_iota(jnp.int32, (rows, 1), 0)
    lane_k = lax.broadcasted_iota(jnp.int32, (2 * w, LANES), 1)
    lane_o = lax.broadcasted_iota(jnp.int32, (w, LANES), 1)
    scale = ATTN_HEAD_DIM ** -0.5

    for hk in range(ATTN_KV_HEADS):
        kchunk, khalf = hk // 2, hk % 2
        k2 = jnp.concatenate([k_prev[kchunk], k_cur[kchunk]], axis=0)
        in_half = (lane_k // ATTN_HEAD_DIM) == khalf
        k2 = jnp.where(in_half, k2, 0.0).astype(BF16)
        v2 = jnp.concatenate([v_prev[kchunk], v_cur[kchunk]], axis=0).astype(BF16)
        qs = []
        for g in range(ATTN_GROUP):
            hq = hk * ATTN_GROUP + g
            qchunk, qhalf = hq // 2, hq % 2
            sl = slice(qchunk * LANES, (qchunk + 1) * LANES)
            qc = _rope(q_ref[:, sl] + b_ref[:, sl], cos_c, sin_c) * scale
            if qhalf != khalf:
                qc = pltpu.roll(qc, ATTN_HEAD_DIM, 1)
            qs.append(qc)
        q4 = jnp.concatenate(qs, axis=0).astype(BF16)
        s = _dot_nt(q4, k2)
        s = jnp.where(valid, s, -jnp.inf)
        sink = jnp.zeros((rows, 1), F32)
        for g in range(ATTN_GROUP):
            sink = jnp.where(row_id // w == g, sink_ref[hk * ATTN_GROUP + g], sink)
        m = jnp.maximum(jnp.max(s, axis=-1, keepdims=True), sink)
        e = jnp.exp(s - m)
        den = jnp.sum(e, axis=-1, keepdims=True) + jnp.exp(sink - m)
        p = (e / den).astype(BF16)
        pv = _dot(p, v2)
        outs = []
        for g in range(ATTN_GROUP):
            og = pv[g * w:(g + 1) * w, :]
            if (g % 2) != khalf:
                og = pltpu.roll(og, ATTN_HEAD_DIM, 1)
            outs.append(og)
        for j in range(ATTN_GROUP // 2):
            oc = jnp.where(lane_o < ATTN_HEAD_DIM, outs[2 * j], outs[2 * j + 1])
            c = hk * (ATTN_GROUP // 2) + j
            o_ref[:, c * LANES:(c + 1) * LANES] = oc.astype(o_ref.dtype)


def _attn_prompt(proj, b_attn, sinks, cos_t, sin_t):
    nb = SEQ // WINDOW
    kcol = ATTN_WIDTH // KV_WIDTH
    prev = lambda i: jnp.maximum(i - 1, 0)
    return pl.pallas_call(
        _attn_prompt_kernel,
        grid=(nb,),
        in_specs=[pl.BlockSpec(memory_space=pltpu.SMEM),
                  pl.BlockSpec((WINDOW, ATTN_WIDTH), lambda i: (i, 0)),
                  pl.BlockSpec((WINDOW, KV_WIDTH), lambda i: (i, kcol)),
                  pl.BlockSpec((WINDOW, KV_WIDTH), lambda i: (prev(i), kcol)),
                  pl.BlockSpec((WINDOW, KV_WIDTH), lambda i: (i, kcol + 1)),
                  pl.BlockSpec((WINDOW, KV_WIDTH), lambda i: (prev(i), kcol + 1)),
                  pl.BlockSpec((1, ATTN_QKV_WIDTH), lambda i: (0, 0)),
                  pl.BlockSpec((WINDOW, LANES), lambda i: (i, 0)),
                  pl.BlockSpec((WINDOW, LANES), lambda i: (i, 0)),
                  pl.BlockSpec((WINDOW, LANES), lambda i: (prev(i), 0)),
                  pl.BlockSpec((WINDOW, LANES), lambda i: (prev(i), 0))],
        out_specs=[pl.BlockSpec((WINDOW, ATTN_WIDTH), lambda i: (i, 0)),
                   pl.BlockSpec((WINDOW, KV_WIDTH), lambda i: (0, 0)),
                   pl.BlockSpec((WINDOW, KV_WIDTH), lambda i: (0, 0))],
        out_shape=[jax.ShapeDtypeStruct((SEQ, ATTN_WIDTH), BF16),
                   jax.ShapeDtypeStruct((WINDOW, KV_WIDTH), F32),
                   jax.ShapeDtypeStruct((WINDOW, KV_WIDTH), F32)],
        compiler_params=_cparams(("arbitrary",)),
        name="attn_prompt",
    )(sinks, proj, proj, proj, proj, proj, b_attn.reshape(1, ATTN_QKV_WIDTH),
      cos_t, sin_t, cos_t, sin_t)


ATTN_SB = 8


def _attn_sample_kernel(q_ref, k_ref, v_ref, bq_ref, bk_ref, bv_ref, cos_ref, sin_ref,
                        sink_ref, e_ref, et_ref, ck_ref, cv_ref, o_ref, ko_ref, vo_ref):
    cos, sin = cos_ref[...], sin_ref[...]
    scale = ATTN_HEAD_DIM ** -0.5
    qr = []
    for c in range(ATTN_WIDTH // LANES):
        sl = slice(c * LANES, (c + 1) * LANES)
        qr.append(_rope(q_ref[:, sl] + bq_ref[:, sl], cos, sin) * scale)
    q = jnp.concatenate(qr, axis=1)
    kn = jnp.concatenate(
        [_rope(k_ref[:, c * LANES:(c + 1) * LANES] + bk_ref[:, c * LANES:(c + 1) * LANES], cos, sin)
         for c in range(KV_WIDTH // LANES)], axis=1)
    vn = v_ref[...] + bv_ref[...]
    lb = WINDOW
    for b in range(ATTN_SB):
        ko_ref[b, 0:lb - 1, :] = ck_ref[b, 1:lb, :]
        ko_ref[b, lb - 1:lb, :] = kn[b:b + 1, :]
        vo_ref[b, 0:lb - 1, :] = cv_ref[b, 1:lb, :]
        vo_ref[b, lb - 1:lb, :] = vn[b:b + 1, :]
        kb = ko_ref[b]
        vb = vo_ref[b]
        prod = jnp.concatenate(
            [kb * q[b:b + 1, g * KV_WIDTH:(g + 1) * KV_WIDTH] for g in range(ATTN_GROUP)], axis=0)
        s = _dot(prod.astype(BF16), e_ref[...])
        ps = []
        for g in range(ATTN_GROUP):
            sg = s[g * lb:(g + 1) * lb, :]
            sink = sink_ref[g:g + 1, :]
            m = jnp.maximum(jnp.max(sg, axis=0, keepdims=True), sink)
            e = jnp.exp(sg - m)
            den = jnp.sum(e, axis=0, keepdims=True) + jnp.exp(sink - m)
            ps.append(e / den)
        pe = _dot(jnp.concatenate(ps, axis=0).astype(BF16), et_ref[...])
        for g in range(ATTN_GROUP):
            og = jnp.sum(pe[g * lb:(g + 1) * lb, :] * vb, axis=0, keepdims=True)
            o_ref[b:b + 1, g * KV_WIDTH:(g + 1) * KV_WIDTH] = og


def _attn_sample(q_perm, k_new, v_new, bq_perm, bk, bv, cos_row, sin_row, sinks_gk, cache_k, cache_v):
    n = DEC_BATCH
    sb = ATTN_SB
    head_of_lane = jnp.arange(KV_WIDTH) // ATTN_HEAD_DIM
    e_mat = (head_of_lane[:, None] == jnp.arange(LANES)[None, :]).astype(BF16)
    row = lambda w: pl.BlockSpec((sb, w), lambda i: (i, 0))
    vec = lambda w: pl.BlockSpec((1, w), lambda i: (0, 0))
    full = lambda a: pl.BlockSpec(a.shape, lambda i: (0,) * a.ndim)
    cache = pl.BlockSpec((sb, WINDOW, KV_WIDTH), lambda i: (i, 0, 0))
    return pl.pallas_call(
        _attn_sample_kernel,
        grid=(n // sb,),
        in_specs=[row(ATTN_WIDTH), row(KV_WIDTH), row(KV_WIDTH),
                  vec(ATTN_WIDTH), vec(KV_WIDTH), vec(KV_WIDTH), vec(LANES), vec(LANES),
                  full(sinks_gk), full(e_mat), full(e_mat.T), cache, cache],
        out_specs=[row(ATTN_WIDTH), cache, cache],
        out_shape=[jax.ShapeDtypeStruct((n, ATTN_WIDTH), F32),
                   jax.ShapeDtypeStruct(cache_k.shape, F32),
                   jax.ShapeDtypeStruct(cache_v.shape, F32)],
        compiler_params=_cparams(("parallel",)),
        name="attn_sample",
    )(q_perm, k_new, v_new, bq_perm, bk, bv, cos_row, sin_row, sinks_gk, e_mat, e_mat.T,
      cache_k, cache_v)


DN_ROWS = 512
DN_PAIR = 2 * DN_CHUNK
CONV_PAD = 8


def _dn_gates_kernel(ab_ref, alog_ref, dtb_ref, cum_ref, beta_ref):
    ab = ab_ref[...]
    g = -jnp.exp(alog_ref[...]) * _softplus(ab + dtb_ref[...])
    row = lax.broadcasted_iota(jnp.int32, ab.shape, 0) % DN_CHUNK
    shift = 1
    while shift < DN_CHUNK:
        g = g + jnp.where(row >= shift, pltpu.roll(g, shift, 0), 0.0)
        shift *= 2
    cum_ref[...] = g
    beta_ref[...] = pltpu.roll(_sigmoid(ab), LANES - DN_HEADS, 1)


def _dn_gates(proj_ab, alog_pad, dtb_pad):
    blk = pl.BlockSpec((DN_ROWS, LANES), lambda i: (i, 0))
    vec = pl.BlockSpec((1, LANES), lambda i: (0, 0))
    out = jax.ShapeDtypeStruct((SEQ, LANES), F32)
    return pl.pallas_call(
        _dn_gates_kernel,
        grid=(SEQ // DN_ROWS,),
        in_specs=[blk, vec, vec],
        out_specs=[blk, blk],
        out_shape=[out, out],
        compiler_params=_cparams(("parallel",)),
        name="dn_gates",
    )(proj_ab, alog_pad, dtb_pad)


def _dn_prompt_kernel(q_ref, qp_ref, k_ref, kp_ref, v_ref, vp_ref, z_ref, cum_ref, beta_ref,
                      wq_ref, wk_ref, wv_ref, nw_ref, o_ref, so_ref, xq_s, xk_s, xv_s, s_s):
    h = pl.program_id(0)
    i = pl.program_id(1)
    pr = DN_PAIR
    cl = DN_CHUNK

    @pl.when(i == 0)
    def _():
        s_s[...] = jnp.zeros_like(s_s)

    for x_ref, prev_ref, pad_s in ((q_ref, qp_ref, xq_s), (k_ref, kp_ref, xk_s), (v_ref, vp_ref, xv_s)):
        pad_s[0:CONV_PAD, :] = jnp.where(i > 0, prev_ref[...], 0.0)
        pad_s[CONV_PAD:, :] = x_ref[...]

    def conv(pad_s, w_ref, r0):
        y = None
        for j in range(CONV_WIDTH):
            start = CONV_PAD + r0 - (CONV_WIDTH - 1 - j)
            term = w_ref[j:j + 1, :] * pad_s[start:start + pr, :]
            y = term if y is None else y + term
        return _silu(y)

    def l2n(x):
        return x * lax.rsqrt(jnp.sum(x * x, axis=-1, keepdims=True) + RMS_EPS)

    ri = lax.broadcasted_iota(jnp.int32, (pr, pr), 0)
    ci = lax.broadcasted_iota(jnp.int32, (pr, pr), 1)
    same_chunk = (ri // cl) == (ci // cl)
    causal = same_chunk & (ri >= ci)
    strict = same_chunk & (ri > ci)
    eye = (ri == ci).astype(F32)
    first_chunk = ri < cl
    pick = ci == h
    zeros_c = jnp.zeros((cl, DN_DIM), F32)
    nw = nw_ref[...]
    s = s_s[...]

    for a in range(DN_ROWS // pr):
        r0 = a * pr
        q_p = l2n(conv(xq_s, wq_ref, r0)) * (DN_DIM ** -0.5)
        k_p = l2n(conv(xk_s, wk_ref, r0))
        v_p = conv(xv_s, wv_ref, r0)
        cum = jnp.broadcast_to(
            jnp.sum(jnp.where(pick, cum_ref[r0:r0 + pr, :], 0.0), axis=1, keepdims=True), (pr, pr))
        beta = jnp.broadcast_to(
            jnp.sum(jnp.where(pick, beta_ref[r0:r0 + pr, :], 0.0), axis=1, keepdims=True), (pr, pr))
        gamma = jnp.exp(cum)
        decay = jnp.where(causal, jnp.exp(cum - cum.T), 0.0)
        k_b = k_p.astype(BF16)
        kq = _dot_nt(jnp.concatenate([k_p, q_p], axis=0).astype(BF16), k_b)
        x = jnp.where(strict, -(beta * kq[:pr] * decay), 0.0)
        att = kq[pr:] * decay
        t = eye + x
        xb = x.astype(BF16)
        xp = _dot(xb, xb)
        for _ in range(int(math.log2(cl)) - 2):
            xb = xp.astype(BF16)
            y = _dot(xb, jnp.concatenate([xb, t.astype(BF16)], axis=1))
            xp = y[:, :pr]
            t = t + y[:, pr:]
        t = t + _dot(xp.astype(BF16), t.astype(BF16))
        rhs = jnp.concatenate([k_p * (beta * gamma), v_p * beta], axis=1).astype(BF16)
        sol = _dot(t.astype(BF16), rhs)
        w_p, u_p = sol[:, :DN_DIM], sol[:, DN_DIM:]
        cum_last = jnp.where(first_chunk, cum[cl - 1:cl, :], cum[pr - 1:pr, :])
        k_dec_t = (k_p * jnp.exp(cum_last - cum)).T.astype(BF16)
        q_dec = q_p * gamma
        att_b = att.astype(BF16)
        for c in range(pr // cl):
            rs = slice(c * cl, (c + 1) * cl)
            ws = _dot(jnp.concatenate([w_p[rs], q_dec[rs]], axis=0).astype(BF16), s.astype(BF16))
            u_new = u_p[rs] - ws[:cl]
            u_full = jnp.concatenate([u_new, zeros_c] if c == 0 else [zeros_c, u_new], axis=0).astype(BF16)
            o = ws[cl:] + _dot(att_b[rs], u_full)
            s = jnp.exp(cum[(c + 1) * cl - 1:(c + 1) * cl, :]) * s + _dot(k_dec_t, u_full)
            o = o * lax.rsqrt(jnp.mean(o * o, axis=-1, keepdims=True) + RMS_EPS) * nw
            z = z_ref[r0 + c * cl:r0 + (c + 1) * cl, :]
            o_ref[r0 + c * cl:r0 + (c + 1) * cl, :] = (o * _silu(z)).astype(o_ref.dtype)

    s_s[...] = s

    @pl.when(i == pl.num_programs(1) - 1)
    def _():
        so_ref[0] = s


def _dn_prompt(proj, cum, beta, w_conv, norm_w):
    rows = DN_ROWS
    q0 = ATTN_QKV_WIDTH // LANES
    k0 = q0 + DN_HEADS
    v0 = k0 + DN_HEADS
    z0 = v0 + DN_HEADS
    per = rows // CONV_PAD

    def blk(c0):
        return pl.BlockSpec((rows, LANES), lambda h, i: (i, c0 + h))

    def prev(c0):
        return pl.BlockSpec((CONV_PAD, LANES), lambda h, i: (jnp.maximum(i * per - 1, 0), c0 + h))

    def wblk(c0):
        return pl.BlockSpec((CONV_WIDTH, LANES), lambda h, i: (0, c0 + h))

    gate = pl.BlockSpec((rows, LANES), lambda h, i: (i, 0))
    vec = pl.BlockSpec((1, LANES), lambda h, i: (0, 0))
    pad = pltpu.VMEM((rows + CONV_PAD, LANES), F32)
    return pl.pallas_call(
        _dn_prompt_kernel,
        grid=(DN_HEADS, SEQ // rows),
        in_specs=[blk(q0), prev(q0), blk(k0), prev(k0), blk(v0), prev(v0), blk(z0), gate, gate,
                  wblk(0), wblk(DN_HEADS), wblk(2 * DN_HEADS), vec],
        out_specs=[pl.BlockSpec((rows, LANES), lambda h, i: (i, h)),
                   pl.BlockSpec((1, DN_DIM, DN_DIM), lambda h, i: (h, 0, 0))],
        out_shape=[jax.ShapeDtypeStruct((SEQ, DN_WIDTH), BF16),
                   jax.ShapeDtypeStruct((DN_HEADS, DN_DIM, DN_DIM), F32)],
        scratch_shapes=[pad, pad, pad, pltpu.VMEM((DN_DIM, DN_DIM), F32)],
        compiler_params=_cparams(("parallel", "arbitrary")),
        name="dn_prompt",
    )(proj, proj, proj, proj, proj, proj, proj, cum, beta, w_conv, w_conv, w_conv,
      norm_w.reshape(1, DN_DIM))


DN_SB = 8


def _dn_gates_of_head(ab, alog, dtb, h):
    lane = lax.broadcasted_iota(jnp.int32, ab.shape, 1)
    g_all = -jnp.exp(alog) * _softplus(ab + dtb)
    g = jnp.sum(jnp.where(lane == h, g_all, 0.0), axis=1, keepdims=True)
    beta = jnp.sum(jnp.where(lane == h + DN_HEADS, _sigmoid(ab), 0.0), axis=1, keepdims=True)
    return jnp.broadcast_to(g, ab.shape), jnp.broadcast_to(beta, ab.shape)


def _dn_sample_prep_kernel(x_ref, sc_ref, ab_ref, w_ref, alog_ref, dtb_ref,
                           qt_ref, kt_ref, vb_ref, bg_ref, gam_ref):
    ab = ab_ref[...]
    alog, dtb = alog_ref[...], dtb_ref[...]

    def conv(col):
        sl = slice(col, col + LANES)
        y = w_ref[CONV_WIDTH - 1:CONV_WIDTH, sl] * x_ref[:, sl]
        for j in range(CONV_WIDTH - 1):
            y = y + w_ref[j:j + 1, sl] * sc_ref[:, j * CONV_DIM + col:j * CONV_DIM + col + LANES]
        return _silu(y)

    def l2n(x):
        return x * lax.rsqrt(jnp.sum(x * x, axis=-1, keepdims=True) + RMS_EPS)

    for h in range(DN_HEADS):
        q = l2n(conv(h * DN_DIM)) * (DN_DIM ** -0.5)
        k = l2n(conv(DN_WIDTH + h * DN_DIM))
        v = conv(2 * DN_WIDTH + h * DN_DIM)
        g_b, beta_b = _dn_gates_of_head(ab, alog, dtb, h)
        gamma = jnp.exp(g_b)
        qt_ref[h] = q.T
        kt_ref[h] = k.T
        vb_ref[h] = v * beta_b
        bg_ref[h] = beta_b * gamma
        gam_ref[h] = gamma


def _dn_sample_prep(qkv_s, conv_state, ab_s, w_conv, alog_pad, dtb_pad):
    n = DEC_BATCH
    full = lambda a: pl.BlockSpec(a.shape, lambda i: (0,) * a.ndim)
    hm = jax.ShapeDtypeStruct((DN_HEADS, n, DN_DIM), F32)
    hm_spec = pl.BlockSpec((DN_HEADS, n, DN_DIM), lambda i: (0, 0, 0))
    args = (qkv_s, conv_state, ab_s, w_conv, alog_pad, dtb_pad)
    return pl.pallas_call(
        _dn_sample_prep_kernel,
        grid=(1,),
        in_specs=[full(a) for a in args],
        out_specs=[hm_spec] * 5,
        out_shape=[hm] * 5,
        compiler_params=_cparams(("arbitrary",)),
        name="dn_sample_prep",
    )(*args)


def _dn_sample_kernel(s_ref, qt_ref, kt_ref, vb_ref, bg_ref, gam_ref, z_ref, nw_ref,
                      so_ref, o_ref, o_s):
    step = pl.program_id(0)
    lane = lax.broadcasted_iota(jnp.int32, (DN_DIM, DEC_BATCH), 1)

    def head(h, carry):
        kt = kt_ref[h]
        qt = qt_ref[h]
        for bb in range(DN_SB):
            b = step * DN_SB + bb
            pick = lane == b
            kcol = jnp.sum(jnp.where(pick, kt, 0.0), axis=1, keepdims=True)
            qcol = jnp.sum(jnp.where(pick, qt, 0.0), axis=1, keepdims=True)
            s = s_ref[bb, h]
            ks = jnp.sum(kcol * s, axis=0, keepdims=True)
            u = vb_ref[h, pl.ds(b, 1), :] - bg_ref[h, pl.ds(b, 1), :] * ks
            s_new = gam_ref[h, pl.ds(b, 1), :] * s + kcol * u
            so_ref[bb, h] = s_new
            o_s[h, pl.ds(bb, 1), :] = jnp.sum(qcol * s_new, axis=0, keepdims=True)
        return carry

    lax.fori_loop(0, DN_HEADS, head, 0)
    nw = nw_ref[...]
    for h in range(DN_HEADS):
        o = o_s[h]
        o = o * lax.rsqrt(jnp.mean(o * o, axis=-1, keepdims=True) + RMS_EPS) * nw
        sl = slice(h * DN_DIM, (h + 1) * DN_DIM)
        o_ref[:, sl] = o * _silu(z_ref[:, sl])


def _dn_sample(state, qt, kt, vb, bg, gam, z_s, norm_w):
    n = DEC_BATCH
    sb = DN_SB
    sblk = pl.BlockSpec((sb, DN_HEADS, DN_DIM, DN_DIM), lambda i: (i, 0, 0, 0))
    hm = pl.BlockSpec((DN_HEADS, n, DN_DIM), lambda i: (0, 0, 0))
    row = pl.BlockSpec((sb, DN_WIDTH), lambda i: (i, 0))
    return pl.pallas_call(
        _dn_sample_kernel,
        grid=(n // sb,),
        in_specs=[sblk, hm, hm, hm, hm, hm, row, pl.BlockSpec((1, DN_DIM), lambda i: (0, 0))],
        out_specs=[sblk, row],
        out_shape=[jax.ShapeDtypeStruct(state.shape, F32), jax.ShapeDtypeStruct((n, DN_WIDTH), F32)],
        scratch_shapes=[pltpu.VMEM((DN_HEADS, sb, DN_DIM), F32)],
        compiler_params=_cparams(("parallel",)),
        name="dn_sample",
    )(state, qt, kt, vb, bg, gam, z_s, norm_w.reshape(1, DN_DIM))


BM = 1040
BN = 1024
BF_FFN = 256
BR_LN = 320


def _rope_tables(pos):
    half = ATTN_HEAD_DIM // 2
    inv_freq = ROPE_THETA ** (-jnp.arange(half, dtype=F32) / half)
    ang = pos.astype(F32)[:, None] * inv_freq
    cos, sin = jnp.cos(ang), jnp.sin(ang)
    reps = LANES // ATTN_HEAD_DIM
    return jnp.tile(cos, (1, 2 * reps)), jnp.tile(jnp.concatenate([-sin, sin], axis=1), (1, reps))


def _group_major(a):
    lead = a.shape[:-1]
    a = a.reshape(lead + (ATTN_KV_HEADS, ATTN_GROUP, ATTN_HEAD_DIM))
    return jnp.swapaxes(a, -3, -2).reshape(lead + (ATTN_WIDTH,))


def _layer(x32, cache_k, cache_v, state_conv, state_delta, w_in, b_attn, attn_sinks, w_conv,
           dn_a_log, dn_dt_bias, dn_norm_w, w_out, ln1_g, ln1_b, w_gate, w_up, w_down, ln2_g, ln2_b):
    n_s = DEC_BATCH
    xb = x32.astype(BF16)
    w_main = _col_blocks(w_in[:, :MAIN_WIDTH], BN).astype(BF16)
    w_ab = jnp.pad(w_in[:, MAIN_WIDTH:], ((0, 0), (0, LANES - 2 * DN_HEADS))).astype(BF16)
    proj = _matmul(xb, w_main, BM)
    proj_ab = _matmul(xb, w_ab[None], BM)

    pad16 = lambda v: jnp.pad(v, (0, LANES - DN_HEADS)).reshape(1, LANES)
    alog_pad, dtb_pad = pad16(dn_a_log), pad16(dn_dt_bias)

    cos_t, sin_t = _rope_tables(jnp.arange(SEQ, dtype=jnp.int32))
    attn_p, pk, pv = _attn_prompt(proj, b_attn, attn_sinks, cos_t, sin_t)
    cum, beta = _dn_gates(proj_ab, alog_pad, dtb_pad)
    dn_p, ps = _dn_prompt(proj, cum, beta, w_conv, dn_norm_w)
    pc = proj[SEQ - (CONV_WIDTH - 1):SEQ, ATTN_QKV_WIDTH:ATTN_QKV_WIDTH + CONV_DIM]

    proj_s = proj[SEQ:]
    cos_r, sin_r = _rope_tables(jnp.full((1,), PAST_LEN, jnp.int32))
    sinks_gk = jnp.pad(attn_sinks.reshape(ATTN_KV_HEADS, ATTN_GROUP).T,
                       ((0, 0), (0, LANES - ATTN_KV_HEADS)))
    attn_s_perm, sk, sv = _attn_sample(
        _group_major(proj_s[:, :ATTN_WIDTH]),
        proj_s[:, ATTN_WIDTH:ATTN_WIDTH + KV_WIDTH],
        proj_s[:, ATTN_WIDTH + KV_WIDTH:ATTN_QKV_WIDTH],
        _group_major(b_attn[:ATTN_WIDTH]).reshape(1, ATTN_WIDTH),
        b_attn[ATTN_WIDTH:ATTN_WIDTH + KV_WIDTH].reshape(1, KV_WIDTH),
        b_attn[ATTN_WIDTH + KV_WIDTH:].reshape(1, KV_WIDTH),
        cos_r, sin_r, sinks_gk, cache_k.reshape(n_s, WINDOW, KV_WIDTH),
        cache_v.reshape(n_s, WINDOW, KV_WIDTH))
    attn_s = jnp.swapaxes(attn_s_perm.reshape(n_s, ATTN_GROUP, ATTN_KV_HEADS, ATTN_HEAD_DIM), 1, 2)
    attn_s = attn_s.reshape(n_s, ATTN_WIDTH)

    qkv_s = proj_s[:, ATTN_QKV_WIDTH:ATTN_QKV_WIDTH + CONV_DIM]
    z_s = proj_s[:, ATTN_QKV_WIDTH + CONV_DIM:]
    qt, kt, vb, bg, gam = _dn_sample_prep(
        qkv_s, state_conv.reshape(n_s, (CONV_WIDTH - 1) * CONV_DIM), proj_ab[SEQ:],
        w_conv, alog_pad, dtb_pad)
    ss, dn_s = _dn_sample(state_delta, qt, kt, vb, bg, gam, z_s, dn_norm_w)
    sc = jnp.concatenate([state_conv[:, 1:], qkv_s[:, None, :]], axis=1)

    attn_all = jnp.concatenate([attn_p, attn_s.astype(BF16)], axis=0)
    dn_all = jnp.concatenate([dn_p, dn_s.astype(BF16)], axis=0)
    mixed = _out_proj(attn_all, dn_all, _col_blocks(w_out, BN).astype(BF16), BM)
    h32, hb = _deepnorm_ln(x32, mixed, ln1_g, ln1_b, BR_LN, True)
    w_gu = jnp.concatenate([_col_blocks(w_gate, BF_FFN), _col_blocks(w_up, BF_FFN)], axis=2).astype(BF16)
    w_d = w_down.reshape(FFN_HIDDEN // BF_FFN, BF_FFN, D_MODEL).astype(BF16)
    ffn = _ffn(hb, w_gu, w_d, BM)
    (y,) = _deepnorm_ln(h32, ffn, ln2_g, ln2_b, BR_LN, False)
    return (y, pk.reshape(WINDOW, ATTN_KV_HEADS, ATTN_HEAD_DIM), pv.reshape(WINDOW, ATTN_KV_HEADS, ATTN_HEAD_DIM),
            pc, ps, sk.reshape(n_s, WINDOW, ATTN_KV_HEADS, ATTN_HEAD_DIM),
            sv.reshape(n_s, WINDOW, ATTN_KV_HEADS, ATTN_HEAD_DIM), sc, ss)


def kernel(x_prompt, x_sample, cache_swa_k, cache_swa_v, state_conv, state_delta, w_in, b_attn,
           attn_sinks, w_conv, dn_a_log, dn_dt_bias, dn_norm_w, w_out, ln1_g, ln1_b, w_gate, w_up,
           w_down, ln2_g, ln2_b):
    assert x_prompt.shape == (1, SEQ, D_MODEL) and x_sample.shape == (DEC_BATCH, 1, D_MODEL)
    assert w_in.shape[0] == 1, "one layer"
    x32 = jnp.concatenate([x_prompt[0], x_sample[:, 0]], axis=0)
    y, pk, pv, pc, ps, sk, sv, sc, ss = _layer(
        x32, cache_swa_k[0], cache_swa_v[0], state_conv[0], state_delta[0], w_in[0], b_attn[0],
        attn_sinks[0], w_conv[0], dn_a_log[0], dn_dt_bias[0], dn_norm_w[0], w_out[0], ln1_g[0],
        ln1_b[0], w_gate[0], w_up[0], w_down[0], ln2_g[0], ln2_b[0])
    return (y[:SEQ][None], y[SEQ:][:, None], pk[None, None], pv[None, None], pc[None, None],
            ps[None, None], sk[None], sv[None], sc[None], ss[None])
```

```python
import math

import jax
import jax.numpy as jnp
from jax import lax
from jax.experimental import pallas as pl
from jax.experimental.pallas import tpu as pltpu

D_MODEL = 4096
SEQ = 8192
DEC_BATCH = 128
PAST_LEN = 8192
ROWS = SEQ + DEC_BATCH

ATTN_HEADS = 32
ATTN_KV_HEADS = 8
ATTN_HEAD_DIM = 64
ATTN_GROUP = ATTN_HEADS // ATTN_KV_HEADS
ATTN_WIDTH = ATTN_HEADS * ATTN_HEAD_DIM
KV_WIDTH = ATTN_KV_HEADS * ATTN_HEAD_DIM
WINDOW = 128
ROPE_THETA = 10000.0
DN_HEADS = 16
DN_DIM = 128
DN_WIDTH = DN_HEADS * DN_DIM
CONV_WIDTH = 4
CONV_DIM = 3 * DN_WIDTH
DN_CHUNK = 64
ATTN_QKV_WIDTH = ATTN_WIDTH + 2 * KV_WIDTH
MAIN_WIDTH = ATTN_QKV_WIDTH + CONV_DIM + DN_WIDTH
FFN_HIDDEN = 11008
DEEPNORM_ALPHA = 2.0 ** 0.25
LN_EPS = 1e-5
RMS_EPS = 1e-6

LANES = 128
VMEM_LIMIT = 56 * 1024 * 1024

F32 = jnp.float32
BF16 = jnp.bfloat16


def _cparams(sem, vmem=VMEM_LIMIT):
    return pltpu.CompilerParams(dimension_semantics=sem, vmem_limit_bytes=vmem)


def _dot(a, b):
    return jnp.dot(a, b, preferred_element_type=F32)


def _dot_nt(a, b):
    return lax.dot_general(a, b, (((1,), (1,)), ((), ())), preferred_element_type=F32)


def _sigmoid(x):
    return 1.0 / (1.0 + jnp.exp(-x))


def _silu(x):
    return x * _sigmoid(x)


def _softplus(x):
    return jnp.maximum(x, 0.0) + jnp.log(1.0 + jnp.exp(-jnp.abs(x)))


def _rope(x, cos, sin_signed):
    lane = lax.broadcasted_iota(jnp.int32, x.shape, 1)
    first_half = (lane % ATTN_HEAD_DIM) < (ATTN_HEAD_DIM // 2)
    partner = jnp.where(first_half, pltpu.roll(x, LANES - ATTN_HEAD_DIM // 2, 1),
                        pltpu.roll(x, ATTN_HEAD_DIM // 2, 1))
    return x * cos + partner * sin_signed


def _mm_kernel(x_ref, w_ref, o_ref):
    o_ref[...] = _dot(x_ref[...], w_ref[0]).astype(o_ref.dtype)


def _matmul(x, w_blocks, bm, out_dtype=F32):
    m, k = x.shape
    nb, _, bn = w_blocks.shape
    return pl.pallas_call(
        _mm_kernel,
        grid=(m // bm, nb),
        in_specs=[pl.BlockSpec((bm, k), lambda i, j: (i, 0)),
                  pl.BlockSpec((1, k, bn), lambda i, j: (j, 0, 0))],
        out_specs=pl.BlockSpec((bm, bn), lambda i, j: (i, j)),
        out_shape=jax.ShapeDtypeStruct((m, nb * bn), out_dtype),
        compiler_params=_cparams(("parallel", "parallel")),
        name="matmul",
    )(x, w_blocks)


def _out_proj_kernel(a_ref, d_ref, w_ref, o_ref):
    ka = a_ref.shape[1]
    o_ref[...] = _dot(a_ref[...], w_ref[0, :ka, :]) + _dot(d_ref[...], w_ref[0, ka:, :])


def _out_proj(a, d, w_blocks, bm):
    m, ka = a.shape
    kd = d.shape[1]
    nb, _, bn = w_blocks.shape
    return pl.pallas_call(
        _out_proj_kernel,
        grid=(m // bm, nb),
        in_specs=[pl.BlockSpec((bm, ka), lambda i, j: (i, 0)),
                  pl.BlockSpec((bm, kd), lambda i, j: (i, 0)),
                  pl.BlockSpec((1, ka + kd, bn), lambda i, j: (j, 0, 0))],
        out_specs=pl.BlockSpec((bm, bn), lambda i, j: (i, j)),
        out_shape=jax.ShapeDtypeStruct((m, nb * bn), F32),
        compiler_params=_cparams(("parallel", "parallel")),
        name="out_proj",
    )(a, d, w_blocks)


FFN_NCHUNK = 1024


def _ffn_kernel(h_ref, wgu_ref, wd_ref, o_ref):
    f = pl.program_id(1)
    bf = wd_ref.shape[1]

    @pl.when(f == 0)
    def _():
        o_ref[...] = jnp.zeros_like(o_ref)

    r = _dot(h_ref[...], wgu_ref[0])
    act = (_silu(r[:, :bf]) * r[:, bf:]).astype(BF16)
    for n in range(0, o_ref.shape[1], FFN_NCHUNK):
        o_ref[:, n:n + FFN_NCHUNK] += _dot(act, wd_ref[0, :, n:n + FFN_NCHUNK])


def _ffn(h, wgu_blocks, wd_blocks, bm):
    m, d = h.shape
    nf, _, bf2 = wgu_blocks.shape
    bf = bf2 // 2
    once = pl.Buffered(1)
    return pl.pallas_call(
        _ffn_kernel,
        grid=(m // bm, nf),
        in_specs=[pl.BlockSpec((bm, d), lambda i, f: (i, 0), pipeline_mode=once),
                  pl.BlockSpec((1, d, bf2), lambda i, f: (f, 0, 0)),
                  pl.BlockSpec((1, bf, d), lambda i, f: (f, 0, 0))],
        out_specs=pl.BlockSpec((bm, d), lambda i, f: (i, 0), pipeline_mode=once),
        out_shape=jax.ShapeDtypeStruct((m, d), F32),
        compiler_params=_cparams(("parallel", "arbitrary")),
        name="ffn",
    )(h, wgu_blocks, wd_blocks)


BR = WINDOW
NB_PROMPT = SEQ // BR


def _deepnorm(x, mixed, g, b):
    v = DEEPNORM_ALPHA * x + mixed
    mu = jnp.mean(v, axis=-1, keepdims=True)
    c = v - mu
    var = jnp.mean(c * c, axis=-1, keepdims=True)
    return c * lax.rsqrt(var + LN_EPS) * g + b


def _prompt_rows(width):
    return pl.BlockSpec((BR, width), lambda i: (jnp.minimum(i, NB_PROMPT - 1), 0))


def _decode_rows(width):
    return pl.BlockSpec((BR, width), lambda i: (0, 0))


def _xcast_kernel(xp_ref, xs_ref, o_ref):
    i = pl.program_id(0)
    o_ref[...] = jnp.where(i < NB_PROMPT, xp_ref[...], xs_ref[...]).astype(o_ref.dtype)


def _xcast(x_prompt, x_sample):
    d = x_prompt.shape[1]
    return pl.pallas_call(
        _xcast_kernel,
        grid=(NB_PROMPT + 1,),
        in_specs=[_prompt_rows(d), _decode_rows(d)],
        out_specs=pl.BlockSpec((BR, d), lambda i: (i, 0)),
        out_shape=jax.ShapeDtypeStruct((ROWS, d), BF16),
        compiler_params=_cparams(("arbitrary",)),
        name="xcast",
    )(x_prompt, x_sample)


def _ln1_kernel(xp_ref, xs_ref, m_ref, g_ref, b_ref, o_ref, ob_ref):
    i = pl.program_id(0)
    x = jnp.where(i < NB_PROMPT, xp_ref[...], xs_ref[...])
    y = _deepnorm(x, m_ref[...], g_ref[...], b_ref[...])
    o_ref[...] = y
    ob_ref[...] = y.astype(BF16)


def _ln1(x_prompt, x_sample, mixed, g, b):
    d = mixed.shape[1]
    row = pl.BlockSpec((BR, d), lambda i: (i, 0))
    vec = pl.BlockSpec((1, d), lambda i: (0, 0))
    return pl.pallas_call(
        _ln1_kernel,
        grid=(NB_PROMPT + 1,),
        in_specs=[_prompt_rows(d), _decode_rows(d), row, vec, vec],
        out_specs=[row, row],
        out_shape=[jax.ShapeDtypeStruct((ROWS, d), F32), jax.ShapeDtypeStruct((ROWS, d), BF16)],
        compiler_params=_cparams(("arbitrary",)),
        name="ln1",
    )(x_prompt, x_sample, mixed, g.reshape(1, d), b.reshape(1, d))


def _ln2_kernel(h_ref, f_ref, g_ref, b_ref, yp_ref, ys_ref):
    i = pl.program_id(0)
    y = _deepnorm(h_ref[...], f_ref[...], g_ref[...], b_ref[...])

    @pl.when(i < NB_PROMPT)
    def _():
        yp_ref[...] = y

    @pl.when(i >= NB_PROMPT)
    def _():
        ys_ref[...] = y


def _ln2(h32, ffn, g, b):
    d = h32.shape[1]
    row = pl.BlockSpec((BR, d), lambda i: (i, 0))
    vec = pl.BlockSpec((1, d), lambda i: (0, 0))
    return pl.pallas_call(
        _ln2_kernel,
        grid=(NB_PROMPT + 1,),
        in_specs=[row, row, vec, vec],
        out_specs=[_prompt_rows(d), _decode_rows(d)],
        out_shape=[jax.ShapeDtypeStruct((SEQ, d), F32), jax.ShapeDtypeStruct((DEC_BATCH, d), F32)],
        compiler_params=_cparams(("arbitrary",)),
        name="ln2",
    )(h32, ffn, g.reshape(1, d), b.reshape(1, d))


def _cast_kernel(*refs):
    o_ref = refs[-1]
    off = 0
    for r in refs[:-1]:
        width = r.shape[1]
        o_ref[0, :, off:off + width] = r[...].astype(o_ref.dtype)
        off += width


def _cast_col_blocks(ws, bn, n_blocks, bk):
    k = ws[0].shape[0]
    return pl.pallas_call(
        _cast_kernel,
        grid=(n_blocks, k // bk),
        in_specs=[pl.BlockSpec((bk, bn), lambda j, r: (r, j)) for _ in ws],
        out_specs=pl.BlockSpec((1, bk, bn * len(ws)), lambda j, r: (j, r, 0)),
        out_shape=jax.ShapeDtypeStruct((n_blocks, k, bn * len(ws)), BF16),
        compiler_params=_cparams(("parallel", "parallel")),
        name="cast_w",
    )(*ws)


def _attn_prompt_kernel(sink_ref, q_ref, kc_ref, kp_ref, vc_ref, vp_ref, b_ref,
                        cc_ref, sc_ref, cp_ref, sp_ref, o_ref, ko_ref, vo_ref):
    i = pl.program_id(0)
    w = WINDOW
    cos_c, sin_c = cc_ref[...], sc_ref[...]
    cos_p, sin_p = cp_ref[...], sp_ref[...]
    n_kchunk = KV_WIDTH // LANES

    k_cur, k_prev, v_cur, v_prev = [], [], [], []
    for c in range(n_kchunk):
        sl = slice(c * LANES, (c + 1) * LANES)
        bk = b_ref[:, ATTN_WIDTH + c * LANES:ATTN_WIDTH + (c + 1) * LANES]
        bv = b_ref[:, ATTN_WIDTH + KV_WIDTH + c * LANES:ATTN_WIDTH + KV_WIDTH + (c + 1) * LANES]
        kc = _rope(kc_ref[:, sl] + bk, cos_c, sin_c)
        kp = _rope(kp_ref[:, sl] + bk, cos_p, sin_p)
        vc = vc_ref[:, sl] + bv
        vp = vp_ref[:, sl] + bv
        ko_ref[:, sl] = kc
        vo_ref[:, sl] = vc
        k_cur.append(kc)
        k_prev.append(kp)
        v_cur.append(vc)
        v_prev.append(vp)

    rows = ATTN_GROUP * w
    r = lax.broadcasted_iota(jnp.int32, (rows, 2 * w), 0) % w
    col = lax.broadcasted_iota(jnp.int32, (rows, 2 * w), 1)
    valid = (col > r) & (col <= r + w) & ((col >= w) | (i > 0))
    row_id = lax.broadcasted_iota(jnp.int32, (rows, 1), 0)
    lane_k = lax.broadcasted_iota(jnp.int32, (2 * w, LANES), 1)
    lane_o = lax.broadcasted_iota(jnp.int32, (w, LANES), 1)
    scale = ATTN_HEAD_DIM ** -0.5

    for hk in range(ATTN_KV_HEADS):
        kchunk, khalf = hk // 2, hk % 2
        k2 = jnp.concatenate([k_prev[kchunk], k_cur[kchunk]], axis=0)
        in_half = (lane_k // ATTN_HEAD_DIM) == khalf
        k2 = jnp.where(in_half, k2, 0.0).astype(BF16)
        v2 = jnp.concatenate([v_prev[kchunk], v_cur[kchunk]], axis=0).astype(BF16)
        qs = []
        for g in range(ATTN_GROUP):
            hq = hk * ATTN_GROUP + g
            qchunk, qhalf = hq // 2, hq % 2
            sl = slice(qchunk * LANES, (qchunk + 1) * LANES)
            qc = _rope(q_ref[:, sl] + b_ref[:, sl], cos_c, sin_c) * scale
            if qhalf != khalf:
                qc = pltpu.roll(qc, ATTN_HEAD_DIM, 1)
            qs.append(qc)
        q4 = jnp.concatenate(qs, axis=0).astype(BF16)
        s = _dot_nt(q4, k2)
        s = jnp.where(valid, s, -jnp.inf)
        sink = jnp.zeros((rows, 1), F32)
        for g in range(ATTN_GROUP):
            sink = jnp.where(row_id // w == g, sink_ref[hk * ATTN_GROUP + g], sink)
        m = jnp.maximum(jnp.max(s, axis=-1, keepdims=True), sink)
        e = jnp.exp(s - m)
        den = jnp.sum(e, axis=-1, keepdims=True) + jnp.exp(sink - m)
        p = (e / den).astype(BF16)
        pv = _dot(p, v2)
        outs = []
        for g in range(ATTN_GROUP):
            og = pv[g * w:(g + 1) * w, :]
            if (g % 2) != khalf:
                og = pltpu.roll(og, ATTN_HEAD_DIM, 1)
            outs.append(og)
        for j in range(ATTN_GROUP // 2):
            oc = jnp.where(lane_o < ATTN_HEAD_DIM, outs[2 * j], outs[2 * j + 1])
            c = hk * (ATTN_GROUP // 2) + j
            o_ref[:, c * LANES:(c + 1) * LANES] = oc.astype(o_ref.dtype)


def _attn_prompt(proj, b_attn, sinks, cos_t, sin_t):
    nb = SEQ // WINDOW
    kcol = ATTN_WIDTH // KV_WIDTH
    prev = lambda i: jnp.maximum(i - 1, 0)
    return pl.pallas_call(
        _attn_prompt_kernel,
        grid=(nb,),
        in_specs=[pl.BlockSpec(memory_space=pltpu.SMEM),
                  pl.BlockSpec((WINDOW, ATTN_WIDTH), lambda i: (i, 0)),
                  pl.BlockSpec((WINDOW, KV_WIDTH), lambda i: (i, kcol)),
                  pl.BlockSpec((WINDOW, KV_WIDTH), lambda i: (prev(i), kcol)),
                  pl.BlockSpec((WINDOW, KV_WIDTH), lambda i: (i, kcol + 1)),
                  pl.BlockSpec((WINDOW, KV_WIDTH), lambda i: (prev(i), kcol + 1)),
                  pl.BlockSpec((1, ATTN_QKV_WIDTH), lambda i: (0, 0)),
                  pl.BlockSpec((WINDOW, LANES), lambda i: (i, 0)),
                  pl.BlockSpec((WINDOW, LANES), lambda i: (i, 0)),
                  pl.BlockSpec((WINDOW, LANES), lambda i: (prev(i), 0)),
                  pl.BlockSpec((WINDOW, LANES), lambda i: (prev(i), 0))],
        out_specs=[pl.BlockSpec((WINDOW, ATTN_WIDTH), lambda i: (i, 0)),
                   pl.BlockSpec((WINDOW, KV_WIDTH), lambda i: (0, 0)),
                   pl.BlockSpec((WINDOW, KV_WIDTH), lambda i: (0, 0))],
        out_shape=[jax.ShapeDtypeStruct((SEQ, ATTN_WIDTH), BF16),
                   jax.ShapeDtypeStruct((WINDOW, KV_WIDTH), F32),
                   jax.ShapeDtypeStruct((WINDOW, KV_WIDTH), F32)],
        compiler_params=_cparams(("arbitrary",)),
        name="attn_prompt",
    )(sinks, proj, proj, proj, proj, proj, b_attn.reshape(1, ATTN_QKV_WIDTH),
      cos_t, sin_t, cos_t, sin_t)


ATTN_SB = 8


def _attn_sample_kernel(q_ref, k_ref, v_ref, bq_ref, bk_ref, bv_ref, cos_ref, sin_ref,
                        sink_ref, e_ref, et_ref, ck_ref, cv_ref, o_ref, ko_ref, vo_ref):
    cos, sin = cos_ref[...], sin_ref[...]
    scale = ATTN_HEAD_DIM ** -0.5
    qr = []
    for c in range(ATTN_WIDTH // LANES):
        sl = slice(c * LANES, (c + 1) * LANES)
        qr.append(_rope(q_ref[:, sl] + bq_ref[:, sl], cos, sin) * scale)
    q = jnp.concatenate(qr, axis=1)
    kn = jnp.concatenate(
        [_rope(k_ref[:, c * LANES:(c + 1) * LANES] + bk_ref[:, c * LANES:(c + 1) * LANES], cos, sin)
         for c in range(KV_WIDTH // LANES)], axis=1)
    vn = v_ref[...] + bv_ref[...]
    lb = WINDOW
    for b in range(ATTN_SB):
        ko_ref[b, 0:lb - 1, :] = ck_ref[b, 1:lb, :]
        ko_ref[b, lb - 1:lb, :] = kn[b:b + 1, :]
        vo_ref[b, 0:lb - 1, :] = cv_ref[b, 1:lb, :]
        vo_ref[b, lb - 1:lb, :] = vn[b:b + 1, :]
        kb = ko_ref[b]
        vb = vo_ref[b]
        prod = jnp.concatenate(
            [kb * q[b:b + 1, g * KV_WIDTH:(g + 1) * KV_WIDTH] for g in range(ATTN_GROUP)], axis=0)
        s = _dot(prod.astype(BF16), e_ref[...])
        ps = []
        for g in range(ATTN_GROUP):
            sg = s[g * lb:(g + 1) * lb, :]
            sink = sink_ref[g:g + 1, :]
            m = jnp.maximum(jnp.max(sg, axis=0, keepdims=True), sink)
            e = jnp.exp(sg - m)
            den = jnp.sum(e, axis=0, keepdims=True) + jnp.exp(sink - m)
            ps.append(e / den)
        pe = _dot(jnp.concatenate(ps, axis=0).astype(BF16), et_ref[...])
        for g in range(ATTN_GROUP):
            og = jnp.sum(pe[g * lb:(g + 1) * lb, :] * vb, axis=0, keepdims=True)
            o_ref[b:b + 1, g * KV_WIDTH:(g + 1) * KV_WIDTH] = og


def _attn_sample(q_perm, k_new, v_new, bq_perm, bk, bv, cos_row, sin_row, sinks_gk, cache_k, cache_v):
    n = DEC_BATCH
    sb = ATTN_SB
    head_of_lane = jnp.arange(KV_WIDTH) // ATTN_HEAD_DIM
    e_mat = (head_of_lane[:, None] == jnp.arange(LANES)[None, :]).astype(BF16)
    row = lambda w: pl.BlockSpec((sb, w), lambda i: (i, 0))
    vec = lambda w: pl.BlockSpec((1, w), lambda i: (0, 0))
    full = lambda a: pl.BlockSpec(a.shape, lambda i: (0,) * a.ndim)
    cache = pl.BlockSpec((sb, WINDOW, KV_WIDTH), lambda i: (i, 0, 0))
    return pl.pallas_call(
        _attn_sample_kernel,
        grid=(n // sb,),
        in_specs=[row(ATTN_WIDTH), row(KV_WIDTH), row(KV_WIDTH),
                  vec(ATTN_WIDTH), vec(KV_WIDTH), vec(KV_WIDTH), vec(LANES), vec(LANES),
                  full(sinks_gk), full(e_mat), full(e_mat.T), cache, cache],
        out_specs=[row(ATTN_WIDTH), cache, cache],
        out_shape=[jax.ShapeDtypeStruct((n, ATTN_WIDTH), F32),
                   jax.ShapeDtypeStruct(cache_k.shape, F32),
                   jax.ShapeDtypeStruct(cache_v.shape, F32)],
        compiler_params=_cparams(("parallel",)),
        name="attn_sample",
    )(q_perm, k_new, v_new, bq_perm, bk, bv, cos_row, sin_row, sinks_gk, e_mat, e_mat.T,
      cache_k, cache_v)


DN_ROWS = 512
DN_PAIR = 2 * DN_CHUNK
CONV_PAD = 8
DN_HPS = 2


def _dn_gates_kernel(ab_ref, alog_ref, dtb_ref, cum_ref, beta_ref):
    ab = ab_ref[...]
    g = -jnp.exp(alog_ref[...]) * _softplus(ab + dtb_ref[...])
    row = lax.broadcasted_iota(jnp.int32, ab.shape, 0) % DN_CHUNK
    shift = 1
    while shift < DN_CHUNK:
        g = g + jnp.where(row >= shift, pltpu.roll(g, shift, 0), 0.0)
        shift *= 2
    cum_ref[...] = g
    beta_ref[...] = pltpu.roll(_sigmoid(ab), LANES - DN_HEADS, 1)


def _dn_gates(proj_ab, alog_pad, dtb_pad):
    blk = pl.BlockSpec((DN_ROWS, LANES), lambda i: (i, 0))
    vec = pl.BlockSpec((1, LANES), lambda i: (0, 0))
    out = jax.ShapeDtypeStruct((SEQ, LANES), F32)
    return pl.pallas_call(
        _dn_gates_kernel,
        grid=(SEQ // DN_ROWS,),
        in_specs=[blk, vec, vec],
        out_specs=[blk, blk],
        out_shape=[out, out],
        compiler_params=_cparams(("parallel",)),
        name="dn_gates",
    )(proj_ab, alog_pad, dtb_pad)


def _dn_prompt_kernel(q_ref, qp_ref, k_ref, kp_ref, v_ref, vp_ref, z_ref, cum_ref, beta_ref,
                      wq_ref, wk_ref, wv_ref, nw_ref, o_ref, so_ref, xq_s, xk_s, xv_s, s_s):
    hp = pl.program_id(0)
    i = pl.program_id(1)
    pr = DN_PAIR
    cl = DN_CHUNK

    @pl.when(i == 0)
    def _():
        s_s[...] = jnp.zeros_like(s_s)

    for x_ref, prev_ref, pad_s in ((q_ref, qp_ref, xq_s), (k_ref, kp_ref, xk_s), (v_ref, vp_ref, xv_s)):
        pad_s[0:CONV_PAD, :] = jnp.where(i > 0, prev_ref[...], 0.0)
        pad_s[CONV_PAD:, :] = x_ref[...]

    def conv(pad_s, w_ref, r0, lanes):
        y = None
        for j in range(CONV_WIDTH):
            start = CONV_PAD + r0 - (CONV_WIDTH - 1 - j)
            term = w_ref[j:j + 1, lanes] * pad_s[start:start + pr, lanes]
            y = term if y is None else y + term
        return _silu(y)

    def l2n(x):
        return x * lax.rsqrt(jnp.sum(x * x, axis=-1, keepdims=True) + RMS_EPS)

    ri = lax.broadcasted_iota(jnp.int32, (pr, pr), 0)
    ci = lax.broadcasted_iota(jnp.int32, (pr, pr), 1)
    same_chunk = (ri // cl) == (ci // cl)
    causal = same_chunk & (ri >= ci)
    strict = same_chunk & (ri > ci)
    eye = (ri == ci).astype(F32)
    first_rows = ri < cl
    first_rows2 = lax.broadcasted_iota(jnp.int32, (pr, 2 * DN_DIM), 0) < cl
    nw = nw_ref[...]

    def phase_a(hh, r0):
        lanes = slice(hh * DN_DIM, (hh + 1) * DN_DIM)
        q_p = l2n(conv(xq_s, wq_ref, r0, lanes)) * (DN_DIM ** -0.5)
        k_p = l2n(conv(xk_s, wk_ref, r0, lanes))
        v_p = conv(xv_s, wv_ref, r0, lanes)
        pick = ci == hp * DN_HPS + hh
        cum = jnp.broadcast_to(
            jnp.sum(jnp.where(pick, cum_ref[r0:r0 + pr, :], 0.0), axis=1, keepdims=True), (pr, pr))
        beta = jnp.broadcast_to(
            jnp.sum(jnp.where(pick, beta_ref[r0:r0 + pr, :], 0.0), axis=1, keepdims=True), (pr, pr))
        gamma = jnp.exp(cum)
        decay = jnp.where(causal, jnp.exp(cum - cum.T), 0.0)
        kq = _dot_nt(jnp.concatenate([k_p, q_p], axis=0).astype(BF16), k_p.astype(BF16))
        x = jnp.where(strict, -(beta * kq[:pr] * decay), 0.0)
        att_b = (kq[pr:] * decay).astype(BF16)
        t = eye + x
        xb = x.astype(BF16)
        xp = _dot(xb, xb)
        for _ in range(int(math.log2(cl)) - 2):
            xb = xp.astype(BF16)
            y = _dot(xb, jnp.concatenate([xb, t.astype(BF16)], axis=1))
            xp = y[:, :pr]
            t = t + y[:, pr:]
        t = t + _dot(xp.astype(BF16), t.astype(BF16))
        rhs = jnp.concatenate([k_p * (beta * gamma), v_p * beta], axis=1).astype(BF16)
        wu = _dot(t.astype(BF16), rhs)
        awu = _dot(att_b, wu.astype(BF16))
        q_prime = (q_p * gamma - awu[:, :DN_DIM]).astype(BF16)
        au = awu[:, DN_DIM:]
        cum_last = jnp.where(first_rows, cum[cl - 1:cl, :], cum[pr - 1:pr, :])
        k_dec_t = (k_p * jnp.exp(cum_last - cum)).T.astype(BF16)
        kwu1 = _dot(k_dec_t, jnp.where(first_rows2, wu, 0.0).astype(BF16))
        kwu2 = _dot(k_dec_t, jnp.where(first_rows2, 0.0, wu).astype(BF16))
        mp1, n1 = -kwu1[:, :DN_DIM], kwu1[:, DN_DIM:]
        mp2, n2 = -kwu2[:, :DN_DIM], kwu2[:, DN_DIM:]
        g1 = jnp.exp(cum[cl - 1:cl, :])
        g2 = jnp.exp(cum[pr - 1:pr, :])
        comp = _dot(mp2.astype(BF16), jnp.concatenate([mp1, n1], axis=1).astype(BF16))
        mp12 = g2 * mp1 + g1 * mp2 + comp[:, :DN_DIM]
        n12 = g2 * n1 + comp[:, DN_DIM:] + n2
        lhs = jnp.concatenate([q_prime[:cl], mp1.astype(BF16), mp12.astype(BF16)], axis=0)
        return lhs, q_prime[cl:], au, n1, n12, g1, g1 * g2

    def phase_b(hh, r0, pa, s):
        lhs, q_prime2, au, n1, n12, g1, g12 = pa
        r = _dot(lhs, s.astype(BF16))
        s_mid = g1 * s + r[cl:cl + DN_DIM] + n1
        s_new = g12 * s + r[cl + DN_DIM:] + n12
        o = jnp.concatenate([r[:cl], _dot(q_prime2, s_mid.astype(BF16))], axis=0) + au
        o = o * lax.rsqrt(jnp.mean(o * o, axis=-1, keepdims=True) + RMS_EPS) * nw
        lanes = slice(hh * DN_DIM, (hh + 1) * DN_DIM)
        o_ref[r0:r0 + pr, lanes] = (o * _silu(z_ref[r0:r0 + pr, lanes])).astype(o_ref.dtype)
        return s_new

    s = [s_s[hh] for hh in range(DN_HPS)]
    for a in range(DN_ROWS // pr):
        r0 = a * pr
        pas = [phase_a(hh, r0) for hh in range(DN_HPS)]
        s = [phase_b(hh, r0, pas[hh], s[hh]) for hh in range(DN_HPS)]

    for hh in range(DN_HPS):
        s_s[hh] = s[hh]

    @pl.when(i == pl.num_programs(1) - 1)
    def _():
        for hh in range(DN_HPS):
            so_ref[hh] = s[hh]


def _dn_prompt(proj, cum, beta, w_conv, norm_w):
    rows = DN_ROWS
    width = DN_HPS * DN_DIM
    q0 = ATTN_QKV_WIDTH // width
    k0 = q0 + DN_WIDTH // width
    v0 = k0 + DN_WIDTH // width
    z0 = v0 + DN_WIDTH // width
    per = rows // CONV_PAD

    def blk(c0):
        return pl.BlockSpec((rows, width), lambda h, i: (i, c0 + h))

    def prev(c0):
        return pl.BlockSpec((CONV_PAD, width), lambda h, i: (jnp.maximum(i * per - 1, 0), c0 + h))

    def wblk(c0):
        return pl.BlockSpec((CONV_WIDTH, width), lambda h, i: (0, c0 + h))

    gate = pl.BlockSpec((rows, LANES), lambda h, i: (i, 0))
    vec = pl.BlockSpec((1, DN_DIM), lambda h, i: (0, 0))
    pad = pltpu.VMEM((rows + CONV_PAD, width), F32)
    return pl.pallas_call(
        _dn_prompt_kernel,
        grid=(DN_HEADS // DN_HPS, SEQ // rows),
        in_specs=[blk(q0), prev(q0), blk(k0), prev(k0), blk(v0), prev(v0), blk(z0), gate, gate,
                  wblk(0), wblk(DN_WIDTH // width), wblk(2 * DN_WIDTH // width), vec],
        out_specs=[pl.BlockSpec((rows, width), lambda h, i: (i, h)),
                   pl.BlockSpec((DN_HPS, DN_DIM, DN_DIM), lambda h, i: (h, 0, 0))],
        out_shape=[jax.ShapeDtypeStruct((SEQ, DN_WIDTH), BF16),
                   jax.ShapeDtypeStruct((DN_HEADS, DN_DIM, DN_DIM), F32)],
        scratch_shapes=[pad, pad, pad, pltpu.VMEM((DN_HPS, DN_DIM, DN_DIM), F32)],
        compiler_params=_cparams(("parallel", "arbitrary")),
        name="dn_prompt",
    )(proj, proj, proj, proj, proj, proj, proj, cum, beta, w_conv, w_conv, w_conv,
      norm_w.reshape(1, DN_DIM))


DN_SB = 8


def _dn_gates_of_head(ab, alog, dtb, h):
    lane = lax.broadcasted_iota(jnp.int32, ab.shape, 1)
    g_all = -jnp.exp(alog) * _softplus(ab + dtb)
    g = jnp.sum(jnp.where(lane == h, g_all, 0.0), axis=1, keepdims=True)
    beta = jnp.sum(jnp.where(lane == h + DN_HEADS, _sigmoid(ab), 0.0), axis=1, keepdims=True)
    return jnp.broadcast_to(g, ab.shape), jnp.broadcast_to(beta, ab.shape)


def _dn_sample_prep_kernel(x_ref, sc_ref, ab_ref, w_ref, alog_ref, dtb_ref,
                           qt_ref, kt_ref, vb_ref, bg_ref, gam_ref):
    ab = ab_ref[...]
    alog, dtb = alog_ref[...], dtb_ref[...]

    def conv(col):
        sl = slice(col, col + LANES)
        y = w_ref[CONV_WIDTH - 1:CONV_WIDTH, sl] * x_ref[:, sl]
        for j in range(CONV_WIDTH - 1):
            y = y + w_ref[j:j + 1, sl] * sc_ref[:, j * CONV_DIM + col:j * CONV_DIM + col + LANES]
        return _silu(y)

    def l2n(x):
        return x * lax.rsqrt(jnp.sum(x * x, axis=-1, keepdims=True) + RMS_EPS)

    for h in range(DN_HEADS):
        q = l2n(conv(h * DN_DIM)) * (DN_DIM ** -0.5)
        k = l2n(conv(DN_WIDTH + h * DN_DIM))
        v = conv(2 * DN_WIDTH + h * DN_DIM)
        g_b, beta_b = _dn_gates_of_head(ab, alog, dtb, h)
        gamma = jnp.exp(g_b)
        qt_ref[h] = q.T
        kt_ref[h] = k.T
        vb_ref[h] = v * beta_b
        bg_ref[h] = beta_b * gamma
        gam_ref[h] = gamma


def _dn_sample_prep(qkv_s, conv_state, ab_s, w_conv, alog_pad, dtb_pad):
    n = DEC_BATCH
    full = lambda a: pl.BlockSpec(a.shape, lambda i: (0,) * a.ndim)
    hm = jax.ShapeDtypeStruct((DN_HEADS, n, DN_DIM), F32)
    hm_spec = pl.BlockSpec((DN_HEADS, n, DN_DIM), lambda i: (0, 0, 0))
    args = (qkv_s, conv_state, ab_s, w_conv, alog_pad, dtb_pad)
    return pl.pallas_call(
        _dn_sample_prep_kernel,
        grid=(1,),
        in_specs=[full(a) for a in args],
        out_specs=[hm_spec] * 5,
        out_shape=[hm] * 5,
        compiler_params=_cparams(("arbitrary",)),
        name="dn_sample_prep",
    )(*args)


def _dn_sample_kernel(s_ref, qt_ref, kt_ref, vb_ref, bg_ref, gam_ref, z_ref, nw_ref,
                      so_ref, o_ref, o_s):
    step = pl.program_id(0)
    lane = lax.broadcasted_iota(jnp.int32, (DN_DIM, DEC_BATCH), 1)

    def head(h, carry):
        kt = kt_ref[h]
        qt = qt_ref[h]
        for bb in range(DN_SB):
            b = step * DN_SB + bb
            pick = lane == b
            kcol = jnp.sum(jnp.where(pick, kt, 0.0), axis=1, keepdims=True)
            qcol = jnp.sum(jnp.where(pick, qt, 0.0), axis=1, keepdims=True)
            s = s_ref[bb, h]
            ks = jnp.sum(kcol * s, axis=0, keepdims=True)
            u = vb_ref[h, pl.ds(b, 1), :] - bg_ref[h, pl.ds(b, 1), :] * ks
            s_new = gam_ref[h, pl.ds(b, 1), :] * s + kcol * u
            so_ref[bb, h] = s_new
            o_s[h, pl.ds(bb, 1), :] = jnp.sum(qcol * s_new, axis=0, keepdims=True)
        return carry

    lax.fori_loop(0, DN_HEADS, head, 0)
    nw = nw_ref[...]
    for h in range(DN_HEADS):
        o = o_s[h]
        o = o * lax.rsqrt(jnp.mean(o * o, axis=-1, keepdims=True) + RMS_EPS) * nw
        sl = slice(h * DN_DIM, (h + 1) * DN_DIM)
        o_ref[:, sl] = o * _silu(z_ref[:, sl])


def _dn_sample(state, qt, kt, vb, bg, gam, z_s, norm_w):
    n = DEC_BATCH
    sb = DN_SB
    sblk = pl.BlockSpec((sb, DN_HEADS, DN_DIM, DN_DIM), lambda i: (i, 0, 0, 0))
    hm = pl.BlockSpec((DN_HEADS, n, DN_DIM), lambda i: (0, 0, 0))
    row = pl.BlockSpec((sb, DN_WIDTH), lambda i: (i, 0))
    return pl.pallas_call(
        _dn_sample_kernel,
        grid=(n // sb,),
        in_specs=[sblk, hm, hm, hm, hm, hm, row, pl.BlockSpec((1, DN_DIM), lambda i: (0, 0))],
        out_specs=[sblk, row],
        out_shape=[jax.ShapeDtypeStruct(state.shape, F32), jax.ShapeDtypeStruct((n, DN_WIDTH), F32)],
        scratch_shapes=[pltpu.VMEM((DN_HEADS, sb, DN_DIM), F32)],
        compiler_params=_cparams(("parallel",)),
        name="dn_sample",
    )(state, qt, kt, vb, bg, gam, z_s, norm_w.reshape(1, DN_DIM))


BM = 1040
BN = 1024
BF_FFN = 256


def _rope_tables(pos):
    half = ATTN_HEAD_DIM // 2
    inv_freq = ROPE_THETA ** (-jnp.arange(half, dtype=F32) / half)
    ang = pos.astype(F32)[:, None] * inv_freq
    cos, sin = jnp.cos(ang), jnp.sin(ang)
    reps = LANES // ATTN_HEAD_DIM
    return jnp.tile(cos, (1, 2 * reps)), jnp.tile(jnp.concatenate([-sin, sin], axis=1), (1, reps))


def _group_major(a):
    lead = a.shape[:-1]
    a = a.reshape(lead + (ATTN_KV_HEADS, ATTN_GROUP, ATTN_HEAD_DIM))
    return jnp.swapaxes(a, -3, -2).reshape(lead + (ATTN_WIDTH,))


def _layer(x_prompt, x_sample, cache_k, cache_v, state_conv, state_delta, w_in, b_attn, attn_sinks,
           w_conv, dn_a_log, dn_dt_bias, dn_norm_w, w_out, ln1_g, ln1_b, w_gate, w_up, w_down,
           ln2_g, ln2_b):
    n_s = DEC_BATCH
    xb = _xcast(x_prompt, x_sample)
    w_main = _cast_col_blocks([w_in], BN, MAIN_WIDTH // BN, 1024)
    w_ab = jnp.pad(w_in[:, MAIN_WIDTH:], ((0, 0), (0, LANES - 2 * DN_HEADS))).astype(BF16)
    proj = _matmul(xb, w_main, BM)
    proj_ab = _matmul(xb, w_ab[None], BM)

    pad16 = lambda v: jnp.pad(v, (0, LANES - DN_HEADS)).reshape(1, LANES)
    alog_pad, dtb_pad = pad16(dn_a_log), pad16(dn_dt_bias)

    cos_t, sin_t = _rope_tables(jnp.arange(SEQ, dtype=jnp.int32))
    attn_p, pk, pv = _attn_prompt(proj, b_attn, attn_sinks, cos_t, sin_t)
    cum, beta = _dn_gates(proj_ab, alog_pad, dtb_pad)
    dn_p, ps = _dn_prompt(proj, cum, beta, w_conv, dn_norm_w)
    pc = proj[SEQ - (CONV_WIDTH - 1):SEQ, ATTN_QKV_WIDTH:ATTN_QKV_WIDTH + CONV_DIM]

    proj_s = proj[SEQ:]
    cos_r, sin_r = _rope_tables(jnp.full((1,), PAST_LEN, jnp.int32))
    sinks_gk = jnp.pad(attn_sinks.reshape(ATTN_KV_HEADS, ATTN_GROUP).T,
                       ((0, 0), (0, LANES - ATTN_KV_HEADS)))
    attn_s_perm, sk, sv = _attn_sample(
        _group_major(proj_s[:, :ATTN_WIDTH]),
        proj_s[:, ATTN_WIDTH:ATTN_WIDTH + KV_WIDTH],
        proj_s[:, ATTN_WIDTH + KV_WIDTH:ATTN_QKV_WIDTH],
        _group_major(b_attn[:ATTN_WIDTH]).reshape(1, ATTN_WIDTH),
        b_attn[ATTN_WIDTH:ATTN_WIDTH + KV_WIDTH].reshape(1, KV_WIDTH),
        b_attn[ATTN_WIDTH + KV_WIDTH:].reshape(1, KV_WIDTH),
        cos_r, sin_r, sinks_gk, cache_k.reshape(n_s, WINDOW, KV_WIDTH),
        cache_v.reshape(n_s, WINDOW, KV_WIDTH))
    attn_s = jnp.swapaxes(attn_s_perm.reshape(n_s, ATTN_GROUP, ATTN_KV_HEADS, ATTN_HEAD_DIM), 1, 2)
    attn_s = attn_s.reshape(n_s, ATTN_WIDTH)

    qkv_s = proj_s[:, ATTN_QKV_WIDTH:ATTN_QKV_WIDTH + CONV_DIM]
    z_s = proj_s[:, ATTN_QKV_WIDTH + CONV_DIM:]
    qt, kt, vb, bg, gam = _dn_sample_prep(
        qkv_s, state_conv.reshape(n_s, (CONV_WIDTH - 1) * CONV_DIM), proj_ab[SEQ:],
        w_conv, alog_pad, dtb_pad)
    ss, dn_s = _dn_sample(state_delta, qt, kt, vb, bg, gam, z_s, dn_norm_w)
    sc = jnp.concatenate([state_conv[:, 1:], qkv_s[:, None, :]], axis=1)

    attn_all = jnp.concatenate([attn_p, attn_s.astype(BF16)], axis=0)
    dn_all = jnp.concatenate([dn_p, dn_s.astype(BF16)], axis=0)
    mixed = _out_proj(attn_all, dn_all, _cast_col_blocks([w_out], BN, D_MODEL // BN, 1024), BM)
    h32, hb = _ln1(x_prompt, x_sample, mixed, ln1_g, ln1_b)
    w_gu = _cast_col_blocks([w_gate, w_up], BF_FFN, FFN_HIDDEN // BF_FFN, D_MODEL)
    w_d = _cast_col_blocks([w_down], D_MODEL, 1, BF_FFN).reshape(FFN_HIDDEN // BF_FFN, BF_FFN, D_MODEL)
    ffn = _ffn(hb, w_gu, w_d, BM)
    y_p, y_s = _ln2(h32, ffn, ln2_g, ln2_b)
    return (y_p, y_s, pk.reshape(WINDOW, ATTN_KV_HEADS, ATTN_HEAD_DIM),
            pv.reshape(WINDOW, ATTN_KV_HEADS, ATTN_HEAD_DIM), pc, ps,
            sk.reshape(n_s, WINDOW, ATTN_KV_HEADS, ATTN_HEAD_DIM),
            sv.reshape(n_s, WINDOW, ATTN_KV_HEADS, ATTN_HEAD_DIM), sc, ss)


def kernel(x_prompt, x_sample, cache_swa_k, cache_swa_v, state_conv, state_delta, w_in, b_attn,
           attn_sinks, w_conv, dn_a_log, dn_dt_bias, dn_norm_w, w_out, ln1_g, ln1_b, w_gate, w_up,
           w_down, ln2_g, ln2_b):
    assert x_prompt.shape == (1, SEQ, D_MODEL) and x_sample.shape == (DEC_BATCH, 1, D_MODEL)
    assert w_in.shape[0] == 1, "one layer"
    y_p, y_s, pk, pv, pc, ps, sk, sv, sc, ss = _layer(
        x_prompt[0], x_sample[:, 0], cache_swa_k[0], cache_swa_v[0], state_conv[0], state_delta[0],
        w_in[0], b_attn[0], attn_sinks[0], w_conv[0], dn_a_log[0], dn_dt_bias[0], dn_norm_w[0],
        w_out[0], ln1_g[0], ln1_b[0], w_gate[0], w_up[0], w_down[0], ln2_g[0], ln2_b[0])
    return (y_p[None], y_s[:, None], pk[None, None], pv[None, None], pc[None, None],
            ps[None, None], sk[None], sv[None], sc[None], ss[None])
```

```python
import math

import jax
import jax.numpy as jnp
from jax import lax
from jax.experimental import pallas as pl
from jax.experimental.pallas import tpu as pltpu

D_MODEL = 4096
SEQ = 8192
DEC_BATCH = 128
PAST_LEN = 8192
ROWS = SEQ + DEC_BATCH

ATTN_HEADS = 32
ATTN_KV_HEADS = 8
ATTN_HEAD_DIM = 64
ATTN_GROUP = ATTN_HEADS // ATTN_KV_HEADS
ATTN_WIDTH = ATTN_HEADS * ATTN_HEAD_DIM
KV_WIDTH = ATTN_KV_HEADS * ATTN_HEAD_DIM
WINDOW = 128
ROPE_THETA = 10000.0
DN_HEADS = 16
DN_DIM = 128
DN_WIDTH = DN_HEADS * DN_DIM
CONV_WIDTH = 4
CONV_DIM = 3 * DN_WIDTH
DN_CHUNK = 64
ATTN_QKV_WIDTH = ATTN_WIDTH + 2 * KV_WIDTH
MAIN_WIDTH = ATTN_QKV_WIDTH + CONV_DIM + DN_WIDTH
FFN_HIDDEN = 11008
DEEPNORM_ALPHA = 2.0 ** 0.25
LN_EPS = 1e-5
RMS_EPS = 1e-6

LANES = 128
VMEM_LIMIT = 56 * 1024 * 1024

F32 = jnp.float32
BF16 = jnp.bfloat16


def _cparams(sem, vmem=VMEM_LIMIT):
    return pltpu.CompilerParams(dimension_semantics=sem, vmem_limit_bytes=vmem)


def _dot(a, b):
    return jnp.dot(a, b, preferred_element_type=F32)


def _dot_nt(a, b):
    return lax.dot_general(a, b, (((1,), (1,)), ((), ())), preferred_element_type=F32)


def _sigmoid(x):
    return 1.0 / (1.0 + jnp.exp(-x))


def _silu(x):
    return x * _sigmoid(x)


def _softplus(x):
    return jnp.maximum(x, 0.0) + jnp.log(1.0 + jnp.exp(-jnp.abs(x)))


def _rope(x, cos, sin_signed):
    lane = lax.broadcasted_iota(jnp.int32, x.shape, 1)
    first_half = (lane % ATTN_HEAD_DIM) < (ATTN_HEAD_DIM // 2)
    partner = jnp.where(first_half, pltpu.roll(x, LANES - ATTN_HEAD_DIM // 2, 1),
                        pltpu.roll(x, ATTN_HEAD_DIM // 2, 1))
    return x * cos + partner * sin_signed


def _mm_nt_kernel(x_ref, wt_ref, o_ref):
    o_ref[...] = _dot_nt(x_ref[...], wt_ref[0]).astype(o_ref.dtype)


def _matmul_nt(x, wt_blocks, bm, out_dtype=F32):
    m, k = x.shape
    nb, bn, _ = wt_blocks.shape
    return pl.pallas_call(
        _mm_nt_kernel,
        grid=(m // bm, nb),
        in_specs=[pl.BlockSpec((bm, k), lambda i, j: (i, 0)),
                  pl.BlockSpec((1, bn, k), lambda i, j: (j, 0, 0))],
        out_specs=pl.BlockSpec((bm, bn), lambda i, j: (i, j)),
        out_shape=jax.ShapeDtypeStruct((m, nb * bn), out_dtype),
        compiler_params=_cparams(("parallel", "parallel")),
        name="matmul",
    )(x, wt_blocks)


def _out_proj_kernel(a_ref, d_ref, w_ref, o_ref):
    ka = a_ref.shape[1]
    o_ref[...] = _dot(a_ref[...], w_ref[0, :ka, :]) + _dot(d_ref[...], w_ref[0, ka:, :])


def _out_proj(a, d, w_blocks, bm):
    m, ka = a.shape
    kd = d.shape[1]
    nb, _, bn = w_blocks.shape
    return pl.pallas_call(
        _out_proj_kernel,
        grid=(m // bm, nb),
        in_specs=[pl.BlockSpec((bm, ka), lambda i, j: (i, 0)),
                  pl.BlockSpec((bm, kd), lambda i, j: (i, 0)),
                  pl.BlockSpec((1, ka + kd, bn), lambda i, j: (j, 0, 0))],
        out_specs=pl.BlockSpec((bm, bn), lambda i, j: (i, j)),
        out_shape=jax.ShapeDtypeStruct((m, nb * bn), F32),
        compiler_params=_cparams(("parallel", "parallel")),
        name="out_proj",
    )(a, d, w_blocks)


FFN_NCHUNK = 1024
FFN_ROW_ALIGN = 16


def _ffn_kernel(h_ref, wgu_ref, wd_ref, o_ref):
    f = pl.program_id(1)
    bf = wd_ref.shape[1]

    @pl.when(f == 0)
    def _():
        o_ref[...] = jnp.zeros_like(o_ref)

    bm = h_ref.shape[0]
    cut = (bm // 2) // FFN_ROW_ALIGN * FFN_ROW_ALIGN
    groups = ((0, cut), (cut, bm))
    rs = [_dot(h_ref[a:b, :], wgu_ref[0]) for a, b in groups]
    for (a, b), r in zip(groups, rs):
        act = (_silu(r[:, :bf]) * r[:, bf:]).astype(BF16)
        for n in range(0, o_ref.shape[1], FFN_NCHUNK):
            o_ref[a:b, n:n + FFN_NCHUNK] += _dot(act, wd_ref[0, :, n:n + FFN_NCHUNK])


def _ffn(h, wgu_blocks, wd_blocks, bm):
    m, d = h.shape
    nf, _, bf2 = wgu_blocks.shape
    bf = bf2 // 2
    once = pl.Buffered(1)
    return pl.pallas_call(
        _ffn_kernel,
        grid=(m // bm, nf),
        in_specs=[pl.BlockSpec((bm, d), lambda i, f: (i, 0), pipeline_mode=once),
                  pl.BlockSpec((1, d, bf2), lambda i, f: (f, 0, 0)),
                  pl.BlockSpec((1, bf, d), lambda i, f: (f, 0, 0))],
        out_specs=pl.BlockSpec((bm, d), lambda i, f: (i, 0), pipeline_mode=once),
        out_shape=jax.ShapeDtypeStruct((m, d), F32),
        compiler_params=_cparams(("parallel", "arbitrary")),
        name="ffn",
    )(h, wgu_blocks, wd_blocks)


BR = WINDOW
NB_PROMPT = SEQ // BR


def _deepnorm(x, mixed, g, b):
    v = DEEPNORM_ALPHA * x + mixed
    mu = jnp.mean(v, axis=-1, keepdims=True)
    c = v - mu
    var = jnp.mean(c * c, axis=-1, keepdims=True)
    return c * lax.rsqrt(var + LN_EPS) * g + b


def _prompt_rows(width):
    return pl.BlockSpec((BR, width), lambda i: (jnp.minimum(i, NB_PROMPT - 1), 0))


def _decode_rows(width):
    return pl.BlockSpec((BR, width), lambda i: (0, 0))


def _xcast_kernel(xp_ref, xs_ref, o_ref):
    i = pl.program_id(0)
    o_ref[...] = jnp.where(i < NB_PROMPT, xp_ref[...], xs_ref[...]).astype(o_ref.dtype)


def _xcast(x_prompt, x_sample):
    d = x_prompt.shape[1]
    return pl.pallas_call(
        _xcast_kernel,
        grid=(NB_PROMPT + 1,),
        in_specs=[_prompt_rows(d), _decode_rows(d)],
        out_specs=pl.BlockSpec((BR, d), lambda i: (i, 0)),
        out_shape=jax.ShapeDtypeStruct((ROWS, d), BF16),
        compiler_params=_cparams(("arbitrary",)),
        name="xcast",
    )(x_prompt, x_sample)


def _ln1_kernel(xp_ref, xs_ref, m_ref, g_ref, b_ref, o_ref, ob_ref):
    i = pl.program_id(0)
    x = jnp.where(i < NB_PROMPT, xp_ref[...], xs_ref[...])
    y = _deepnorm(x, m_ref[...], g_ref[...], b_ref[...])
    o_ref[...] = y
    ob_ref[...] = y.astype(BF16)


def _ln1(x_prompt, x_sample, mixed, g, b):
    d = mixed.shape[1]
    row = pl.BlockSpec((BR, d), lambda i: (i, 0))
    vec = pl.BlockSpec((1, d), lambda i: (0, 0))
    return pl.pallas_call(
        _ln1_kernel,
        grid=(NB_PROMPT + 1,),
        in_specs=[_prompt_rows(d), _decode_rows(d), row, vec, vec],
        out_specs=[row, row],
        out_shape=[jax.ShapeDtypeStruct((ROWS, d), F32), jax.ShapeDtypeStruct((ROWS, d), BF16)],
        compiler_params=_cparams(("arbitrary",)),
        name="ln1",
    )(x_prompt, x_sample, mixed, g.reshape(1, d), b.reshape(1, d))


def _ln2_kernel(h_ref, f_ref, g_ref, b_ref, yp_ref, ys_ref):
    i = pl.program_id(0)
    y = _deepnorm(h_ref[...], f_ref[...], g_ref[...], b_ref[...])

    @pl.when(i < NB_PROMPT)
    def _():
        yp_ref[...] = y

    @pl.when(i >= NB_PROMPT)
    def _():
        ys_ref[...] = y


def _ln2(h32, ffn, g, b):
    d = h32.shape[1]
    row = pl.BlockSpec((BR, d), lambda i: (i, 0))
    vec = pl.BlockSpec((1, d), lambda i: (0, 0))
    return pl.pallas_call(
        _ln2_kernel,
        grid=(NB_PROMPT + 1,),
        in_specs=[row, row, vec, vec],
        out_specs=[_prompt_rows(d), _decode_rows(d)],
        out_shape=[jax.ShapeDtypeStruct((SEQ, d), F32), jax.ShapeDtypeStruct((DEC_BATCH, d), F32)],
        compiler_params=_cparams(("arbitrary",)),
        name="ln2",
    )(h32, ffn, g.reshape(1, d), b.reshape(1, d))


def _cast_kernel(*refs):
    o_ref = refs[-1]
    off = 0
    for r in refs[:-1]:
        width = r.shape[1]
        o_ref[0, :, off:off + width] = r[...].astype(o_ref.dtype)
        off += width


def _cast_col_blocks(ws, bn, n_blocks, bk, rows=None):
    k = ws[0].shape[0] if rows is None else rows
    return pl.pallas_call(
        _cast_kernel,
        grid=(n_blocks, k // bk),
        in_specs=[pl.BlockSpec((bk, bn), lambda j, r: (r, j)) for _ in ws],
        out_specs=pl.BlockSpec((1, bk, bn * len(ws)), lambda j, r: (j, r, 0)),
        out_shape=jax.ShapeDtypeStruct((n_blocks, k, bn * len(ws)), BF16),
        compiler_params=_cparams(("parallel", "parallel")),
        name="cast_w",
    )(*ws)


def _attn_prompt_kernel(sink_ref, q_ref, kc_ref, kp_ref, vc_ref, vp_ref, b_ref,
                        cc_ref, sc_ref, cp_ref, sp_ref, o_ref, ko_ref, vo_ref):
    i = pl.program_id(0)
    w = WINDOW
    cos_c, sin_c = cc_ref[...], sc_ref[...]
    cos_p, sin_p = cp_ref[...], sp_ref[...]
    n_kchunk = KV_WIDTH // LANES

    k_cur, k_prev, v_cur, v_prev = [], [], [], []
    for c in range(n_kchunk):
        sl = slice(c * LANES, (c + 1) * LANES)
        bk = b_ref[:, ATTN_WIDTH + c * LANES:ATTN_WIDTH + (c + 1) * LANES]
        bv = b_ref[:, ATTN_WIDTH + KV_WIDTH + c * LANES:ATTN_WIDTH + KV_WIDTH + (c + 1) * LANES]
        kc = _rope(kc_ref[:, sl] + bk, cos_c, sin_c)
        kp = _rope(kp_ref[:, sl] + bk, cos_p, sin_p)
        vc = vc_ref[:, sl] + bv
        vp = vp_ref[:, sl] + bv
        ko_ref[:, sl] = kc
        vo_ref[:, sl] = vc
        k_cur.append(kc)
        k_prev.append(kp)
        v_cur.append(vc)
        v_prev.append(vp)

    rows = ATTN_GROUP * w
    r = lax.broadcasted_iota(jnp.int32, (rows, 2 * w), 0) % w
    col = lax.broadcasted_iota(jnp.int32, (rows, 2 * w), 1)
    valid = (col > r) & (col <= r + w) & ((col >= w) | (i > 0))
    row_id = lax.broadcasted_iota(jnp.int32, (rows, 1), 0)
    lane_k = lax.broadcasted_iota(jnp.int32, (2 * w, LANES), 1)
    lane_o = lax.broadcasted_iota(jnp.int32, (w, LANES), 1)
    scale = ATTN_HEAD_DIM ** -0.5

    k2s, v2s, q4s = [], [], []
    for hk in range(ATTN_KV_HEADS):
        kchunk, khalf = hk // 2, hk % 2
        k2 = jnp.concatenate([k_prev[kchunk], k_cur[kchunk]], axis=0)
        in_half = (lane_k // ATTN_HEAD_DIM) == khalf
        k2s.append(jnp.where(in_half, k2, 0.0).astype(BF16))
        v2s.append(jnp.concatenate([v_prev[kchunk], v_cur[kchunk]], axis=0).astype(BF16))
        qs = []
        for g in range(ATTN_GROUP):
            hq = hk * ATTN_GROUP + g
            qchunk, qhalf = hq // 2, hq % 2
            sl = slice(qchunk * LANES, (qchunk + 1) * LANES)
            qc = _rope(q_ref[:, sl] + b_ref[:, sl], cos_c, sin_c) * scale
            if qhalf != khalf:
                qc = pltpu.roll(qc, ATTN_HEAD_DIM, 1)
            qs.append(qc)
        q4s.append(jnp.concatenate(qs, axis=0).astype(BF16))
    scores = [_dot_nt(q4, k2) for q4, k2 in zip(q4s, k2s)]
    probs = []
    for hk, s in enumerate(scores):
        s = jnp.where(valid, s, -jnp.inf)
        sink = jnp.zeros((rows, 1), F32)
        for g in range(ATTN_GROUP):
            sink = jnp.where(row_id // w == g, sink_ref[hk * ATTN_GROUP + g], sink)
        m = jnp.maximum(jnp.max(s, axis=-1, keepdims=True), sink)
        e = jnp.exp(s - m)
        den = jnp.sum(e, axis=-1, keepdims=True) + jnp.exp(sink - m)
        probs.append((e / den).astype(BF16))
    pvs = [_dot(p, v2) for p, v2 in zip(probs, v2s)]
    for hk, pv in enumerate(pvs):
        khalf = hk % 2
        outs = []
        for g in range(ATTN_GROUP):
            og = pv[g * w:(g + 1) * w, :]
            if (g % 2) != khalf:
                og = pltpu.roll(og, ATTN_HEAD_DIM, 1)
            outs.append(og)
        for j in range(ATTN_GROUP // 2):
            oc = jnp.where(lane_o < ATTN_HEAD_DIM, outs[2 * j], outs[2 * j + 1])
            c = hk * (ATTN_GROUP // 2) + j
            o_ref[:, c * LANES:(c + 1) * LANES] = oc.astype(o_ref.dtype)


def _attn_prompt(proj, b_attn, sinks, cos_t, sin_t):
    nb = SEQ // WINDOW
    kcol = ATTN_WIDTH // KV_WIDTH
    prev = lambda i: jnp.maximum(i - 1, 0)
    return pl.pallas_call(
        _attn_prompt_kernel,
        grid=(nb,),
        in_specs=[pl.BlockSpec(memory_space=pltpu.SMEM),
                  pl.BlockSpec((WINDOW, ATTN_WIDTH), lambda i: (i, 0)),
                  pl.BlockSpec((WINDOW, KV_WIDTH), lambda i: (i, kcol)),
                  pl.BlockSpec((WINDOW, KV_WIDTH), lambda i: (prev(i), kcol)),
                  pl.BlockSpec((WINDOW, KV_WIDTH), lambda i: (i, kcol + 1)),
                  pl.BlockSpec((WINDOW, KV_WIDTH), lambda i: (prev(i), kcol + 1)),
                  pl.BlockSpec((1, ATTN_QKV_WIDTH), lambda i: (0, 0)),
                  pl.BlockSpec((WINDOW, LANES), lambda i: (i, 0)),
                  pl.BlockSpec((WINDOW, LANES), lambda i: (i, 0)),
                  pl.BlockSpec((WINDOW, LANES), lambda i: (prev(i), 0)),
                  pl.BlockSpec((WINDOW, LANES), lambda i: (prev(i), 0))],
        out_specs=[pl.BlockSpec((WINDOW, ATTN_WIDTH), lambda i: (i, 0)),
                   pl.BlockSpec((WINDOW, KV_WIDTH), lambda i: (0, 0)),
                   pl.BlockSpec((WINDOW, KV_WIDTH), lambda i: (0, 0))],
        out_shape=[jax.ShapeDtypeStruct((SEQ, ATTN_WIDTH), BF16),
                   jax.ShapeDtypeStruct((WINDOW, KV_WIDTH), F32),
                   jax.ShapeDtypeStruct((WINDOW, KV_WIDTH), F32)],
        compiler_params=_cparams(("arbitrary",)),
        name="attn_prompt",
    )(sinks, proj, proj, proj, proj, proj, b_attn.reshape(1, ATTN_QKV_WIDTH),
      cos_t, sin_t, cos_t, sin_t)


ATTN_SB = 8


def _attn_sample_kernel(q_ref, k_ref, v_ref, bq_ref, bk_ref, bv_ref, cos_ref, sin_ref,
                        sink_ref, e_ref, et_ref, ck_ref, cv_ref, o_ref, ko_ref, vo_ref):
    cos, sin = cos_ref[...], sin_ref[...]
    scale = ATTN_HEAD_DIM ** -0.5
    qr = []
    for c in range(ATTN_WIDTH // LANES):
        sl = slice(c * LANES, (c + 1) * LANES)
        qr.append(_rope(q_ref[:, sl] + bq_ref[:, sl], cos, sin) * scale)
    q = jnp.concatenate(qr, axis=1)
    kn = jnp.concatenate(
        [_rope(k_ref[:, c * LANES:(c + 1) * LANES] + bk_ref[:, c * LANES:(c + 1) * LANES], cos, sin)
         for c in range(KV_WIDTH // LANES)], axis=1)
    vn = v_ref[...] + bv_ref[...]
    lb = WINDOW
    for b in range(ATTN_SB):
        ko_ref[b, 0:lb - 1, :] = ck_ref[b, 1:lb, :]
        ko_ref[b, lb - 1:lb, :] = kn[b:b + 1, :]
        vo_ref[b, 0:lb - 1, :] = cv_ref[b, 1:lb, :]
        vo_ref[b, lb - 1:lb, :] = vn[b:b + 1, :]
        kb = ko_ref[b]
        vb = vo_ref[b]
        prod = jnp.concatenate(
            [kb * q[b:b + 1, g * KV_WIDTH:(g + 1) * KV_WIDTH] for g in range(ATTN_GROUP)], axis=0)
        s = _dot(prod.astype(BF16), e_ref[...])
        ps = []
        for g in range(ATTN_GROUP):
            sg = s[g * lb:(g + 1) * lb, :]
            sink = sink_ref[g:g + 1, :]
            m = jnp.maximum(jnp.max(sg, axis=0, keepdims=True), sink)
            e = jnp.exp(sg - m)
            den = jnp.sum(e, axis=0, keepdims=True) + jnp.exp(sink - m)
            ps.append(e / den)
        pe = _dot(jnp.concatenate(ps, axis=0).astype(BF16), et_ref[...])
        for g in range(ATTN_GROUP):
            og = jnp.sum(pe[g * lb:(g + 1) * lb, :] * vb, axis=0, keepdims=True)
            o_ref[b:b + 1, g * KV_WIDTH:(g + 1) * KV_WIDTH] = og


def _attn_sample(q_perm, k_new, v_new, bq_perm, bk, bv, cos_row, sin_row, sinks_gk, cache_k, cache_v):
    n = DEC_BATCH
    sb = ATTN_SB
    head_of_lane = jnp.arange(KV_WIDTH) // ATTN_HEAD_DIM
    e_mat = (head_of_lane[:, None] == jnp.arange(LANES)[None, :]).astype(BF16)
    row = lambda w: pl.BlockSpec((sb, w), lambda i: (i, 0))
    vec = lambda w: pl.BlockSpec((1, w), lambda i: (0, 0))
    full = lambda a: pl.BlockSpec(a.shape, lambda i: (0,) * a.ndim)
    cache = pl.BlockSpec((sb, WINDOW, KV_WIDTH), lambda i: (i, 0, 0))
    return pl.pallas_call(
        _attn_sample_kernel,
        grid=(n // sb,),
        in_specs=[row(ATTN_WIDTH), row(KV_WIDTH), row(KV_WIDTH),
                  vec(ATTN_WIDTH), vec(KV_WIDTH), vec(KV_WIDTH), vec(LANES), vec(LANES),
                  full(sinks_gk), full(e_mat), full(e_mat.T), cache, cache],
        out_specs=[row(ATTN_WIDTH), cache, cache],
        out_shape=[jax.ShapeDtypeStruct((n, ATTN_WIDTH), F32),
                   jax.ShapeDtypeStruct(cache_k.shape, F32),
                   jax.ShapeDtypeStruct(cache_v.shape, F32)],
        compiler_params=_cparams(("parallel",)),
        name="attn_sample",
    )(q_perm, k_new, v_new, bq_perm, bk, bv, cos_row, sin_row, sinks_gk, e_mat, e_mat.T,
      cache_k, cache_v)


DN_ROWS = 512
DN_PAIR = 2 * DN_CHUNK
CONV_PAD = 8
DN_HPS = 2


def _dn_gates_kernel(ab_ref, alog_ref, dtb_ref, cum_ref, beta_ref):
    ab = ab_ref[...]
    g = -jnp.exp(alog_ref[...]) * _softplus(ab + dtb_ref[...])
    row = lax.broadcasted_iota(jnp.int32, ab.shape, 0) % DN_CHUNK
    shift = 1
    while shift < DN_CHUNK:
        g = g + jnp.where(row >= shift, pltpu.roll(g, shift, 0), 0.0)
        shift *= 2
    cum_ref[...] = g
    beta_ref[...] = pltpu.roll(_sigmoid(ab), LANES - DN_HEADS, 1)


def _dn_gates(proj_ab, alog_pad, dtb_pad):
    blk = pl.BlockSpec((DN_ROWS, LANES), lambda i: (i, 0))
    vec = pl.BlockSpec((1, LANES), lambda i: (0, 0))
    out = jax.ShapeDtypeStruct((SEQ, LANES), F32)
    return pl.pallas_call(
        _dn_gates_kernel,
        grid=(SEQ // DN_ROWS,),
        in_specs=[blk, vec, vec],
        out_specs=[blk, blk],
        out_shape=[out, out],
        compiler_params=_cparams(("parallel",)),
        name="dn_gates",
    )(proj_ab, alog_pad, dtb_pad)


def _dn_prompt_kernel(q_ref, qp_ref, k_ref, kp_ref, v_ref, vp_ref, z_ref, cum_ref, beta_ref,
                      wq_ref, wk_ref, wv_ref, nw_ref, o_ref, so_ref, xq_s, xk_s, xv_s, s_s):
    hp = pl.program_id(0)
    i = pl.program_id(1)
    pr = DN_PAIR
    cl = DN_CHUNK

    @pl.when(i == 0)
    def _():
        s_s[...] = jnp.zeros_like(s_s)

    for x_ref, prev_ref, pad_s in ((q_ref, qp_ref, xq_s), (k_ref, kp_ref, xk_s), (v_ref, vp_ref, xv_s)):
        pad_s[0:CONV_PAD, :] = jnp.where(i > 0, prev_ref[...], 0.0)
        pad_s[CONV_PAD:, :] = x_ref[...]

    def lanes_of(hh):
        return slice(hh * DN_DIM, (hh + 1) * DN_DIM)

    def conv(pad_s, w_ref, hh, r0):
        y = None
        for j in range(CONV_WIDTH):
            start = CONV_PAD + r0 - (CONV_WIDTH - 1 - j)
            term = w_ref[j:j + 1, lanes_of(hh)] * pad_s[start:start + pr, lanes_of(hh)]
            y = term if y is None else y + term
        return _silu(y)

    def l2n(x):
        return x * lax.rsqrt(jnp.sum(x * x, axis=-1, keepdims=True) + RMS_EPS)

    def gate(ref, hh, r0):
        pick = ci == hp * DN_HPS + hh
        col = jnp.sum(jnp.where(pick, ref[r0:r0 + pr, :], 0.0), axis=1, keepdims=True)
        return jnp.broadcast_to(col, (pr, pr))

    ri = lax.broadcasted_iota(jnp.int32, (pr, pr), 0)
    ci = lax.broadcasted_iota(jnp.int32, (pr, pr), 1)
    same_chunk = (ri // cl) == (ci // cl)
    causal = same_chunk & (ri >= ci)
    strict = same_chunk & (ri > ci)
    eye = (ri == ci).astype(F32)
    first_rows = ri < cl
    first_rows2 = lax.broadcasted_iota(jnp.int32, (pr, 2 * DN_DIM), 0) < cl
    nw = nw_ref[...]
    bf = lambda a: a.astype(BF16)
    cat0 = lambda *a: jnp.concatenate(a, axis=0)
    cat1 = lambda *a: jnp.concatenate(a, axis=1)

    n_pairs = DN_ROWS // pr
    probs = [(hh, a * pr) for a in range(n_pairs) for hh in range(DN_HPS)]

    q_p = [l2n(conv(xq_s, wq_ref, hh, r0)) * (DN_DIM ** -0.5) for hh, r0 in probs]
    k_p = [l2n(conv(xk_s, wk_ref, hh, r0)) for hh, r0 in probs]
    v_p = [conv(xv_s, wv_ref, hh, r0) for hh, r0 in probs]
    cum = [gate(cum_ref, hh, r0) for hh, r0 in probs]
    beta = [gate(beta_ref, hh, r0) for hh, r0 in probs]
    gamma = [jnp.exp(c) for c in cum]
    decay = [jnp.where(causal, jnp.exp(c - c.T), 0.0) for c in cum]
    kq = [_dot_nt(bf(cat0(k, q)), bf(k)) for k, q in zip(k_p, q_p)]
    x = [jnp.where(strict, -(b * m[:pr] * d), 0.0) for b, m, d in zip(beta, kq, decay)]
    att = [bf(m[pr:] * d) for m, d in zip(kq, decay)]
    t = [eye + m for m in x]
    xp = [_dot(bf(m), bf(m)) for m in x]
    for _ in range(int(math.log2(cl)) - 2):
        y = [_dot(bf(p), cat1(bf(p), bf(m))) for p, m in zip(xp, t)]
        xp = [m[:, :pr] for m in y]
        t = [m + n[:, pr:] for m, n in zip(t, y)]
    t = [m + _dot(bf(p), bf(m)) for p, m in zip(xp, t)]
    wu = [_dot(bf(m), bf(cat1(k * (b * g), v * b)))
          for m, k, v, b, g in zip(t, k_p, v_p, beta, gamma)]
    awu = [_dot(a, bf(m)) for a, m in zip(att, wu)]
    q_prime = [bf(q * g - m[:, :DN_DIM]) for q, g, m in zip(q_p, gamma, awu)]
    au = [m[:, DN_DIM:] for m in awu]
    k_dec_t = [bf((k * jnp.exp(jnp.where(first_rows, c[cl - 1:cl, :], c[pr - 1:pr, :]) - c)).T)
               for k, c in zip(k_p, cum)]
    kwu1 = [_dot(kt, bf(jnp.where(first_rows2, m, 0.0))) for kt, m in zip(k_dec_t, wu)]
    kwu2 = [_dot(kt, bf(jnp.where(first_rows2, 0.0, m))) for kt, m in zip(k_dec_t, wu)]
    g1 = [jnp.exp(c[cl - 1:cl, :]) for c in cum]
    g2 = [jnp.exp(c[pr - 1:pr, :]) for c in cum]
    comp = [_dot(bf(-m2[:, :DN_DIM]), bf(cat1(-m1[:, :DN_DIM], m1[:, DN_DIM:]))) for m1, m2 in zip(kwu1, kwu2)]
    mp12 = [-(b * m1[:, :DN_DIM]) - a * m2[:, :DN_DIM] + c[:, :DN_DIM]
            for a, b, m1, m2, c in zip(g1, g2, kwu1, kwu2, comp)]
    n12 = [b * m1[:, DN_DIM:] + c[:, DN_DIM:] + m2[:, DN_DIM:] for b, m1, m2, c in zip(g2, kwu1, kwu2, comp)]
    lhs = [cat0(qp[:cl], bf(-m1[:, :DN_DIM]), bf(m)) for qp, m1, m in zip(q_prime, kwu1, mp12)]

    s = [s_s[hh] for hh in range(DN_HPS)]
    for a in range(n_pairs):
        js = [a * DN_HPS + hh for hh in range(DN_HPS)]
        r = [_dot(lhs[j], bf(s[hh])) for hh, j in enumerate(js)]
        s_mid = [g1[j] * s[hh] + r[hh][cl:cl + DN_DIM] + kwu1[j][:, DN_DIM:] for hh, j in enumerate(js)]
        s_new = [(g1[j] * g2[j]) * s[hh] + r[hh][cl + DN_DIM:] + n12[j] for hh, j in enumerate(js)]
        o2 = [_dot(q_prime[j][cl:], bf(s_mid[hh])) for hh, j in enumerate(js)]
        for hh, j in enumerate(js):
            o = cat0(r[hh][:cl], o2[hh]) + au[j]
            o = o * lax.rsqrt(jnp.mean(o * o, axis=-1, keepdims=True) + RMS_EPS) * nw
            r0 = a * pr
            o_ref[r0:r0 + pr, lanes_of(hh)] = (o * _silu(z_ref[r0:r0 + pr, lanes_of(hh)])).astype(o_ref.dtype)
        s = s_new

    for hh in range(DN_HPS):
        s_s[hh] = s[hh]

    @pl.when(i == pl.num_programs(1) - 1)
    def _():
        for hh in range(DN_HPS):
            so_ref[hh] = s[hh]


def _dn_prompt(proj, cum, beta, w_conv, norm_w):
    rows = DN_ROWS
    width = DN_HPS * DN_DIM
    q0 = ATTN_QKV_WIDTH // width
    k0 = q0 + DN_WIDTH // width
    v0 = k0 + DN_WIDTH // width
    z0 = v0 + DN_WIDTH // width
    per = rows // CONV_PAD

    def blk(c0):
        return pl.BlockSpec((rows, width), lambda h, i: (i, c0 + h))

    def prev(c0):
        return pl.BlockSpec((CONV_PAD, width), lambda h, i: (jnp.maximum(i * per - 1, 0), c0 + h))

    def wblk(c0):
        return pl.BlockSpec((CONV_WIDTH, width), lambda h, i: (0, c0 + h))

    gate = pl.BlockSpec((rows, LANES), lambda h, i: (i, 0))
    vec = pl.BlockSpec((1, DN_DIM), lambda h, i: (0, 0))
    pad = pltpu.VMEM((rows + CONV_PAD, width), F32)
    return pl.pallas_call(
        _dn_prompt_kernel,
        grid=(DN_HEADS // DN_HPS, SEQ // rows),
        in_specs=[blk(q0), prev(q0), blk(k0), prev(k0), blk(v0), prev(v0), blk(z0), gate, gate,
                  wblk(0), wblk(DN_WIDTH // width), wblk(2 * DN_WIDTH // width), vec],
        out_specs=[pl.BlockSpec((rows, width), lambda h, i: (i, h)),
                   pl.BlockSpec((DN_HPS, DN_DIM, DN_DIM), lambda h, i: (h, 0, 0))],
        out_shape=[jax.ShapeDtypeStruct((SEQ, DN_WIDTH), BF16),
                   jax.ShapeDtypeStruct((DN_HEADS, DN_DIM, DN_DIM), F32)],
        scratch_shapes=[pad, pad, pad, pltpu.VMEM((DN_HPS, DN_DIM, DN_DIM), F32)],
        compiler_params=_cparams(("parallel", "arbitrary")),
        name="dn_prompt",
    )(proj, proj, proj, proj, proj, proj, proj, cum, beta, w_conv, w_conv, w_conv,
      norm_w.reshape(1, DN_DIM))


DN_SB = 8


def _dn_gates_of_head(ab, alog, dtb, h):
    lane = lax.broadcasted_iota(jnp.int32, ab.shape, 1)
    g_all = -jnp.exp(alog) * _softplus(ab + dtb)
    g = jnp.sum(jnp.where(lane == h, g_all, 0.0), axis=1, keepdims=True)
    beta = jnp.sum(jnp.where(lane == h + DN_HEADS, _sigmoid(ab), 0.0), axis=1, keepdims=True)
    return jnp.broadcast_to(g, ab.shape), jnp.broadcast_to(beta, ab.shape)


def _dn_sample_prep_kernel(x_ref, sc_ref, ab_ref, w_ref, alog_ref, dtb_ref,
                           qt_ref, kt_ref, vb_ref, bg_ref, gam_ref):
    ab = ab_ref[...]
    alog, dtb = alog_ref[...], dtb_ref[...]

    def conv(col):
        sl = slice(col, col + LANES)
        y = w_ref[CONV_WIDTH - 1:CONV_WIDTH, sl] * x_ref[:, sl]
        for j in range(CONV_WIDTH - 1):
            y = y + w_ref[j:j + 1, sl] * sc_ref[:, j * CONV_DIM + col:j * CONV_DIM + col + LANES]
        return _silu(y)

    def l2n(x):
        return x * lax.rsqrt(jnp.sum(x * x, axis=-1, keepdims=True) + RMS_EPS)

    for h in range(DN_HEADS):
        q = l2n(conv(h * DN_DIM)) * (DN_DIM ** -0.5)
        k = l2n(conv(DN_WIDTH + h * DN_DIM))
        v = conv(2 * DN_WIDTH + h * DN_DIM)
        g_b, beta_b = _dn_gates_of_head(ab, alog, dtb, h)
        gamma = jnp.exp(g_b)
        qt_ref[h] = q.T
        kt_ref[h] = k.T
        vb_ref[h] = v * beta_b
        bg_ref[h] = beta_b * gamma
        gam_ref[h] = gamma


def _dn_sample_prep(qkv_s, conv_state, ab_s, w_conv, alog_pad, dtb_pad):
    n = DEC_BATCH
    full = lambda a: pl.BlockSpec(a.shape, lambda i: (0,) * a.ndim)
    hm = jax.ShapeDtypeStruct((DN_HEADS, n, DN_DIM), F32)
    hm_spec = pl.BlockSpec((DN_HEADS, n, DN_DIM), lambda i: (0, 0, 0))
    args = (qkv_s, conv_state, ab_s, w_conv, alog_pad, dtb_pad)
    return pl.pallas_call(
        _dn_sample_prep_kernel,
        grid=(1,),
        in_specs=[full(a) for a in args],
        out_specs=[hm_spec] * 5,
        out_shape=[hm] * 5,
        compiler_params=_cparams(("arbitrary",)),
        name="dn_sample_prep",
    )(*args)


def _dn_sample_kernel(s_ref, qt_ref, kt_ref, vb_ref, bg_ref, gam_ref, z_ref, nw_ref,
                      so_ref, o_ref, o_s):
    step = pl.program_id(0)
    lane = lax.broadcasted_iota(jnp.int32, (DN_DIM, DEC_BATCH), 1)

    def head(h, carry):
        kt = kt_ref[h]
        qt = qt_ref[h]
        for bb in range(DN_SB):
            b = step * DN_SB + bb
            pick = lane == b
            kcol = jnp.sum(jnp.where(pick, kt, 0.0), axis=1, keepdims=True)
            qcol = jnp.sum(jnp.where(pick, qt, 0.0), axis=1, keepdims=True)
            s = s_ref[bb, h]
            ks = jnp.sum(kcol * s, axis=0, keepdims=True)
            u = vb_ref[h, pl.ds(b, 1), :] - bg_ref[h, pl.ds(b, 1), :] * ks
            s_new = gam_ref[h, pl.ds(b, 1), :] * s + kcol * u
            so_ref[bb, h] = s_new
            o_s[h, pl.ds(bb, 1), :] = jnp.sum(qcol * s_new, axis=0, keepdims=True)
        return carry

    lax.fori_loop(0, DN_HEADS, head, 0)
    nw = nw_ref[...]
    for h in range(DN_HEADS):
        o = o_s[h]
        o = o * lax.rsqrt(jnp.mean(o * o, axis=-1, keepdims=True) + RMS_EPS) * nw
        sl = slice(h * DN_DIM, (h + 1) * DN_DIM)
        o_ref[:, sl] = o * _silu(z_ref[:, sl])


def _dn_sample(state, qt, kt, vb, bg, gam, z_s, norm_w):
    n = DEC_BATCH
    sb = DN_SB
    sblk = pl.BlockSpec((sb, DN_HEADS, DN_DIM, DN_DIM), lambda i: (i, 0, 0, 0))
    hm = pl.BlockSpec((DN_HEADS, n, DN_DIM), lambda i: (0, 0, 0))
    row = pl.BlockSpec((sb, DN_WIDTH), lambda i: (i, 0))
    return pl.pallas_call(
        _dn_sample_kernel,
        grid=(n // sb,),
        in_specs=[sblk, hm, hm, hm, hm, hm, row, pl.BlockSpec((1, DN_DIM), lambda i: (0, 0))],
        out_specs=[sblk, row],
        out_shape=[jax.ShapeDtypeStruct(state.shape, F32), jax.ShapeDtypeStruct((n, DN_WIDTH), F32)],
        scratch_shapes=[pltpu.VMEM((DN_HEADS, sb, DN_DIM), F32)],
        compiler_params=_cparams(("parallel",)),
        name="dn_sample",
    )(state, qt, kt, vb, bg, gam, z_s, norm_w.reshape(1, DN_DIM))


BM = 1040
BN = 1024
BF_FFN = 256


def _rope_tables(pos):
    half = ATTN_HEAD_DIM // 2
    inv_freq = ROPE_THETA ** (-jnp.arange(half, dtype=F32) / half)
    ang = pos.astype(F32)[:, None] * inv_freq
    cos, sin = jnp.cos(ang), jnp.sin(ang)
    reps = LANES // ATTN_HEAD_DIM
    return jnp.tile(cos, (1, 2 * reps)), jnp.tile(jnp.concatenate([-sin, sin], axis=1), (1, reps))


def _group_major(a):
    lead = a.shape[:-1]
    a = a.reshape(lead + (ATTN_KV_HEADS, ATTN_GROUP, ATTN_HEAD_DIM))
    return jnp.swapaxes(a, -3, -2).reshape(lead + (ATTN_WIDTH,))


def _layer(x_prompt, x_sample, cache_k, cache_v, state_conv, state_delta, w_in, b_attn, attn_sinks,
           w_conv, dn_a_log, dn_dt_bias, dn_norm_w, w_out, ln1_g, ln1_b, w_gate, w_up, w_down,
           ln2_g, ln2_b):
    n_s = DEC_BATCH
    xb = _xcast(x_prompt, x_sample)
    w_in_t = w_in.T
    w_main = _cast_col_blocks([w_in_t], D_MODEL, 1, BN, MAIN_WIDTH).reshape(MAIN_WIDTH // BN, BN, D_MODEL)
    w_ab = jnp.pad(w_in_t[MAIN_WIDTH:], ((0, LANES - 2 * DN_HEADS), (0, 0))).astype(BF16)
    proj = _matmul_nt(xb, w_main, BM)
    proj_ab = _matmul_nt(xb, w_ab[None], BM)

    pad16 = lambda v: jnp.pad(v, (0, LANES - DN_HEADS)).reshape(1, LANES)
    alog_pad, dtb_pad = pad16(dn_a_log), pad16(dn_dt_bias)

    cos_t, sin_t = _rope_tables(jnp.arange(SEQ, dtype=jnp.int32))
    attn_p, pk, pv = _attn_prompt(proj, b_attn, attn_sinks, cos_t, sin_t)
    cum, beta = _dn_gates(proj_ab, alog_pad, dtb_pad)
    dn_p, ps = _dn_prompt(proj, cum, beta, w_conv, dn_norm_w)
    pc = proj[SEQ - (CONV_WIDTH - 1):SEQ, ATTN_QKV_WIDTH:ATTN_QKV_WIDTH + CONV_DIM]

    proj_s = proj[SEQ:]
    cos_r, sin_r = _rope_tables(jnp.full((1,), PAST_LEN, jnp.int32))
    sinks_gk = jnp.pad(attn_sinks.reshape(ATTN_KV_HEADS, ATTN_GROUP).T,
                       ((0, 0), (0, LANES - ATTN_KV_HEADS)))
    attn_s_perm, sk, sv = _attn_sample(
        _group_major(proj_s[:, :ATTN_WIDTH]),
        proj_s[:, ATTN_WIDTH:ATTN_WIDTH + KV_WIDTH],
        proj_s[:, ATTN_WIDTH + KV_WIDTH:ATTN_QKV_WIDTH],
        _group_major(b_attn[:ATTN_WIDTH]).reshape(1, ATTN_WIDTH),
        b_attn[ATTN_WIDTH:ATTN_WIDTH + KV_WIDTH].reshape(1, KV_WIDTH),
        b_attn[ATTN_WIDTH + KV_WIDTH:].reshape(1, KV_WIDTH),
        cos_r, sin_r, sinks_gk, cache_k.reshape(n_s, WINDOW, KV_WIDTH),
        cache_v.reshape(n_s, WINDOW, KV_WIDTH))
    attn_s = jnp.swapaxes(attn_s_perm.reshape(n_s, ATTN_GROUP, ATTN_KV_HEADS, ATTN_HEAD_DIM), 1, 2)
    attn_s = attn_s.reshape(n_s, ATTN_WIDTH)

    qkv_s = proj_s[:, ATTN_QKV_WIDTH:ATTN_QKV_WIDTH + CONV_DIM]
    z_s = proj_s[:, ATTN_QKV_WIDTH + CONV_DIM:]
    qt, kt, vb, bg, gam = _dn_sample_prep(
        qkv_s, state_conv.reshape(n_s, (CONV_WIDTH - 1) * CONV_DIM), proj_ab[SEQ:],
        w_conv, alog_pad, dtb_pad)
    ss, dn_s = _dn_sample(state_delta, qt, kt, vb, bg, gam, z_s, dn_norm_w)
    sc = jnp.concatenate([state_conv[:, 1:], qkv_s[:, None, :]], axis=1)

    attn_all = jnp.concatenate([attn_p, attn_s.astype(BF16)], axis=0)
    dn_all = jnp.concatenate([dn_p, dn_s.astype(BF16)], axis=0)
    mixed = _out_proj(attn_all, dn_all, _cast_col_blocks([w_out], BN, D_MODEL // BN, 1024), BM)
    h32, hb = _ln1(x_prompt, x_sample, mixed, ln1_g, ln1_b)
    w_gu = _cast_col_blocks([w_gate, w_up], BF_FFN, FFN_HIDDEN // BF_FFN, D_MODEL)
    w_d = _cast_col_blocks([w_down], D_MODEL, 1, BF_FFN).reshape(FFN_HIDDEN // BF_FFN, BF_FFN, D_MODEL)
    ffn = _ffn(hb, w_gu, w_d, BM)
    y_p, y_s = _ln2(h32, ffn, ln2_g, ln2_b)
    return (y_p, y_s, pk.reshape(WINDOW, ATTN_KV_HEADS, ATTN_HEAD_DIM),
            pv.reshape(WINDOW, ATTN_KV_HEADS, ATTN_HEAD_DIM), pc, ps,
            sk.reshape(n_s, WINDOW, ATTN_KV_HEADS, ATTN_HEAD_DIM),
            sv.reshape(n_s, WINDOW, ATTN_KV_HEADS, ATTN_HEAD_DIM), sc, ss)


def kernel(x_prompt, x_sample, cache_swa_k, cache_swa_v, state_conv, state_delta, w_in, b_attn,
           attn_sinks, w_conv, dn_a_log, dn_dt_bias, dn_norm_w, w_out, ln1_g, ln1_b, w_gate, w_up,
           w_down, ln2_g, ln2_b):
    assert x_prompt.shape == (1, SEQ, D_MODEL) and x_sample.shape == (DEC_BATCH, 1, D_MODEL)
    assert w_in.shape[0] == 1, "one layer"
    y_p, y_s, pk, pv, pc, ps, sk, sv, sc, ss = _layer(
        x_prompt[0], x_sample[:, 0], cache_swa_k[0], cache_swa_v[0], state_conv[0], state_delta[0],
        w_in[0], b_attn[0], attn_sinks[0], w_conv[0], dn_a_log[0], dn_dt_bias[0], dn_norm_w[0],
        w_out[0], ln1_g[0], ln1_b[0], w_gate[0], w_up[0], w_down[0], ln2_g[0], ln2_b[0])
    return (y_p[None], y_s[:, None], pk[None, None], pv[None, None], pc[None, None],
            ps[None, None], sk[None], sv[None], sc[None], ss[None])
```

```python
import math

import jax
import jax.numpy as jnp
from jax import lax
from jax.experimental import pallas as pl
from jax.experimental.pallas import tpu as pltpu

D_MODEL = 4096
SEQ = 8192
DEC_BATCH = 128
PAST_LEN = 8192
ROWS = SEQ + DEC_BATCH

ATTN_HEADS = 32
ATTN_KV_HEADS = 8
ATTN_HEAD_DIM = 64
ATTN_GROUP = ATTN_HEADS // ATTN_KV_HEADS
ATTN_WIDTH = ATTN_HEADS * ATTN_HEAD_DIM
KV_WIDTH = ATTN_KV_HEADS * ATTN_HEAD_DIM
WINDOW = 128
ROPE_THETA = 10000.0
DN_HEADS = 16
DN_DIM = 128
DN_WIDTH = DN_HEADS * DN_DIM
CONV_WIDTH = 4
CONV_DIM = 3 * DN_WIDTH
DN_CHUNK = 64
ATTN_QKV_WIDTH = ATTN_WIDTH + 2 * KV_WIDTH
MAIN_WIDTH = ATTN_QKV_WIDTH + CONV_DIM + DN_WIDTH
FFN_HIDDEN = 11008
DEEPNORM_ALPHA = 2.0 ** 0.25
LN_EPS = 1e-5
RMS_EPS = 1e-6

LANES = 128
VMEM_LIMIT = 56 * 1024 * 1024

F32 = jnp.float32
BF16 = jnp.bfloat16


def _cparams(sem, vmem=VMEM_LIMIT):
    return pltpu.CompilerParams(dimension_semantics=sem, vmem_limit_bytes=vmem)


def _dot(a, b):
    return jnp.dot(a, b, preferred_element_type=F32)


def _dot_nt(a, b):
    return lax.dot_general(a, b, (((1,), (1,)), ((), ())), preferred_element_type=F32)


def _sigmoid(x):
    return 1.0 / (1.0 + jnp.exp(-x))


def _silu(x):
    return x * _sigmoid(x)


def _softplus(x):
    return jnp.maximum(x, 0.0) + jnp.log(1.0 + jnp.exp(-jnp.abs(x)))


def _rope(x, cos, sin_signed):
    lane = lax.broadcasted_iota(jnp.int32, x.shape, 1)
    first_half = (lane % ATTN_HEAD_DIM) < (ATTN_HEAD_DIM // 2)
    partner = jnp.where(first_half, pltpu.roll(x, LANES - ATTN_HEAD_DIM // 2, 1),
                        pltpu.roll(x, ATTN_HEAD_DIM // 2, 1))
    return x * cos + partner * sin_signed


def _mm_nt_kernel(x_ref, wt_ref, o_ref):
    o_ref[...] = _dot_nt(x_ref[...], wt_ref[0]).astype(o_ref.dtype)


def _matmul_nt(x, wt_blocks, bm, out_dtype=F32):
    m, k = x.shape
    nb, bn, _ = wt_blocks.shape
    return pl.pallas_call(
        _mm_nt_kernel,
        grid=(m // bm, nb),
        in_specs=[pl.BlockSpec((bm, k), lambda i, j: (i, 0)),
                  pl.BlockSpec((1, bn, k), lambda i, j: (j, 0, 0))],
        out_specs=pl.BlockSpec((bm, bn), lambda i, j: (i, j)),
        out_shape=jax.ShapeDtypeStruct((m, nb * bn), out_dtype),
        compiler_params=_cparams(("parallel", "parallel")),
        name="matmul",
    )(x, wt_blocks)


def _out_proj_kernel(a_ref, d_ref, w_ref, o_ref):
    ka = a_ref.shape[1]
    o_ref[...] = _dot(a_ref[...], w_ref[0, :ka, :]) + _dot(d_ref[...], w_ref[0, ka:, :])


def _out_proj(a, d, w_blocks, bm):
    m, ka = a.shape
    kd = d.shape[1]
    nb, _, bn = w_blocks.shape
    return pl.pallas_call(
        _out_proj_kernel,
        grid=(m // bm, nb),
        in_specs=[pl.BlockSpec((bm, ka), lambda i, j: (i, 0)),
                  pl.BlockSpec((bm, kd), lambda i, j: (i, 0)),
                  pl.BlockSpec((1, ka + kd, bn), lambda i, j: (j, 0, 0))],
        out_specs=pl.BlockSpec((bm, bn), lambda i, j: (i, j)),
        out_shape=jax.ShapeDtypeStruct((m, nb * bn), F32),
        compiler_params=_cparams(("parallel", "parallel")),
        name="out_proj",
    )(a, d, w_blocks)


FFN_NCHUNK = 1024
FFN_ROW_ALIGN = 16


def _ffn_kernel(h_ref, wg_ref, wu_ref, wd_ref, o_ref):
    f = pl.program_id(1)

    @pl.when(f == 0)
    def _():
        o_ref[...] = jnp.zeros_like(o_ref)

    bm = h_ref.shape[0]
    cut = (bm // 2) // FFN_ROW_ALIGN * FFN_ROW_ALIGN
    groups = ((0, cut), (cut, bm))
    wg = wg_ref[...].astype(BF16)
    wu = wu_ref[...].astype(BF16)
    gates = [_dot(h_ref[a:b, :], wg) for a, b in groups]
    ups = [_dot(h_ref[a:b, :], wu) for a, b in groups]
    for (a, b), g, u in zip(groups, gates, ups):
        act = (_silu(g) * u).astype(BF16)
        for n in range(0, o_ref.shape[1], FFN_NCHUNK):
            o_ref[a:b, n:n + FFN_NCHUNK] += _dot(act, wd_ref[0, :, n:n + FFN_NCHUNK])


def _ffn(h, w_gate, w_up, wd_blocks, bm):
    m, d = h.shape
    nf, bf, _ = wd_blocks.shape
    once = pl.Buffered(1)
    return pl.pallas_call(
        _ffn_kernel,
        grid=(m // bm, nf),
        in_specs=[pl.BlockSpec((bm, d), lambda i, f: (i, 0), pipeline_mode=once),
                  pl.BlockSpec((d, bf), lambda i, f: (0, f)),
                  pl.BlockSpec((d, bf), lambda i, f: (0, f)),
                  pl.BlockSpec((1, bf, d), lambda i, f: (f, 0, 0))],
        out_specs=pl.BlockSpec((bm, d), lambda i, f: (i, 0), pipeline_mode=once),
        out_shape=jax.ShapeDtypeStruct((m, d), F32),
        compiler_params=_cparams(("parallel", "arbitrary")),
        name="ffn",
    )(h, w_gate, w_up, wd_blocks)


BR = WINDOW
NB_PROMPT = SEQ // BR


def _deepnorm(x, mixed, g, b):
    v = DEEPNORM_ALPHA * x + mixed
    mu = jnp.mean(v, axis=-1, keepdims=True)
    c = v - mu
    var = jnp.mean(c * c, axis=-1, keepdims=True)
    return c * lax.rsqrt(var + LN_EPS) * g + b


def _prompt_rows(width):
    return pl.BlockSpec((BR, width), lambda i: (jnp.minimum(i, NB_PROMPT - 1), 0))


def _decode_rows(width):
    return pl.BlockSpec((BR, width), lambda i: (0, 0))


def _xcast_kernel(xp_ref, xs_ref, o_ref):
    i = pl.program_id(0)
    o_ref[...] = jnp.where(i < NB_PROMPT, xp_ref[...], xs_ref[...]).astype(o_ref.dtype)


def _xcast(x_prompt, x_sample):
    d = x_prompt.shape[1]
    return pl.pallas_call(
        _xcast_kernel,
        grid=(NB_PROMPT + 1,),
        in_specs=[_prompt_rows(d), _decode_rows(d)],
        out_specs=pl.BlockSpec((BR, d), lambda i: (i, 0)),
        out_shape=jax.ShapeDtypeStruct((ROWS, d), BF16),
        compiler_params=_cparams(("arbitrary",)),
        name="xcast",
    )(x_prompt, x_sample)


def _ln1_kernel(xp_ref, xs_ref, m_ref, g_ref, b_ref, o_ref, ob_ref):
    i = pl.program_id(0)
    x = jnp.where(i < NB_PROMPT, xp_ref[...], xs_ref[...])
    y = _deepnorm(x, m_ref[...], g_ref[...], b_ref[...])
    o_ref[...] = y
    ob_ref[...] = y.astype(BF16)


def _ln1(x_prompt, x_sample, mixed, g, b):
    d = mixed.shape[1]
    row = pl.BlockSpec((BR, d), lambda i: (i, 0))
    vec = pl.BlockSpec((1, d), lambda i: (0, 0))
    return pl.pallas_call(
        _ln1_kernel,
        grid=(NB_PROMPT + 1,),
        in_specs=[_prompt_rows(d), _decode_rows(d), row, vec, vec],
        out_specs=[row, row],
        out_shape=[jax.ShapeDtypeStruct((ROWS, d), F32), jax.ShapeDtypeStruct((ROWS, d), BF16)],
        compiler_params=_cparams(("arbitrary",)),
        name="ln1",
    )(x_prompt, x_sample, mixed, g.reshape(1, d), b.reshape(1, d))


def _ln2_kernel(h_ref, f_ref, g_ref, b_ref, yp_ref, ys_ref):
    i = pl.program_id(0)
    y = _deepnorm(h_ref[...], f_ref[...], g_ref[...], b_ref[...])

    @pl.when(i < NB_PROMPT)
    def _():
        yp_ref[...] = y

    @pl.when(i >= NB_PROMPT)
    def _():
        ys_ref[...] = y


def _ln2(h32, ffn, g, b):
    d = h32.shape[1]
    row = pl.BlockSpec((BR, d), lambda i: (i, 0))
    vec = pl.BlockSpec((1, d), lambda i: (0, 0))
    return pl.pallas_call(
        _ln2_kernel,
        grid=(NB_PROMPT + 1,),
        in_specs=[row, row, vec, vec],
        out_specs=[_prompt_rows(d), _decode_rows(d)],
        out_shape=[jax.ShapeDtypeStruct((SEQ, d), F32), jax.ShapeDtypeStruct((DEC_BATCH, d), F32)],
        compiler_params=_cparams(("arbitrary",)),
        name="ln2",
    )(h32, ffn, g.reshape(1, d), b.reshape(1, d))


def _cast_kernel(*refs):
    o_ref = refs[-1]
    off = 0
    for r in refs[:-1]:
        width = r.shape[1]
        o_ref[0, :, off:off + width] = r[...].astype(o_ref.dtype)
        off += width


def _cast_col_blocks(ws, bn, n_blocks, bk, rows=None):
    k = ws[0].shape[0] if rows is None else rows
    return pl.pallas_call(
        _cast_kernel,
        grid=(n_blocks, k // bk),
        in_specs=[pl.BlockSpec((bk, bn), lambda j, r: (r, j)) for _ in ws],
        out_specs=pl.BlockSpec((1, bk, bn * len(ws)), lambda j, r: (j, r, 0)),
        out_shape=jax.ShapeDtypeStruct((n_blocks, k, bn * len(ws)), BF16),
        compiler_params=_cparams(("parallel", "parallel")),
        name="cast_w",
    )(*ws)


def _attn_prompt_kernel(sink_ref, q_ref, kc_ref, kp_ref, vc_ref, vp_ref, b_ref,
                        cc_ref, sc_ref, cp_ref, sp_ref, o_ref, ko_ref, vo_ref):
    i = pl.program_id(0)
    w = WINDOW
    cos_c, sin_c = cc_ref[...], sc_ref[...]
    cos_p, sin_p = cp_ref[...], sp_ref[...]
    n_kchunk = KV_WIDTH // LANES

    k_cur, k_prev, v_cur, v_prev = [], [], [], []
    for c in range(n_kchunk):
        sl = slice(c * LANES, (c + 1) * LANES)
        bk = b_ref[:, ATTN_WIDTH + c * LANES:ATTN_WIDTH + (c + 1) * LANES]
        bv = b_ref[:, ATTN_WIDTH + KV_WIDTH + c * LANES:ATTN_WIDTH + KV_WIDTH + (c + 1) * LANES]
        kc = _rope(kc_ref[:, sl] + bk, cos_c, sin_c)
        kp = _rope(kp_ref[:, sl] + bk, cos_p, sin_p)
        vc = vc_ref[:, sl] + bv
        vp = vp_ref[:, sl] + bv
        ko_ref[:, sl] = kc
        vo_ref[:, sl] = vc
        k_cur.append(kc)
        k_prev.append(kp)
        v_cur.append(vc)
        v_prev.append(vp)

    rows = ATTN_GROUP * w
    r = lax.broadcasted_iota(jnp.int32, (rows, 2 * w), 0) % w
    col = lax.broadcasted_iota(jnp.int32, (rows, 2 * w), 1)
    valid = (col > r) & (col <= r + w) & ((col >= w) | (i > 0))
    row_id = lax.broadcasted_iota(jnp.int32, (rows, 1), 0)
    lane_k = lax.broadcasted_iota(jnp.int32, (2 * w, LANES), 1)
    lane_o = lax.broadcasted_iota(jnp.int32, (w, LANES), 1)
    scale = ATTN_HEAD_DIM ** -0.5

    k2s, v2s, q4s = [], [], []
    for hk in range(ATTN_KV_HEADS):
        kchunk, khalf = hk // 2, hk % 2
        k2 = jnp.concatenate([k_prev[kchunk], k_cur[kchunk]], axis=0)
        in_half = (lane_k // ATTN_HEAD_DIM) == khalf
        k2s.append(jnp.where(in_half, k2, 0.0).astype(BF16))
        v2s.append(jnp.concatenate([v_prev[kchunk], v_cur[kchunk]], axis=0).astype(BF16))
        qs = []
        for g in range(ATTN_GROUP):
            hq = hk * ATTN_GROUP + g
            qchunk, qhalf = hq // 2, hq % 2
            sl = slice(qchunk * LANES, (qchunk + 1) * LANES)
            qc = _rope(q_ref[:, sl] + b_ref[:, sl], cos_c, sin_c) * scale
            if qhalf != khalf:
                qc = pltpu.roll(qc, ATTN_HEAD_DIM, 1)
            qs.append(qc)
        q4s.append(jnp.concatenate(qs, axis=0).astype(BF16))
    scores = [_dot_nt(q4, k2) for q4, k2 in zip(q4s, k2s)]
    probs = []
    for hk, s in enumerate(scores):
        s = jnp.where(valid, s, -jnp.inf)
        sink = jnp.zeros((rows, 1), F32)
        for g in range(ATTN_GROUP):
            sink = jnp.where(row_id // w == g, sink_ref[hk * ATTN_GROUP + g], sink)
        m = jnp.maximum(jnp.max(s, axis=-1, keepdims=True), sink)
        e = jnp.exp(s - m)
        den = jnp.sum(e, axis=-1, keepdims=True) + jnp.exp(sink - m)
        probs.append((e / den).astype(BF16))
    pvs = [_dot(p, v2) for p, v2 in zip(probs, v2s)]
    for hk, pv in enumerate(pvs):
        khalf = hk % 2
        outs = []
        for g in range(ATTN_GROUP):
            og = pv[g * w:(g + 1) * w, :]
            if (g % 2) != khalf:
                og = pltpu.roll(og, ATTN_HEAD_DIM, 1)
            outs.append(og)
        for j in range(ATTN_GROUP // 2):
            oc = jnp.where(lane_o < ATTN_HEAD_DIM, outs[2 * j], outs[2 * j + 1])
            c = hk * (ATTN_GROUP // 2) + j
            o_ref[:, c * LANES:(c + 1) * LANES] = oc.astype(o_ref.dtype)


def _attn_prompt(proj, b_attn, sinks, cos_t, sin_t):
    nb = SEQ // WINDOW
    kcol = ATTN_WIDTH // KV_WIDTH
    prev = lambda i: jnp.maximum(i - 1, 0)
    return pl.pallas_call(
        _attn_prompt_kernel,
        grid=(nb,),
        in_specs=[pl.BlockSpec(memory_space=pltpu.SMEM),
                  pl.BlockSpec((WINDOW, ATTN_WIDTH), lambda i: (i, 0)),
                  pl.BlockSpec((WINDOW, KV_WIDTH), lambda i: (i, kcol)),
                  pl.BlockSpec((WINDOW, KV_WIDTH), lambda i: (prev(i), kcol)),
                  pl.BlockSpec((WINDOW, KV_WIDTH), lambda i: (i, kcol + 1)),
                  pl.BlockSpec((WINDOW, KV_WIDTH), lambda i: (prev(i), kcol + 1)),
                  pl.BlockSpec((1, ATTN_QKV_WIDTH), lambda i: (0, 0)),
                  pl.BlockSpec((WINDOW, LANES), lambda i: (i, 0)),
                  pl.BlockSpec((WINDOW, LANES), lambda i: (i, 0)),
                  pl.BlockSpec((WINDOW, LANES), lambda i: (prev(i), 0)),
                  pl.BlockSpec((WINDOW, LANES), lambda i: (prev(i), 0))],
        out_specs=[pl.BlockSpec((WINDOW, ATTN_WIDTH), lambda i: (i, 0)),
                   pl.BlockSpec((WINDOW, KV_WIDTH), lambda i: (0, 0)),
                   pl.BlockSpec((WINDOW, KV_WIDTH), lambda i: (0, 0))],
        out_shape=[jax.ShapeDtypeStruct((SEQ, ATTN_WIDTH), BF16),
                   jax.ShapeDtypeStruct((WINDOW, KV_WIDTH), F32),
                   jax.ShapeDtypeStruct((WINDOW, KV_WIDTH), F32)],
        compiler_params=_cparams(("arbitrary",)),
        name="attn_prompt",
    )(sinks, proj, proj, proj, proj, proj, b_attn.reshape(1, ATTN_QKV_WIDTH),
      cos_t, sin_t, cos_t, sin_t)


ATTN_SB = 8


def _attn_sample_kernel(q_ref, k_ref, v_ref, bq_ref, bk_ref, bv_ref, cos_ref, sin_ref,
                        sink_ref, e_ref, et_ref, ck_ref, cv_ref, o_ref, ko_ref, vo_ref):
    cos, sin = cos_ref[...], sin_ref[...]
    scale = ATTN_HEAD_DIM ** -0.5
    qr = []
    for c in range(ATTN_WIDTH // LANES):
        sl = slice(c * LANES, (c + 1) * LANES)
        qr.append(_rope(q_ref[:, sl] + bq_ref[:, sl], cos, sin) * scale)
    q = jnp.concatenate(qr, axis=1)
    kn = jnp.concatenate(
        [_rope(k_ref[:, c * LANES:(c + 1) * LANES] + bk_ref[:, c * LANES:(c + 1) * LANES], cos, sin)
         for c in range(KV_WIDTH // LANES)], axis=1)
    vn = v_ref[...] + bv_ref[...]
    lb = WINDOW
    for b in range(ATTN_SB):
        ko_ref[b, 0:lb - 1, :] = ck_ref[b, 1:lb, :]
        ko_ref[b, lb - 1:lb, :] = kn[b:b + 1, :]
        vo_ref[b, 0:lb - 1, :] = cv_ref[b, 1:lb, :]
        vo_ref[b, lb - 1:lb, :] = vn[b:b + 1, :]
        kb = ko_ref[b]
        vb = vo_ref[b]
        prod = jnp.concatenate(
            [kb * q[b:b + 1, g * KV_WIDTH:(g + 1) * KV_WIDTH] for g in range(ATTN_GROUP)], axis=0)
        s = _dot(prod.astype(BF16), e_ref[...])
        ps = []
        for g in range(ATTN_GROUP):
            sg = s[g * lb:(g + 1) * lb, :]
            sink = sink_ref[g:g + 1, :]
            m = jnp.maximum(jnp.max(sg, axis=0, keepdims=True), sink)
            e = jnp.exp(sg - m)
            den = jnp.sum(e, axis=0, keepdims=True) + jnp.exp(sink - m)
            ps.append(e / den)
        pe = _dot(jnp.concatenate(ps, axis=0).astype(BF16), et_ref[...])
        for g in range(ATTN_GROUP):
            og = jnp.sum(pe[g * lb:(g + 1) * lb, :] * vb, axis=0, keepdims=True)
            o_ref[b:b + 1, g * KV_WIDTH:(g + 1) * KV_WIDTH] = og


def _attn_sample(q_perm, k_new, v_new, bq_perm, bk, bv, cos_row, sin_row, sinks_gk, cache_k, cache_v):
    n = DEC_BATCH
    sb = ATTN_SB
    head_of_lane = jnp.arange(KV_WIDTH) // ATTN_HEAD_DIM
    e_mat = (head_of_lane[:, None] == jnp.arange(LANES)[None, :]).astype(BF16)
    row = lambda w: pl.BlockSpec((sb, w), lambda i: (i, 0))
    vec = lambda w: pl.BlockSpec((1, w), lambda i: (0, 0))
    full = lambda a: pl.BlockSpec(a.shape, lambda i: (0,) * a.ndim)
    cache = pl.BlockSpec((sb, WINDOW, KV_WIDTH), lambda i: (i, 0, 0))
    return pl.pallas_call(
        _attn_sample_kernel,
        grid=(n // sb,),
        in_specs=[row(ATTN_WIDTH), row(KV_WIDTH), row(KV_WIDTH),
                  vec(ATTN_WIDTH), vec(KV_WIDTH), vec(KV_WIDTH), vec(LANES), vec(LANES),
                  full(sinks_gk), full(e_mat), full(e_mat.T), cache, cache],
        out_specs=[row(ATTN_WIDTH), cache, cache],
        out_shape=[jax.ShapeDtypeStruct((n, ATTN_WIDTH), F32),
                   jax.ShapeDtypeStruct(cache_k.shape, F32),
                   jax.ShapeDtypeStruct(cache_v.shape, F32)],
        compiler_params=_cparams(("parallel",)),
        name="attn_sample",
    )(q_perm, k_new, v_new, bq_perm, bk, bv, cos_row, sin_row, sinks_gk, e_mat, e_mat.T,
      cache_k, cache_v)


DN_ROWS = 512
DN_PAIR = 2 * DN_CHUNK
CONV_PAD = 8
DN_HPS = 4


def _dn_gates_kernel(ab_ref, alog_ref, dtb_ref, cum_ref, beta_ref):
    ab = ab_ref[...]
    g = -jnp.exp(alog_ref[...]) * _softplus(ab + dtb_ref[...])
    row = lax.broadcasted_iota(jnp.int32, ab.shape, 0) % DN_CHUNK
    shift = 1
    while shift < DN_CHUNK:
        g = g + jnp.where(row >= shift, pltpu.roll(g, shift, 0), 0.0)
        shift *= 2
    cum_ref[...] = g
    beta_ref[...] = pltpu.roll(_sigmoid(ab), LANES - DN_HEADS, 1)


def _dn_gates(proj_ab, alog_pad, dtb_pad):
    blk = pl.BlockSpec((DN_ROWS, LANES), lambda i: (i, 0))
    vec = pl.BlockSpec((1, LANES), lambda i: (0, 0))
    out = jax.ShapeDtypeStruct((SEQ, LANES), F32)
    return pl.pallas_call(
        _dn_gates_kernel,
        grid=(SEQ // DN_ROWS,),
        in_specs=[blk, vec, vec],
        out_specs=[blk, blk],
        out_shape=[out, out],
        compiler_params=_cparams(("parallel",)),
        name="dn_gates",
    )(proj_ab, alog_pad, dtb_pad)


def _dn_prompt_kernel(q_ref, qp_ref, k_ref, kp_ref, v_ref, vp_ref, z_ref, cum_ref, beta_ref,
                      wq_ref, wk_ref, wv_ref, nw_ref, o_ref, so_ref, xq_s, xk_s, xv_s, s_s):
    hp = pl.program_id(0)
    i = pl.program_id(1)
    pr = DN_PAIR
    cl = DN_CHUNK

    @pl.when(i == 0)
    def _():
        s_s[...] = jnp.zeros_like(s_s)

    for x_ref, prev_ref, pad_s in ((q_ref, qp_ref, xq_s), (k_ref, kp_ref, xk_s), (v_ref, vp_ref, xv_s)):
        pad_s[0:CONV_PAD, :] = jnp.where(i > 0, prev_ref[...], 0.0)
        pad_s[CONV_PAD:, :] = x_ref[...]

    def lanes_of(hh):
        return slice(hh * DN_DIM, (hh + 1) * DN_DIM)

    def conv(pad_s, w_ref, hh, r0):
        y = None
        for j in range(CONV_WIDTH):
            start = CONV_PAD + r0 - (CONV_WIDTH - 1 - j)
            term = w_ref[j:j + 1, lanes_of(hh)] * pad_s[start:start + pr, lanes_of(hh)]
            y = term if y is None else y + term
        return _silu(y)

    def l2n(x):
        return x * lax.rsqrt(jnp.sum(x * x, axis=-1, keepdims=True) + RMS_EPS)

    def gate(ref, hh, r0):
        pick = ci == hp * DN_HPS + hh
        col = jnp.sum(jnp.where(pick, ref[r0:r0 + pr, :], 0.0), axis=1, keepdims=True)
        return jnp.broadcast_to(col, (pr, pr))

    ri = lax.broadcasted_iota(jnp.int32, (pr, pr), 0)
    ci = lax.broadcasted_iota(jnp.int32, (pr, pr), 1)
    same_chunk = (ri // cl) == (ci // cl)
    causal = same_chunk & (ri >= ci)
    strict = same_chunk & (ri > ci)
    eye = (ri == ci).astype(F32)
    first_rows = ri < cl
    first_rows2 = lax.broadcasted_iota(jnp.int32, (pr, 2 * DN_DIM), 0) < cl
    nw = nw_ref[...]
    bf = lambda a: a.astype(BF16)
    cat0 = lambda *a: jnp.concatenate(a, axis=0)
    cat1 = lambda *a: jnp.concatenate(a, axis=1)

    n_pairs = DN_ROWS // pr
    probs = [(hh, a * pr) for a in range(n_pairs) for hh in range(DN_HPS)]

    q_p = [l2n(conv(xq_s, wq_ref, hh, r0)) * (DN_DIM ** -0.5) for hh, r0 in probs]
    k_p = [l2n(conv(xk_s, wk_ref, hh, r0)) for hh, r0 in probs]
    v_p = [conv(xv_s, wv_ref, hh, r0) for hh, r0 in probs]
    cum = [gate(cum_ref, hh, r0) for hh, r0 in probs]
    beta = [gate(beta_ref, hh, r0) for hh, r0 in probs]
    gamma = [jnp.exp(c) for c in cum]
    decay = [jnp.where(causal, jnp.exp(c - c.T), 0.0) for c in cum]
    kq = [_dot_nt(bf(cat0(k, q)), bf(k)) for k, q in zip(k_p, q_p)]
    x = [jnp.where(strict, -(b * m[:pr] * d), 0.0) for b, m, d in zip(beta, kq, decay)]
    att = [bf(m[pr:] * d) for m, d in zip(kq, decay)]
    t = [eye + m for m in x]
    xp = [_dot(bf(m), bf(m)) for m in x]
    for _ in range(int(math.log2(cl)) - 2):
        y = [_dot(bf(p), cat1(bf(p), bf(m))) for p, m in zip(xp, t)]
        xp = [m[:, :pr] for m in y]
        t = [m + n[:, pr:] for m, n in zip(t, y)]
    t = [m + _dot(bf(p), bf(m)) for p, m in zip(xp, t)]
    wu = [_dot(bf(m), bf(cat1(k * (b * g), v * b)))
          for m, k, v, b, g in zip(t, k_p, v_p, beta, gamma)]
    awu = [_dot(a, bf(m)) for a, m in zip(att, wu)]
    q_prime = [bf(q * g - m[:, :DN_DIM]) for q, g, m in zip(q_p, gamma, awu)]
    au = [m[:, DN_DIM:] for m in awu]
    k_dec_t = [bf((k * jnp.exp(jnp.where(first_rows, c[cl - 1:cl, :], c[pr - 1:pr, :]) - c)).T)
               for k, c in zip(k_p, cum)]
    kwu1 = [_dot(kt, bf(jnp.where(first_rows2, m, 0.0))) for kt, m in zip(k_dec_t, wu)]
    kwu2 = [_dot(kt, bf(jnp.where(first_rows2, 0.0, m))) for kt, m in zip(k_dec_t, wu)]
    g1 = [jnp.exp(c[cl - 1:cl, :]) for c in cum]
    g2 = [jnp.exp(c[pr - 1:pr, :]) for c in cum]
    comp = [_dot(bf(-m2[:, :DN_DIM]), bf(cat1(-m1[:, :DN_DIM], m1[:, DN_DIM:]))) for m1, m2 in zip(kwu1, kwu2)]
    mp12 = [-(b * m1[:, :DN_DIM]) - a * m2[:, :DN_DIM] + c[:, :DN_DIM]
            for a, b, m1, m2, c in zip(g1, g2, kwu1, kwu2, comp)]
    n12 = [b * m1[:, DN_DIM:] + c[:, DN_DIM:] + m2[:, DN_DIM:] for b, m1, m2, c in zip(g2, kwu1, kwu2, comp)]
    lhs = [cat0(qp[:cl], bf(-m1[:, :DN_DIM]), bf(m)) for qp, m1, m in zip(q_prime, kwu1, mp12)]

    s = [s_s[hh] for hh in range(DN_HPS)]
    for a in range(n_pairs):
        js = [a * DN_HPS + hh for hh in range(DN_HPS)]
        r = [_dot(lhs[j], bf(s[hh])) for hh, j in enumerate(js)]
        s_mid = [g1[j] * s[hh] + r[hh][cl:cl + DN_DIM] + kwu1[j][:, DN_DIM:] for hh, j in enumerate(js)]
        s_new = [(g1[j] * g2[j]) * s[hh] + r[hh][cl + DN_DIM:] + n12[j] for hh, j in enumerate(js)]
        o2 = [_dot(q_prime[j][cl:], bf(s_mid[hh])) for hh, j in enumerate(js)]
        for hh, j in enumerate(js):
            o = cat0(r[hh][:cl], o2[hh]) + au[j]
            o = o * lax.rsqrt(jnp.mean(o * o, axis=-1, keepdims=True) + RMS_EPS) * nw
            r0 = a * pr
            o_ref[r0:r0 + pr, lanes_of(hh)] = (o * _silu(z_ref[r0:r0 + pr, lanes_of(hh)])).astype(o_ref.dtype)
        s = s_new

    for hh in range(DN_HPS):
        s_s[hh] = s[hh]

    @pl.when(i == pl.num_programs(1) - 1)
    def _():
        for hh in range(DN_HPS):
            so_ref[hh] = s[hh]


def _dn_prompt(proj, cum, beta, w_conv, norm_w):
    rows = DN_ROWS
    width = DN_HPS * DN_DIM
    q0 = ATTN_QKV_WIDTH // width
    k0 = q0 + DN_WIDTH // width
    v0 = k0 + DN_WIDTH // width
    z0 = v0 + DN_WIDTH // width
    per = rows // CONV_PAD

    def blk(c0):
        return pl.BlockSpec((rows, width), lambda h, i: (i, c0 + h))

    def prev(c0):
        return pl.BlockSpec((CONV_PAD, width), lambda h, i: (jnp.maximum(i * per - 1, 0), c0 + h))

    def wblk(c0):
        return pl.BlockSpec((CONV_WIDTH, width), lambda h, i: (0, c0 + h))

    gate = pl.BlockSpec((rows, LANES), lambda h, i: (i, 0))
    vec = pl.BlockSpec((1, DN_DIM), lambda h, i: (0, 0))
    pad = pltpu.VMEM((rows + CONV_PAD, width), F32)
    return pl.pallas_call(
        _dn_prompt_kernel,
        grid=(DN_HEADS // DN_HPS, SEQ // rows),
        in_specs=[blk(q0), prev(q0), blk(k0), prev(k0), blk(v0), prev(v0), blk(z0), gate, gate,
                  wblk(0), wblk(DN_WIDTH // width), wblk(2 * DN_WIDTH // width), vec],
        out_specs=[pl.BlockSpec((rows, width), lambda h, i: (i, h)),
                   pl.BlockSpec((DN_HPS, DN_DIM, DN_DIM), lambda h, i: (h, 0, 0))],
        out_shape=[jax.ShapeDtypeStruct((SEQ, DN_WIDTH), BF16),
                   jax.ShapeDtypeStruct((DN_HEADS, DN_DIM, DN_DIM), F32)],
        scratch_shapes=[pad, pad, pad, pltpu.VMEM((DN_HPS, DN_DIM, DN_DIM), F32)],
        compiler_params=_cparams(("parallel", "arbitrary")),
        name="dn_prompt",
    )(proj, proj, proj, proj, proj, proj, proj, cum, beta, w_conv, w_conv, w_conv,
      norm_w.reshape(1, DN_DIM))


DN_SB = 8


def _dn_gates_of_head(ab, alog, dtb, h):
    lane = lax.broadcasted_iota(jnp.int32, ab.shape, 1)
    g_all = -jnp.exp(alog) * _softplus(ab + dtb)
    g = jnp.sum(jnp.where(lane == h, g_all, 0.0), axis=1, keepdims=True)
    beta = jnp.sum(jnp.where(lane == h + DN_HEADS, _sigmoid(ab), 0.0), axis=1, keepdims=True)
    return jnp.broadcast_to(g, ab.shape), jnp.broadcast_to(beta, ab.shape)


def _dn_sample_prep_kernel(x_ref, sc_ref, ab_ref, w_ref, alog_ref, dtb_ref,
                           qt_ref, kt_ref, vb_ref, bg_ref, gam_ref):
    ab = ab_ref[...]
    alog, dtb = alog_ref[...], dtb_ref[...]

    def conv(col):
        sl = slice(col, col + LANES)
        y = w_ref[CONV_WIDTH - 1:CONV_WIDTH, sl] * x_ref[:, sl]
        for j in range(CONV_WIDTH - 1):
            y = y + w_ref[j:j + 1, sl] * sc_ref[:, j * CONV_DIM + col:j * CONV_DIM + col + LANES]
        return _silu(y)

    def l2n(x):
        return x * lax.rsqrt(jnp.sum(x * x, axis=-1, keepdims=True) + RMS_EPS)

    for h in range(DN_HEADS):
        q = l2n(conv(h * DN_DIM)) * (DN_DIM ** -0.5)
        k = l2n(conv(DN_WIDTH + h * DN_DIM))
        v = conv(2 * DN_WIDTH + h * DN_DIM)
        g_b, beta_b = _dn_gates_of_head(ab, alog, dtb, h)
        gamma = jnp.exp(g_b)
        qt_ref[h] = q.T
        kt_ref[h] = k.T
        vb_ref[h] = v * beta_b
        bg_ref[h] = beta_b * gamma
        gam_ref[h] = gamma


def _dn_sample_prep(qkv_s, conv_state, ab_s, w_conv, alog_pad, dtb_pad):
    n = DEC_BATCH
    full = lambda a: pl.BlockSpec(a.shape, lambda i: (0,) * a.ndim)
    hm = jax.ShapeDtypeStruct((DN_HEADS, n, DN_DIM), F32)
    hm_spec = pl.BlockSpec((DN_HEADS, n, DN_DIM), lambda i: (0, 0, 0))
    args = (qkv_s, conv_state, ab_s, w_conv, alog_pad, dtb_pad)
    return pl.pallas_call(
        _dn_sample_prep_kernel,
        grid=(1,),
        in_specs=[full(a) for a in args],
        out_specs=[hm_spec] * 5,
        out_shape=[hm] * 5,
        compiler_params=_cparams(("arbitrary",)),
        name="dn_sample_prep",
    )(*args)


def _dn_sample_kernel(s_ref, qt_ref, kt_ref, vb_ref, bg_ref, gam_ref, z_ref, nw_ref,
                      so_ref, o_ref, o_s):
    step = pl.program_id(0)
    lane = lax.broadcasted_iota(jnp.int32, (DN_DIM, DEC_BATCH), 1)

    def head(h, carry):
        kt = kt_ref[h]
        qt = qt_ref[h]
        for bb in range(DN_SB):
            b = step * DN_SB + bb
            pick = lane == b
            kcol = jnp.sum(jnp.where(pick, kt, 0.0), axis=1, keepdims=True)
            qcol = jnp.sum(jnp.where(pick, qt, 0.0), axis=1, keepdims=True)
            s = s_ref[bb, h]
            ks = jnp.sum(kcol * s, axis=0, keepdims=True)
            u = vb_ref[h, pl.ds(b, 1), :] - bg_ref[h, pl.ds(b, 1), :] * ks
            s_new = gam_ref[h, pl.ds(b, 1), :] * s + kcol * u
            so_ref[bb, h] = s_new
            o_s[h, pl.ds(bb, 1), :] = jnp.sum(qcol * s_new, axis=0, keepdims=True)
        return carry

    lax.fori_loop(0, DN_HEADS, head, 0)
    nw = nw_ref[...]
    for h in range(DN_HEADS):
        o = o_s[h]
        o = o * lax.rsqrt(jnp.mean(o * o, axis=-1, keepdims=True) + RMS_EPS) * nw
        sl = slice(h * DN_DIM, (h + 1) * DN_DIM)
        o_ref[:, sl] = o * _silu(z_ref[:, sl])


def _dn_sample(state, qt, kt, vb, bg, gam, z_s, norm_w):
    n = DEC_BATCH
    sb = DN_SB
    sblk = pl.BlockSpec((sb, DN_HEADS, DN_DIM, DN_DIM), lambda i: (i, 0, 0, 0))
    hm = pl.BlockSpec((DN_HEADS, n, DN_DIM), lambda i: (0, 0, 0))
    row = pl.BlockSpec((sb, DN_WIDTH), lambda i: (i, 0))
    return pl.pallas_call(
        _dn_sample_kernel,
        grid=(n // sb,),
        in_specs=[sblk, hm, hm, hm, hm, hm, row, pl.BlockSpec((1, DN_DIM), lambda i: (0, 0))],
        out_specs=[sblk, row],
        out_shape=[jax.ShapeDtypeStruct(state.shape, F32), jax.ShapeDtypeStruct((n, DN_WIDTH), F32)],
        scratch_shapes=[pltpu.VMEM((DN_HEADS, sb, DN_DIM), F32)],
        compiler_params=_cparams(("parallel",)),
        name="dn_sample",
    )(state, qt, kt, vb, bg, gam, z_s, norm_w.reshape(1, DN_DIM))


BM = 1040
BN = 1024
BF_FFN = 256


def _rope_tables(pos):
    half = ATTN_HEAD_DIM // 2
    inv_freq = ROPE_THETA ** (-jnp.arange(half, dtype=F32) / half)
    ang = pos.astype(F32)[:, None] * inv_freq
    cos, sin = jnp.cos(ang), jnp.sin(ang)
    reps = LANES // ATTN_HEAD_DIM
    return jnp.tile(cos, (1, 2 * reps)), jnp.tile(jnp.concatenate([-sin, sin], axis=1), (1, reps))


def _group_major(a):
    lead = a.shape[:-1]
    a = a.reshape(lead + (ATTN_KV_HEADS, ATTN_GROUP, ATTN_HEAD_DIM))
    return jnp.swapaxes(a, -3, -2).reshape(lead + (ATTN_WIDTH,))


def _layer(x_prompt, x_sample, cache_k, cache_v, state_conv, state_delta, w_in, b_attn, attn_sinks,
           w_conv, dn_a_log, dn_dt_bias, dn_norm_w, w_out, ln1_g, ln1_b, w_gate, w_up, w_down,
           ln2_g, ln2_b):
    n_s = DEC_BATCH
    xb = _xcast(x_prompt, x_sample)
    w_in_t = w_in.T
    w_main = _cast_col_blocks([w_in_t], D_MODEL, 1, BN, MAIN_WIDTH).reshape(MAIN_WIDTH // BN, BN, D_MODEL)
    w_ab = jnp.pad(w_in_t[MAIN_WIDTH:], ((0, LANES - 2 * DN_HEADS), (0, 0))).astype(BF16)
    proj = _matmul_nt(xb, w_main, BM)
    proj_ab = _matmul_nt(xb, w_ab[None], BM)

    pad16 = lambda v: jnp.pad(v, (0, LANES - DN_HEADS)).reshape(1, LANES)
    alog_pad, dtb_pad = pad16(dn_a_log), pad16(dn_dt_bias)

    cos_t, sin_t = _rope_tables(jnp.arange(SEQ, dtype=jnp.int32))
    attn_p, pk, pv = _attn_prompt(proj, b_attn, attn_sinks, cos_t, sin_t)
    cum, beta = _dn_gates(proj_ab, alog_pad, dtb_pad)
    dn_p, ps = _dn_prompt(proj, cum, beta, w_conv, dn_norm_w)
    pc = proj[SEQ - (CONV_WIDTH - 1):SEQ, ATTN_QKV_WIDTH:ATTN_QKV_WIDTH + CONV_DIM]

    proj_s = proj[SEQ:]
    cos_r, sin_r = _rope_tables(jnp.full((1,), PAST_LEN, jnp.int32))
    sinks_gk = jnp.pad(attn_sinks.reshape(ATTN_KV_HEADS, ATTN_GROUP).T,
                       ((0, 0), (0, LANES - ATTN_KV_HEADS)))
    attn_s_perm, sk, sv = _attn_sample(
        _group_major(proj_s[:, :ATTN_WIDTH]),
        proj_s[:, ATTN_WIDTH:ATTN_WIDTH + KV_WIDTH],
        proj_s[:, ATTN_WIDTH + KV_WIDTH:ATTN_QKV_WIDTH],
        _group_major(b_attn[:ATTN_WIDTH]).reshape(1, ATTN_WIDTH),
        b_attn[ATTN_WIDTH:ATTN_WIDTH + KV_WIDTH].reshape(1, KV_WIDTH),
        b_attn[ATTN_WIDTH + KV_WIDTH:].reshape(1, KV_WIDTH),
        cos_r, sin_r, sinks_gk, cache_k.reshape(n_s, WINDOW, KV_WIDTH),
        cache_v.reshape(n_s, WINDOW, KV_WIDTH))
    attn_s = jnp.swapaxes(attn_s_perm.reshape(n_s, ATTN_GROUP, ATTN_KV_HEADS, ATTN_HEAD_DIM), 1, 2)
    attn_s = attn_s.reshape(n_s, ATTN_WIDTH)

    qkv_s = proj_s[:, ATTN_QKV_WIDTH:ATTN_QKV_WIDTH + CONV_DIM]
    z_s = proj_s[:, ATTN_QKV_WIDTH + CONV_DIM:]
    qt, kt, vb, bg, gam = _dn_sample_prep(
        qkv_s, state_conv.reshape(n_s, (CONV_WIDTH - 1) * CONV_DIM), proj_ab[SEQ:],
        w_conv, alog_pad, dtb_pad)
    ss, dn_s = _dn_sample(state_delta, qt, kt, vb, bg, gam, z_s, dn_norm_w)
    sc = jnp.concatenate([state_conv[:, 1:], qkv_s[:, None, :]], axis=1)

    attn_all = jnp.concatenate([attn_p, attn_s.astype(BF16)], axis=0)
    dn_all = jnp.concatenate([dn_p, dn_s.astype(BF16)], axis=0)
    mixed = _out_proj(attn_all, dn_all, _cast_col_blocks([w_out], BN, D_MODEL // BN, 1024), BM)
    h32, hb = _ln1(x_prompt, x_sample, mixed, ln1_g, ln1_b)
    w_d = _cast_col_blocks([w_down], D_MODEL, 1, BF_FFN).reshape(FFN_HIDDEN // BF_FFN, BF_FFN, D_MODEL)
    ffn = _ffn(hb, w_gate, w_up, w_d, BM)
    y_p, y_s = _ln2(h32, ffn, ln2_g, ln2_b)
    return (y_p, y_s, pk.reshape(WINDOW, ATTN_KV_HEADS, ATTN_HEAD_DIM),
            pv.reshape(WINDOW, ATTN_KV_HEADS, ATTN_HEAD_DIM), pc, ps,
            sk.reshape(n_s, WINDOW, ATTN_KV_HEADS, ATTN_HEAD_DIM),
            sv.reshape(n_s, WINDOW, ATTN_KV_HEADS, ATTN_HEAD_DIM), sc, ss)


def kernel(x_prompt, x_sample, cache_swa_k, cache_swa_v, state_conv, state_delta, w_in, b_attn,
           attn_sinks, w_conv, dn_a_log, dn_dt_bias, dn_norm_w, w_out, ln1_g, ln1_b, w_gate, w_up,
           w_down, ln2_g, ln2_b):
    assert x_prompt.shape == (1, SEQ, D_MODEL) and x_sample.shape == (DEC_BATCH, 1, D_MODEL)
    assert w_in.shape[0] == 1, "one layer"
    y_p, y_s, pk, pv, pc, ps, sk, sv, sc, ss = _layer(
        x_prompt[0], x_sample[:, 0], cache_swa_k[0], cache_swa_v[0], state_conv[0], state_delta[0],
        w_in[0], b_attn[0], attn_sinks[0], w_conv[0], dn_a_log[0], dn_dt_bias[0], dn_norm_w[0],
        w_out[0], ln1_g[0], ln1_b[0], w_gate[0], w_up[0], w_down[0], ln2_g[0], ln2_b[0])
    return (y_p[None], y_s[:, None], pk[None, None], pv[None, None], pc[None, None],
            ps[None, None], sk[None], sv[None], sc[None], ss[None])
```

```python
import math

import jax
import jax.numpy as jnp
from jax import lax
from jax.experimental import pallas as pl
from jax.experimental.pallas import tpu as pltpu

D_MODEL = 4096
SEQ = 8192
DEC_BATCH = 128
PAST_LEN = 8192
ROWS = SEQ + DEC_BATCH

ATTN_HEADS = 32
ATTN_KV_HEADS = 8
ATTN_HEAD_DIM = 64
ATTN_GROUP = ATTN_HEADS // ATTN_KV_HEADS
ATTN_WIDTH = ATTN_HEADS * ATTN_HEAD_DIM
KV_WIDTH = ATTN_KV_HEADS * ATTN_HEAD_DIM
WINDOW = 128
ROPE_THETA = 10000.0
DN_HEADS = 16
DN_DIM = 128
DN_WIDTH = DN_HEADS * DN_DIM
CONV_WIDTH = 4
CONV_DIM = 3 * DN_WIDTH
DN_CHUNK = 64
ATTN_QKV_WIDTH = ATTN_WIDTH + 2 * KV_WIDTH
MAIN_WIDTH = ATTN_QKV_WIDTH + CONV_DIM + DN_WIDTH
FFN_HIDDEN = 11008
DEEPNORM_ALPHA = 2.0 ** 0.25
LN_EPS = 1e-5
RMS_EPS = 1e-6

LANES = 128
VMEM_LIMIT = 56 * 1024 * 1024
VMEM_LIMIT_FFN = 61 * 1024 * 1024

F32 = jnp.float32
BF16 = jnp.bfloat16


def _cparams(sem, vmem=VMEM_LIMIT):
    return pltpu.CompilerParams(dimension_semantics=sem, vmem_limit_bytes=vmem)


def _dot(a, b):
    return jnp.dot(a, b, preferred_element_type=F32)


def _dot_nt(a, b):
    return lax.dot_general(a, b, (((1,), (1,)), ((), ())), preferred_element_type=F32)


def _sigmoid(x):
    return 1.0 / (1.0 + jnp.exp(-x))


def _silu(x):
    return x * _sigmoid(x)


def _softplus(x):
    return jnp.maximum(x, 0.0) + jnp.log(1.0 + jnp.exp(-jnp.abs(x)))


def _rope(x, cos, sin_signed):
    lane = lax.broadcasted_iota(jnp.int32, x.shape, 1)
    first_half = (lane % ATTN_HEAD_DIM) < (ATTN_HEAD_DIM // 2)
    partner = jnp.where(first_half, pltpu.roll(x, LANES - ATTN_HEAD_DIM // 2, 1),
                        pltpu.roll(x, ATTN_HEAD_DIM // 2, 1))
    return x * cos + partner * sin_signed


def _mm_nt_kernel(x_ref, wt_ref, o_ref):
    o_ref[...] = _dot_nt(x_ref[...], wt_ref[...].astype(BF16)).astype(o_ref.dtype)


def _matmul_nt(x, wt, bm, bn, nb, out_dtype=F32):
    m, k = x.shape
    return pl.pallas_call(
        _mm_nt_kernel,
        grid=(m // bm, nb),
        in_specs=[pl.BlockSpec((bm, k), lambda i, j: (i, 0)),
                  pl.BlockSpec((bn, k), lambda i, j: (j, 0))],
        out_specs=pl.BlockSpec((bm, bn), lambda i, j: (i, j)),
        out_shape=jax.ShapeDtypeStruct((m, nb * bn), out_dtype),
        compiler_params=_cparams(("parallel", "parallel")),
        name="matmul",
    )(x, wt)


def _out_proj_kernel(ap_ref, dp_ref, as_ref, ds_ref, w_ref, op_ref, os_ref, wb_s):
    i = pl.program_id(1)
    last = pl.num_programs(1) - 1
    ka = ap_ref.shape[1]

    @pl.when(i == 0)
    def _():
        wb_s[...] = w_ref[...].astype(BF16)

    @pl.when(i < last)
    def _():
        op_ref[...] = _dot(ap_ref[...], wb_s[:ka, :]) + _dot(dp_ref[...], wb_s[ka:, :])

    @pl.when(i == last)
    def _():
        os_ref[...] = _dot(as_ref[...], wb_s[:ka, :]) + _dot(ds_ref[...], wb_s[ka:, :])


def _out_proj(a_p, d_p, a_s, d_s, w, bm, bn):
    m, ka = a_p.shape
    kd = d_p.shape[1]
    ms = a_s.shape[0]
    n = w.shape[1]
    nt = m // bm
    tile = lambda width: pl.BlockSpec((bm, width), lambda j, i: (jnp.minimum(i, nt - 1), 0))
    dec = lambda width: pl.BlockSpec((ms, width), lambda j, i: (0, 0))
    return pl.pallas_call(
        _out_proj_kernel,
        grid=(n // bn, nt + 1),
        in_specs=[tile(ka), tile(kd), dec(ka), dec(kd),
                  pl.BlockSpec((ka + kd, bn), lambda j, i: (0, j))],
        out_specs=[pl.BlockSpec((bm, bn), lambda j, i: (jnp.minimum(i, nt - 1), j)),
                   pl.BlockSpec((ms, bn), lambda j, i: (0, j))],
        out_shape=[jax.ShapeDtypeStruct((m, n), F32), jax.ShapeDtypeStruct((ms, n), F32)],
        scratch_shapes=[pltpu.VMEM((ka + kd, bn), BF16)],
        compiler_params=_cparams(("parallel", "arbitrary")),
        name="out_proj",
    )(a_p, d_p, a_s, d_s, w)


FFN_NCHUNK = 1024
FFN_ROW_ALIGN = 16


def _ffn_kernel(h_ref, wg_ref, wu_ref, wd_ref, o_ref):
    f = pl.program_id(1)

    @pl.when(f == 0)
    def _():
        o_ref[...] = jnp.zeros_like(o_ref)

    bm = h_ref.shape[0]
    cut = (bm // 2) // FFN_ROW_ALIGN * FFN_ROW_ALIGN
    groups = ((0, cut), (cut, bm))
    wg = wg_ref[...].astype(BF16)
    wu = wu_ref[...].astype(BF16)
    gates = [_dot(h_ref[a:b, :], wg) for a, b in groups]
    ups = [_dot(h_ref[a:b, :], wu) for a, b in groups]
    acts = [(_silu(g) * u).astype(BF16) for g, u in zip(gates, ups)]
    for n in range(0, o_ref.shape[1], FFN_NCHUNK):
        wd = wd_ref[:, n:n + FFN_NCHUNK].astype(BF16)
        for (a, b), act in zip(groups, acts):
            o_ref[a:b, n:n + FFN_NCHUNK] += _dot(act, wd)


def _ffn(h, w_gate, w_up, w_down, bm, bf):
    m, d = h.shape
    hidden = w_gate.shape[1]
    once = pl.Buffered(1)
    return pl.pallas_call(
        _ffn_kernel,
        grid=(m // bm, hidden // bf),
        in_specs=[pl.BlockSpec((bm, d), lambda i, f: (i, 0), pipeline_mode=once),
                  pl.BlockSpec((d, bf), lambda i, f: (0, f)),
                  pl.BlockSpec((d, bf), lambda i, f: (0, f)),
                  pl.BlockSpec((bf, d), lambda i, f: (f, 0))],
        out_specs=pl.BlockSpec((bm, d), lambda i, f: (i, 0), pipeline_mode=once),
        out_shape=jax.ShapeDtypeStruct((m, d), F32),
        compiler_params=_cparams(("parallel", "arbitrary"), VMEM_LIMIT_FFN),
        name="ffn",
    )(h, w_gate, w_up, w_down)


BR = WINDOW
NB_PROMPT = SEQ // BR


def _deepnorm(x, mixed, g, b):
    v = DEEPNORM_ALPHA * x + mixed
    mu = jnp.mean(v, axis=-1, keepdims=True)
    c = v - mu
    var = jnp.mean(c * c, axis=-1, keepdims=True)
    return c * lax.rsqrt(var + LN_EPS) * g + b


def _prompt_rows(width):
    return pl.BlockSpec((BR, width), lambda i: (jnp.minimum(i, NB_PROMPT - 1), 0))


def _decode_rows(width):
    return pl.BlockSpec((BR, width), lambda i: (0, 0))


def _xcast_kernel(xp_ref, xs_ref, o_ref):
    i = pl.program_id(0)
    o_ref[...] = jnp.where(i < NB_PROMPT, xp_ref[...], xs_ref[...]).astype(o_ref.dtype)


def _xcast(x_prompt, x_sample):
    d = x_prompt.shape[1]
    return pl.pallas_call(
        _xcast_kernel,
        grid=(NB_PROMPT + 1,),
        in_specs=[_prompt_rows(d), _decode_rows(d)],
        out_specs=pl.BlockSpec((BR, d), lambda i: (i, 0)),
        out_shape=jax.ShapeDtypeStruct((ROWS, d), BF16),
        compiler_params=_cparams(("arbitrary",)),
        name="xcast",
    )(x_prompt, x_sample)


def _ln1_kernel(xp_ref, xs_ref, mp_ref, ms_ref, g_ref, b_ref, o_ref, ob_ref):
    i = pl.program_id(0)
    is_prompt = i < NB_PROMPT
    x = jnp.where(is_prompt, xp_ref[...], xs_ref[...])
    mixed = jnp.where(is_prompt, mp_ref[...], ms_ref[...])
    y = _deepnorm(x, mixed, g_ref[...], b_ref[...])
    o_ref[...] = y
    ob_ref[...] = y.astype(BF16)


def _ln1(x_prompt, x_sample, mixed_p, mixed_s, g, b):
    d = mixed_p.shape[1]
    row = pl.BlockSpec((BR, d), lambda i: (i, 0))
    vec = pl.BlockSpec((1, d), lambda i: (0, 0))
    return pl.pallas_call(
        _ln1_kernel,
        grid=(NB_PROMPT + 1,),
        in_specs=[_prompt_rows(d), _decode_rows(d), _prompt_rows(d), _decode_rows(d), vec, vec],
        out_specs=[row, row],
        out_shape=[jax.ShapeDtypeStruct((ROWS, d), F32), jax.ShapeDtypeStruct((ROWS, d), BF16)],
        compiler_params=_cparams(("arbitrary",)),
        name="ln1",
    )(x_prompt, x_sample, mixed_p, mixed_s, g.reshape(1, d), b.reshape(1, d))


def _ln2_kernel(h_ref, f_ref, g_ref, b_ref, yp_ref, ys_ref):
    i = pl.program_id(0)
    y = _deepnorm(h_ref[...], f_ref[...], g_ref[...], b_ref[...])

    @pl.when(i < NB_PROMPT)
    def _():
        yp_ref[...] = y

    @pl.when(i >= NB_PROMPT)
    def _():
        ys_ref[...] = y


def _ln2(h32, ffn, g, b):
    d = h32.shape[1]
    row = pl.BlockSpec((BR, d), lambda i: (i, 0))
    vec = pl.BlockSpec((1, d), lambda i: (0, 0))
    return pl.pallas_call(
        _ln2_kernel,
        grid=(NB_PROMPT + 1,),
        in_specs=[row, row, vec, vec],
        out_specs=[_prompt_rows(d), _decode_rows(d)],
        out_shape=[jax.ShapeDtypeStruct((SEQ, d), F32), jax.ShapeDtypeStruct((DEC_BATCH, d), F32)],
        compiler_params=_cparams(("arbitrary",)),
        name="ln2",
    )(h32, ffn, g.reshape(1, d), b.reshape(1, d))


def _attn_prompt_kernel(sink_ref, q_ref, kc_ref, kp_ref, vc_ref, vp_ref, b_ref,
                        cc_ref, sc_ref, cp_ref, sp_ref, o_ref, ko_ref, vo_ref):
    i = pl.program_id(0)
    w = WINDOW
    cos_c, sin_c = cc_ref[...], sc_ref[...]
    cos_p, sin_p = cp_ref[...], sp_ref[...]
    n_kchunk = KV_WIDTH // LANES

    k_cur, k_prev, v_cur, v_prev = [], [], [], []
    for c in range(n_kchunk):
        sl = slice(c * LANES, (c + 1) * LANES)
        bk = b_ref[:, ATTN_WIDTH + c * LANES:ATTN_WIDTH + (c + 1) * LANES]
        bv = b_ref[:, ATTN_WIDTH + KV_WIDTH + c * LANES:ATTN_WIDTH + KV_WIDTH + (c + 1) * LANES]
        kc = _rope(kc_ref[:, sl] + bk, cos_c, sin_c)
        kp = _rope(kp_ref[:, sl] + bk, cos_p, sin_p)
        vc = vc_ref[:, sl] + bv
        vp = vp_ref[:, sl] + bv
        ko_ref[:, sl] = kc
        vo_ref[:, sl] = vc
        k_cur.append(kc)
        k_prev.append(kp)
        v_cur.append(vc)
        v_prev.append(vp)

    rows = ATTN_GROUP * w
    r = lax.broadcasted_iota(jnp.int32, (rows, 2 * w), 0) % w
    col = lax.broadcasted_iota(jnp.int32, (rows, 2 * w), 1)
    valid = (col > r) & (col <= r + w) & ((col >= w) | (i > 0))
    row_id = lax.broadcasted_iota(jnp.int32, (rows, 1), 0)
    lane_k = lax.broadcasted_iota(jnp.int32, (2 * w, LANES), 1)
    lane_o = lax.broadcasted_iota(jnp.int32, (w, LANES), 1)
    scale = ATTN_HEAD_DIM ** -0.5

    k2s, v2s, q4s = [], [], []
    for hk in range(ATTN_KV_HEADS):
        kchunk, khalf = hk // 2, hk % 2
        k2 = jnp.concatenate([k_prev[kchunk], k_cur[kchunk]], axis=0)
        in_half = (lane_k // ATTN_HEAD_DIM) == khalf
        k2s.append(jnp.where(in_half, k2, 0.0).astype(BF16))
        v2s.append(jnp.concatenate([v_prev[kchunk], v_cur[kchunk]], axis=0).astype(BF16))
        qs = []
        for g in range(ATTN_GROUP):
            hq = hk * ATTN_GROUP + g
            qchunk, qhalf = hq // 2, hq % 2
            sl = slice(qchunk * LANES, (qchunk + 1) * LANES)
            qc = _rope(q_ref[:, sl] + b_ref[:, sl], cos_c, sin_c) * scale
            if qhalf != khalf:
                qc = pltpu.roll(qc, ATTN_HEAD_DIM, 1)
            qs.append(qc)
        q4s.append(jnp.concatenate(qs, axis=0).astype(BF16))
    scores = [_dot_nt(q4, k2) for q4, k2 in zip(q4s, k2s)]
    probs = []
    for hk, s in enumerate(scores):
        s = jnp.where(valid, s, -jnp.inf)
        sink = jnp.zeros((rows, 1), F32)
        for g in range(ATTN_GROUP):
            sink = jnp.where(row_id // w == g, sink_ref[hk * ATTN_GROUP + g], sink)
        m = jnp.maximum(jnp.max(s, axis=-1, keepdims=True), sink)
        e = jnp.exp(s - m)
        den = jnp.sum(e, axis=-1, keepdims=True) + jnp.exp(sink - m)
        probs.append((e / den).astype(BF16))
    pvs = [_dot(p, v2) for p, v2 in zip(probs, v2s)]
    for hk, pv in enumerate(pvs):
        khalf = hk % 2
        outs = []
        for g in range(ATTN_GROUP):
            og = pv[g * w:(g + 1) * w, :]
            if (g % 2) != khalf:
                og = pltpu.roll(og, ATTN_HEAD_DIM, 1)
            outs.append(og)
        for j in range(ATTN_GROUP // 2):
            oc = jnp.where(lane_o < ATTN_HEAD_DIM, outs[2 * j], outs[2 * j + 1])
            c = hk * (ATTN_GROUP // 2) + j
            o_ref[:, c * LANES:(c + 1) * LANES] = oc.astype(o_ref.dtype)


def _attn_prompt(proj, b_attn, sinks, cos_t, sin_t):
    nb = SEQ // WINDOW
    kcol = ATTN_WIDTH // KV_WIDTH
    prev = lambda i: jnp.maximum(i - 1, 0)
    return pl.pallas_call(
        _attn_prompt_kernel,
        grid=(nb,),
        in_specs=[pl.BlockSpec(memory_space=pltpu.SMEM),
                  pl.BlockSpec((WINDOW, ATTN_WIDTH), lambda i: (i, 0)),
                  pl.BlockSpec((WINDOW, KV_WIDTH), lambda i: (i, kcol)),
                  pl.BlockSpec((WINDOW, KV_WIDTH), lambda i: (prev(i), kcol)),
                  pl.BlockSpec((WINDOW, KV_WIDTH), lambda i: (i, kcol + 1)),
                  pl.BlockSpec((WINDOW, KV_WIDTH), lambda i: (prev(i), kcol + 1)),
                  pl.BlockSpec((1, ATTN_QKV_WIDTH), lambda i: (0, 0)),
                  pl.BlockSpec((WINDOW, LANES), lambda i: (i, 0)),
                  pl.BlockSpec((WINDOW, LANES), lambda i: (i, 0)),
                  pl.BlockSpec((WINDOW, LANES), lambda i: (prev(i), 0)),
                  pl.BlockSpec((WINDOW, LANES), lambda i: (prev(i), 0))],
        out_specs=[pl.BlockSpec((WINDOW, ATTN_WIDTH), lambda i: (i, 0)),
                   pl.BlockSpec((WINDOW, KV_WIDTH), lambda i: (0, 0)),
                   pl.BlockSpec((WINDOW, KV_WIDTH), lambda i: (0, 0))],
        out_shape=[jax.ShapeDtypeStruct((SEQ, ATTN_WIDTH), BF16),
                   jax.ShapeDtypeStruct((WINDOW, KV_WIDTH), F32),
                   jax.ShapeDtypeStruct((WINDOW, KV_WIDTH), F32)],
        compiler_params=_cparams(("arbitrary",)),
        name="attn_prompt",
    )(sinks, proj, proj, proj, proj, proj, b_attn.reshape(1, ATTN_QKV_WIDTH),
      cos_t, sin_t, cos_t, sin_t)


ATTN_SB = 8


def _attn_sample_kernel(q_ref, k_ref, v_ref, bq_ref, bk_ref, bv_ref, cos_ref, sin_ref,
                        sink_ref, e_ref, et_ref, ck_ref, cv_ref, o_ref, ko_ref, vo_ref):
    cos, sin = cos_ref[...], sin_ref[...]
    scale = ATTN_HEAD_DIM ** -0.5
    qr = []
    for c in range(ATTN_WIDTH // LANES):
        sl = slice(c * LANES, (c + 1) * LANES)
        qr.append(_rope(q_ref[:, sl] + bq_ref[:, sl], cos, sin) * scale)
    q = jnp.concatenate(qr, axis=1)
    kn = jnp.concatenate(
        [_rope(k_ref[:, c * LANES:(c + 1) * LANES] + bk_ref[:, c * LANES:(c + 1) * LANES], cos, sin)
         for c in range(KV_WIDTH // LANES)], axis=1)
    vn = v_ref[...] + bv_ref[...]
    lb = WINDOW
    for b in range(ATTN_SB):
        ko_ref[b, 0:lb - 1, :] = ck_ref[b, 1:lb, :]
        ko_ref[b, lb - 1:lb, :] = kn[b:b + 1, :]
        vo_ref[b, 0:lb - 1, :] = cv_ref[b, 1:lb, :]
        vo_ref[b, lb - 1:lb, :] = vn[b:b + 1, :]
        kb = ko_ref[b]
        vb = vo_ref[b]
        prod = jnp.concatenate(
            [kb * q[b:b + 1, g * KV_WIDTH:(g + 1) * KV_WIDTH] for g in range(ATTN_GROUP)], axis=0)
        s = _dot(prod.astype(BF16), e_ref[...])
        ps = []
        for g in range(ATTN_GROUP):
            sg = s[g * lb:(g + 1) * lb, :]
            sink = sink_ref[g:g + 1, :]
            m = jnp.maximum(jnp.max(sg, axis=0, keepdims=True), sink)
            e = jnp.exp(sg - m)
            den = jnp.sum(e, axis=0, keepdims=True) + jnp.exp(sink - m)
            ps.append(e / den)
        pe = _dot(jnp.concatenate(ps, axis=0).astype(BF16), et_ref[...])
        for g in range(ATTN_GROUP):
            og = jnp.sum(pe[g * lb:(g + 1) * lb, :] * vb, axis=0, keepdims=True)
            o_ref[b:b + 1, g * KV_WIDTH:(g + 1) * KV_WIDTH] = og


def _attn_sample(q_perm, k_new, v_new, bq_perm, bk, bv, cos_row, sin_row, sinks_gk, cache_k, cache_v):
    n = DEC_BATCH
    sb = ATTN_SB
    head_of_lane = jnp.arange(KV_WIDTH) // ATTN_HEAD_DIM
    e_mat = (head_of_lane[:, None] == jnp.arange(LANES)[None, :]).astype(BF16)
    row = lambda w: pl.BlockSpec((sb, w), lambda i: (i, 0))
    vec = lambda w: pl.BlockSpec((1, w), lambda i: (0, 0))
    full = lambda a: pl.BlockSpec(a.shape, lambda i: (0,) * a.ndim)
    cache = pl.BlockSpec((sb, WINDOW, KV_WIDTH), lambda i: (i, 0, 0))
    return pl.pallas_call(
        _attn_sample_kernel,
        grid=(n // sb,),
        in_specs=[row(ATTN_WIDTH), row(KV_WIDTH), row(KV_WIDTH),
                  vec(ATTN_WIDTH), vec(KV_WIDTH), vec(KV_WIDTH), vec(LANES), vec(LANES),
                  full(sinks_gk), full(e_mat), full(e_mat.T), cache, cache],
        out_specs=[row(ATTN_WIDTH), cache, cache],
        out_shape=[jax.ShapeDtypeStruct((n, ATTN_WIDTH), F32),
                   jax.ShapeDtypeStruct(cache_k.shape, F32),
                   jax.ShapeDtypeStruct(cache_v.shape, F32)],
        compiler_params=_cparams(("parallel",)),
        name="attn_sample",
    )(q_perm, k_new, v_new, bq_perm, bk, bv, cos_row, sin_row, sinks_gk, e_mat, e_mat.T,
      cache_k, cache_v)


DN_ROWS = 512
DN_PAIR = 2 * DN_CHUNK
CONV_PAD = 8
DN_HPS = 4


def _dn_gates_kernel(ab_ref, alog_ref, dtb_ref, cum_ref, beta_ref):
    ab = ab_ref[...]
    g = -jnp.exp(alog_ref[...]) * _softplus(ab + dtb_ref[...])
    row = lax.broadcasted_iota(jnp.int32, ab.shape, 0) % DN_CHUNK
    shift = 1
    while shift < DN_CHUNK:
        g = g + jnp.where(row >= shift, pltpu.roll(g, shift, 0), 0.0)
        shift *= 2
    cum_ref[...] = g
    beta_ref[...] = pltpu.roll(_sigmoid(ab), LANES - DN_HEADS, 1)


def _dn_gates(proj_ab, alog_pad, dtb_pad):
    blk = pl.BlockSpec((DN_ROWS, LANES), lambda i: (i, 0))
    vec = pl.BlockSpec((1, LANES), lambda i: (0, 0))
    out = jax.ShapeDtypeStruct((SEQ, LANES), F32)
    return pl.pallas_call(
        _dn_gates_kernel,
        grid=(SEQ // DN_ROWS,),
        in_specs=[blk, vec, vec],
        out_specs=[blk, blk],
        out_shape=[out, out],
        compiler_params=_cparams(("parallel",)),
        name="dn_gates",
    )(proj_ab, alog_pad, dtb_pad)


def _dn_prompt_kernel(q_ref, qp_ref, k_ref, kp_ref, v_ref, vp_ref, z_ref, cum_ref, beta_ref,
                      wq_ref, wk_ref, wv_ref, nw_ref, o_ref, so_ref, xq_s, xk_s, xv_s, s_s):
    hp = pl.program_id(0)
    i = pl.program_id(1)
    pr = DN_PAIR
    cl = DN_CHUNK

    @pl.when(i == 0)
    def _():
        s_s[...] = jnp.zeros_like(s_s)

    for x_ref, prev_ref, pad_s in ((q_ref, qp_ref, xq_s), (k_ref, kp_ref, xk_s), (v_ref, vp_ref, xv_s)):
        pad_s[0:CONV_PAD, :] = jnp.where(i > 0, prev_ref[...], 0.0)
        pad_s[CONV_PAD:, :] = x_ref[...]

    def lanes_of(hh):
        return slice(hh * DN_DIM, (hh + 1) * DN_DIM)

    def conv(pad_s, w_ref, hh, r0):
        y = None
        for j in range(CONV_WIDTH):
            start = CONV_PAD + r0 - (CONV_WIDTH - 1 - j)
            term = w_ref[j:j + 1, lanes_of(hh)] * pad_s[start:start + pr, lanes_of(hh)]
            y = term if y is None else y + term
        return _silu(y)

    def l2n(x):
        return x * lax.rsqrt(jnp.sum(x * x, axis=-1, keepdims=True) + RMS_EPS)

    def gate(ref, hh, r0):
        pick = ci == hp * DN_HPS + hh
        col = jnp.sum(jnp.where(pick, ref[r0:r0 + pr, :], 0.0), axis=1, keepdims=True)
        return jnp.broadcast_to(col, (pr, pr))

    ri = lax.broadcasted_iota(jnp.int32, (pr, pr), 0)
    ci = lax.broadcasted_iota(jnp.int32, (pr, pr), 1)
    same_chunk = (ri // cl) == (ci // cl)
    causal = same_chunk & (ri >= ci)
    strict = same_chunk & (ri > ci)
    eye = (ri == ci).astype(F32)
    first_rows = ri < cl
    first_rows2 = lax.broadcasted_iota(jnp.int32, (pr, 2 * DN_DIM), 0) < cl
    nw = nw_ref[...]
    bf = lambda a: a.astype(BF16)
    cat0 = lambda *a: jnp.concatenate(a, axis=0)
    cat1 = lambda *a: jnp.concatenate(a, axis=1)

    n_pairs = DN_ROWS // pr
    probs = [(hh, a * pr) for a in range(n_pairs) for hh in range(DN_HPS)]

    q_p = [l2n(conv(xq_s, wq_ref, hh, r0)) * (DN_DIM ** -0.5) for hh, r0 in probs]
    k_p = [l2n(conv(xk_s, wk_ref, hh, r0)) for hh, r0 in probs]
    v_p = [conv(xv_s, wv_ref, hh, r0) for hh, r0 in probs]
    cum = [gate(cum_ref, hh, r0) for hh, r0 in probs]
    beta = [gate(beta_ref, hh, r0) for hh, r0 in probs]
    gamma = [jnp.exp(c) for c in cum]
    decay = [jnp.where(causal, jnp.exp(c - c.T), 0.0) for c in cum]
    kq = [_dot_nt(bf(cat0(k, q)), bf(k)) for k, q in zip(k_p, q_p)]
    x = [jnp.where(strict, -(b * m[:pr] * d), 0.0) for b, m, d in zip(beta, kq, decay)]
    att = [bf(m[pr:] * d) for m, d in zip(kq, decay)]
    t = [eye + m for m in x]
    xp = [_dot(bf(m), bf(m)) for m in x]
    for _ in range(int(math.log2(cl)) - 2):
        y = [_dot(bf(p), cat1(bf(p), bf(m))) for p, m in zip(xp, t)]
        xp = [m[:, :pr] for m in y]
        t = [m + n[:, pr:] for m, n in zip(t, y)]
    t = [m + _dot(bf(p), bf(m)) for p, m in zip(xp, t)]
    wu = [_dot(bf(m), bf(cat1(k * (b * g), v * b)))
          for m, k, v, b, g in zip(t, k_p, v_p, beta, gamma)]
    awu = [_dot(a, bf(m)) for a, m in zip(att, wu)]
    q_prime = [bf(q * g - m[:, :DN_DIM]) for q, g, m in zip(q_p, gamma, awu)]
    au = [m[:, DN_DIM:] for m in awu]
    k_dec_t = [bf((k * jnp.exp(jnp.where(first_rows, c[cl - 1:cl, :], c[pr - 1:pr, :]) - c)).T)
               for k, c in zip(k_p, cum)]
    kwu1 = [_dot(kt, bf(jnp.where(first_rows2, m, 0.0))) for kt, m in zip(k_dec_t, wu)]
    kwu2 = [_dot(kt, bf(jnp.where(first_rows2, 0.0, m))) for kt, m in zip(k_dec_t, wu)]
    g1 = [jnp.exp(c[cl - 1:cl, :]) for c in cum]
    g2 = [jnp.exp(c[pr - 1:pr, :]) for c in cum]
    comp = [_dot(bf(-m2[:, :DN_DIM]), bf(cat1(-m1[:, :DN_DIM], m1[:, DN_DIM:]))) for m1, m2 in zip(kwu1, kwu2)]
    mp12 = [-(b * m1[:, :DN_DIM]) - a * m2[:, :DN_DIM] + c[:, :DN_DIM]
            for a, b, m1, m2, c in zip(g1, g2, kwu1, kwu2, comp)]
    n12 = [b * m1[:, DN_DIM:] + c[:, DN_DIM:] + m2[:, DN_DIM:] for b, m1, m2, c in zip(g2, kwu1, kwu2, comp)]
    lhs = [cat0(qp[:cl], bf(-m1[:, :DN_DIM]), bf(m)) for qp, m1, m in zip(q_prime, kwu1, mp12)]

    s = [s_s[hh] for hh in range(DN_HPS)]
    for a in range(n_pairs):
        js = [a * DN_HPS + hh for hh in range(DN_HPS)]
        r = [_dot(lhs[j], bf(s[hh])) for hh, j in enumerate(js)]
        s_mid = [g1[j] * s[hh] + r[hh][cl:cl + DN_DIM] + kwu1[j][:, DN_DIM:] for hh, j in enumerate(js)]
        s_new = [(g1[j] * g2[j]) * s[hh] + r[hh][cl + DN_DIM:] + n12[j] for hh, j in enumerate(js)]
        o2 = [_dot(q_prime[j][cl:], bf(s_mid[hh])) for hh, j in enumerate(js)]
        for hh, j in enumerate(js):
            o = cat0(r[hh][:cl], o2[hh]) + au[j]
            o = o * lax.rsqrt(jnp.mean(o * o, axis=-1, keepdims=True) + RMS_EPS) * nw
            r0 = a * pr
            o_ref[r0:r0 + pr, lanes_of(hh)] = (o * _silu(z_ref[r0:r0 + pr, lanes_of(hh)])).astype(o_ref.dtype)
        s = s_new

    for hh in range(DN_HPS):
        s_s[hh] = s[hh]

    @pl.when(i == pl.num_programs(1) - 1)
    def _():
        for hh in range(DN_HPS):
            so_ref[hh] = s[hh]


def _dn_prompt(proj, cum, beta, w_conv, norm_w):
    rows = DN_ROWS
    width = DN_HPS * DN_DIM
    q0 = ATTN_QKV_WIDTH // width
    k0 = q0 + DN_WIDTH // width
    v0 = k0 + DN_WIDTH // width
    z0 = v0 + DN_WIDTH // width
    per = rows // CONV_PAD

    def blk(c0):
        return pl.BlockSpec((rows, width), lambda h, i: (i, c0 + h))

    def prev(c0):
        return pl.BlockSpec((CONV_PAD, width), lambda h, i: (jnp.maximum(i * per - 1, 0), c0 + h))

    def wblk(c0):
        return pl.BlockSpec((CONV_WIDTH, width), lambda h, i: (0, c0 + h))

    gate = pl.BlockSpec((rows, LANES), lambda h, i: (i, 0))
    vec = pl.BlockSpec((1, DN_DIM), lambda h, i: (0, 0))
    pad = pltpu.VMEM((rows + CONV_PAD, width), F32)
    return pl.pallas_call(
        _dn_prompt_kernel,
        grid=(DN_HEADS // DN_HPS, SEQ // rows),
        in_specs=[blk(q0), prev(q0), blk(k0), prev(k0), blk(v0), prev(v0), blk(z0), gate, gate,
                  wblk(0), wblk(DN_WIDTH // width), wblk(2 * DN_WIDTH // width), vec],
        out_specs=[pl.BlockSpec((rows, width), lambda h, i: (i, h)),
                   pl.BlockSpec((DN_HPS, DN_DIM, DN_DIM), lambda h, i: (h, 0, 0))],
        out_shape=[jax.ShapeDtypeStruct((SEQ, DN_WIDTH), BF16),
                   jax.ShapeDtypeStruct((DN_HEADS, DN_DIM, DN_DIM), F32)],
        scratch_shapes=[pad, pad, pad, pltpu.VMEM((DN_HPS, DN_DIM, DN_DIM), F32)],
        compiler_params=_cparams(("parallel", "arbitrary")),
        name="dn_prompt",
    )(proj, proj, proj, proj, proj, proj, proj, cum, beta, w_conv, w_conv, w_conv,
      norm_w.reshape(1, DN_DIM))


DN_SB = 8


def _dn_gates_of_head(ab, alog, dtb, h):
    lane = lax.broadcasted_iota(jnp.int32, ab.shape, 1)
    g_all = -jnp.exp(alog) * _softplus(ab + dtb)
    g = jnp.sum(jnp.where(lane == h, g_all, 0.0), axis=1, keepdims=True)
    beta = jnp.sum(jnp.where(lane == h + DN_HEADS, _sigmoid(ab), 0.0), axis=1, keepdims=True)
    return jnp.broadcast_to(g, ab.shape), jnp.broadcast_to(beta, ab.shape)


def _dn_sample_prep_kernel(x_ref, sc_ref, ab_ref, w_ref, alog_ref, dtb_ref,
                           qt_ref, kt_ref, vb_ref, bg_ref, gam_ref):
    ab = ab_ref[...]
    alog, dtb = alog_ref[...], dtb_ref[...]

    def conv(col):
        sl = slice(col, col + LANES)
        y = w_ref[CONV_WIDTH - 1:CONV_WIDTH, sl] * x_ref[:, sl]
        for j in range(CONV_WIDTH - 1):
            y = y + w_ref[j:j + 1, sl] * sc_ref[:, j * CONV_DIM + col:j * CONV_DIM + col + LANES]
        return _silu(y)

    def l2n(x):
        return x * lax.rsqrt(jnp.sum(x * x, axis=-1, keepdims=True) + RMS_EPS)

    for h in range(DN_HEADS):
        q = l2n(conv(h * DN_DIM)) * (DN_DIM ** -0.5)
        k = l2n(conv(DN_WIDTH + h * DN_DIM))
        v = conv(2 * DN_WIDTH + h * DN_DIM)
        g_b, beta_b = _dn_gates_of_head(ab, alog, dtb, h)
        gamma = jnp.exp(g_b)
        qt_ref[h] = q.T
        kt_ref[h] = k.T
        vb_ref[h] = v * beta_b
        bg_ref[h] = beta_b * gamma
        gam_ref[h] = gamma


def _dn_sample_prep(qkv_s, conv_state, ab_s, w_conv, alog_pad, dtb_pad):
    n = DEC_BATCH
    full = lambda a: pl.BlockSpec(a.shape, lambda i: (0,) * a.ndim)
    hm = jax.ShapeDtypeStruct((DN_HEADS, n, DN_DIM), F32)
    hm_spec = pl.BlockSpec((DN_HEADS, n, DN_DIM), lambda i: (0, 0, 0))
    args = (qkv_s, conv_state, ab_s, w_conv, alog_pad, dtb_pad)
    return pl.pallas_call(
        _dn_sample_prep_kernel,
        grid=(1,),
        in_specs=[full(a) for a in args],
        out_specs=[hm_spec] * 5,
        out_shape=[hm] * 5,
        compiler_params=_cparams(("arbitrary",)),
        name="dn_sample_prep",
    )(*args)


def _dn_sample_kernel(s_ref, qt_ref, kt_ref, vb_ref, bg_ref, gam_ref, z_ref, nw_ref,
                      so_ref, o_ref, o_s):
    step = pl.program_id(0)
    lane = lax.broadcasted_iota(jnp.int32, (DN_DIM, DEC_BATCH), 1)

    def head(h, carry):
        kt = kt_ref[h]
        qt = qt_ref[h]
        for bb in range(DN_SB):
            b = step * DN_SB + bb
            pick = lane == b
            kcol = jnp.sum(jnp.where(pick, kt, 0.0), axis=1, keepdims=True)
            qcol = jnp.sum(jnp.where(pick, qt, 0.0), axis=1, keepdims=True)
            s = s_ref[bb, h]
            ks = jnp.sum(kcol * s, axis=0, keepdims=True)
            u = vb_ref[h, pl.ds(b, 1), :] - bg_ref[h, pl.ds(b, 1), :] * ks
            s_new = gam_ref[h, pl.ds(b, 1), :] * s + kcol * u
            so_ref[bb, h] = s_new
            o_s[h, pl.ds(bb, 1), :] = jnp.sum(qcol * s_new, axis=0, keepdims=True)
        return carry

    lax.fori_loop(0, DN_HEADS, head, 0)
    nw = nw_ref[...]
    for h in range(DN_HEADS):
        o = o_s[h]
        o = o * lax.rsqrt(jnp.mean(o * o, axis=-1, keepdims=True) + RMS_EPS) * nw
        sl = slice(h * DN_DIM, (h + 1) * DN_DIM)
        o_ref[:, sl] = o * _silu(z_ref[:, sl])


def _dn_sample(state, qt, kt, vb, bg, gam, z_s, norm_w):
    n = DEC_BATCH
    sb = DN_SB
    sblk = pl.BlockSpec((sb, DN_HEADS, DN_DIM, DN_DIM), lambda i: (i, 0, 0, 0))
    hm = pl.BlockSpec((DN_HEADS, n, DN_DIM), lambda i: (0, 0, 0))
    row = pl.BlockSpec((sb, DN_WIDTH), lambda i: (i, 0))
    return pl.pallas_call(
        _dn_sample_kernel,
        grid=(n // sb,),
        in_specs=[sblk, hm, hm, hm, hm, hm, row, pl.BlockSpec((1, DN_DIM), lambda i: (0, 0))],
        out_specs=[sblk, row],
        out_shape=[jax.ShapeDtypeStruct(state.shape, F32), jax.ShapeDtypeStruct((n, DN_WIDTH), F32)],
        scratch_shapes=[pltpu.VMEM((DN_HEADS, sb, DN_DIM), F32)],
        compiler_params=_cparams(("parallel",)),
        name="dn_sample",
    )(state, qt, kt, vb, bg, gam, z_s, norm_w.reshape(1, DN_DIM))


BM = 1040
BN_IN = 512
BM_OUT = 1024
BN_OUT = 512
BF_FFN = 256


def _rope_tables(pos):
    half = ATTN_HEAD_DIM // 2
    inv_freq = ROPE_THETA ** (-jnp.arange(half, dtype=F32) / half)
    ang = pos.astype(F32)[:, None] * inv_freq
    cos, sin = jnp.cos(ang), jnp.sin(ang)
    reps = LANES // ATTN_HEAD_DIM
    return jnp.tile(cos, (1, 2 * reps)), jnp.tile(jnp.concatenate([-sin, sin], axis=1), (1, reps))


def _group_major(a):
    lead = a.shape[:-1]
    a = a.reshape(lead + (ATTN_KV_HEADS, ATTN_GROUP, ATTN_HEAD_DIM))
    return jnp.swapaxes(a, -3, -2).reshape(lead + (ATTN_WIDTH,))


def _layer(x_prompt, x_sample, cache_k, cache_v, state_conv, state_delta, w_in, b_attn, attn_sinks,
           w_conv, dn_a_log, dn_dt_bias, dn_norm_w, w_out, ln1_g, ln1_b, w_gate, w_up, w_down,
           ln2_g, ln2_b):
    n_s = DEC_BATCH
    xb = _xcast(x_prompt, x_sample)
    w_in_t = w_in.T
    w_ab = jnp.pad(w_in_t[MAIN_WIDTH:], ((0, LANES - 2 * DN_HEADS), (0, 0)))
    proj = _matmul_nt(xb, w_in_t, BM, BN_IN, MAIN_WIDTH // BN_IN)
    proj_ab = _matmul_nt(xb, w_ab, BM, LANES, 1)

    pad16 = lambda v: jnp.pad(v, (0, LANES - DN_HEADS)).reshape(1, LANES)
    alog_pad, dtb_pad = pad16(dn_a_log), pad16(dn_dt_bias)

    cos_t, sin_t = _rope_tables(jnp.arange(SEQ, dtype=jnp.int32))
    attn_p, pk, pv = _attn_prompt(proj, b_attn, attn_sinks, cos_t, sin_t)
    cum, beta = _dn_gates(proj_ab, alog_pad, dtb_pad)
    dn_p, ps = _dn_prompt(proj, cum, beta, w_conv, dn_norm_w)
    pc = proj[SEQ - (CONV_WIDTH - 1):SEQ, ATTN_QKV_WIDTH:ATTN_QKV_WIDTH + CONV_DIM]

    proj_s = proj[SEQ:]
    cos_r, sin_r = _rope_tables(jnp.full((1,), PAST_LEN, jnp.int32))
    sinks_gk = jnp.pad(attn_sinks.reshape(ATTN_KV_HEADS, ATTN_GROUP).T,
                       ((0, 0), (0, LANES - ATTN_KV_HEADS)))
    attn_s_perm, sk, sv = _attn_sample(
        _group_major(proj_s[:, :ATTN_WIDTH]),
        proj_s[:, ATTN_WIDTH:ATTN_WIDTH + KV_WIDTH],
        proj_s[:, ATTN_WIDTH + KV_WIDTH:ATTN_QKV_WIDTH],
        _group_major(b_attn[:ATTN_WIDTH]).reshape(1, ATTN_WIDTH),
        b_attn[ATTN_WIDTH:ATTN_WIDTH + KV_WIDTH].reshape(1, KV_WIDTH),
        b_attn[ATTN_WIDTH + KV_WIDTH:].reshape(1, KV_WIDTH),
        cos_r, sin_r, sinks_gk, cache_k.reshape(n_s, WINDOW, KV_WIDTH),
        cache_v.reshape(n_s, WINDOW, KV_WIDTH))
    attn_s = jnp.swapaxes(attn_s_perm.reshape(n_s, ATTN_GROUP, ATTN_KV_HEADS, ATTN_HEAD_DIM), 1, 2)
    attn_s = attn_s.reshape(n_s, ATTN_WIDTH)

    qkv_s = proj_s[:, ATTN_QKV_WIDTH:ATTN_QKV_WIDTH + CONV_DIM]
    z_s = proj_s[:, ATTN_QKV_WIDTH + CONV_DIM:]
    qt, kt, vb, bg, gam = _dn_sample_prep(
        qkv_s, state_conv.reshape(n_s, (CONV_WIDTH - 1) * CONV_DIM), proj_ab[SEQ:],
        w_conv, alog_pad, dtb_pad)
    ss, dn_s = _dn_sample(state_delta, qt, kt, vb, bg, gam, z_s, dn_norm_w)
    sc = jnp.concatenate([state_conv[:, 1:], qkv_s[:, None, :]], axis=1)

    mixed_p, mixed_s = _out_proj(attn_p, dn_p, attn_s.astype(BF16), dn_s.astype(BF16), w_out,
                                 BM_OUT, BN_OUT)
    h32, hb = _ln1(x_prompt, x_sample, mixed_p, mixed_s, ln1_g, ln1_b)
    ffn = _ffn(hb, w_gate, w_up, w_down, BM, BF_FFN)
    y_p, y_s = _ln2(h32, ffn, ln2_g, ln2_b)
    return (y_p, y_s, pk.reshape(WINDOW, ATTN_KV_HEADS, ATTN_HEAD_DIM),
            pv.reshape(WINDOW, ATTN_KV_HEADS, ATTN_HEAD_DIM), pc, ps,
            sk.reshape(n_s, WINDOW, ATTN_KV_HEADS, ATTN_HEAD_DIM),
            sv.reshape(n_s, WINDOW, ATTN_KV_HEADS, ATTN_HEAD_DIM), sc, ss)


def kernel(x_prompt, x_sample, cache_swa_k, cache_swa_v, state_conv, state_delta, w_in, b_attn,
           attn_sinks, w_conv, dn_a_log, dn_dt_bias, dn_norm_w, w_out, ln1_g, ln1_b, w_gate, w_up,
           w_down, ln2_g, ln2_b):
    assert x_prompt.shape == (1, SEQ, D_MODEL) and x_sample.shape == (DEC_BATCH, 1, D_MODEL)
    assert w_in.shape[0] == 1, "one layer"
    y_p, y_s, pk, pv, pc, ps, sk, sv, sc, ss = _layer(
        x_prompt[0], x_sample[:, 0], cache_swa_k[0], cache_swa_v[0], state_conv[0], state_delta[0],
        w_in[0], b_attn[0], attn_sinks[0], w_conv[0], dn_a_log[0], dn_dt_bias[0], dn_norm_w[0],
        w_out[0], ln1_g[0], ln1_b[0], w_gate[0], w_up[0], w_down[0], ln2_g[0], ln2_b[0])
    return (y_p[None], y_s[:, None], pk[None, None], pv[None, None], pc[None, None],
            ps[None, None], sk[None], sv[None], sc[None], ss[None])
```

```python
import math

import jax
import jax.numpy as jnp
from jax import lax
from jax.experimental import pallas as pl
from jax.experimental.pallas import tpu as pltpu

D_MODEL = 4096
SEQ = 8192
DEC_BATCH = 128
PAST_LEN = 8192
ROWS = SEQ + DEC_BATCH

ATTN_HEADS = 32
ATTN_KV_HEADS = 8
ATTN_HEAD_DIM = 64
ATTN_GROUP = ATTN_HEADS // ATTN_KV_HEADS
ATTN_WIDTH = ATTN_HEADS * ATTN_HEAD_DIM
KV_WIDTH = ATTN_KV_HEADS * ATTN_HEAD_DIM
WINDOW = 128
ROPE_THETA = 10000.0
DN_HEADS = 16
DN_DIM = 128
DN_WIDTH = DN_HEADS * DN_DIM
CONV_WIDTH = 4
CONV_DIM = 3 * DN_WIDTH
DN_CHUNK = 64
ATTN_QKV_WIDTH = ATTN_WIDTH + 2 * KV_WIDTH
MAIN_WIDTH = ATTN_QKV_WIDTH + CONV_DIM + DN_WIDTH
FFN_HIDDEN = 11008
DEEPNORM_ALPHA = 2.0 ** 0.25
LN_EPS = 1e-5
RMS_EPS = 1e-6

LANES = 128
VMEM_LIMIT = 56 * 1024 * 1024
VMEM_LIMIT_FFN = 61 * 1024 * 1024

F32 = jnp.float32
BF16 = jnp.bfloat16


def _cparams(sem, vmem=VMEM_LIMIT):
    return pltpu.CompilerParams(dimension_semantics=sem, vmem_limit_bytes=vmem)


def _dot(a, b):
    return jnp.dot(a, b, preferred_element_type=F32)


def _dot_nt(a, b):
    return lax.dot_general(a, b, (((1,), (1,)), ((), ())), preferred_element_type=F32)


def _sigmoid(x):
    return 0.5 * jnp.tanh(0.5 * x) + 0.5


def _silu(x):
    return x * _sigmoid(x)


def _softplus(x):
    return jnp.maximum(x, 0.0) + jnp.log(1.0 + jnp.exp(-jnp.abs(x)))


def _rope(x, cos, sin_signed):
    lane = lax.broadcasted_iota(jnp.int32, x.shape, 1)
    first_half = (lane % ATTN_HEAD_DIM) < (ATTN_HEAD_DIM // 2)
    partner = jnp.where(first_half, pltpu.roll(x, LANES - ATTN_HEAD_DIM // 2, 1),
                        pltpu.roll(x, ATTN_HEAD_DIM // 2, 1))
    return x * cos + partner * sin_signed


def _mm_nt_kernel(x_ref, wt_ref, o_ref):
    o_ref[...] = _dot_nt(x_ref[...], wt_ref[...].astype(BF16)).astype(o_ref.dtype)


def _matmul_nt(x, wt, bm, bn, nb, out_dtype=F32):
    m, k = x.shape
    return pl.pallas_call(
        _mm_nt_kernel,
        grid=(m // bm, nb),
        in_specs=[pl.BlockSpec((bm, k), lambda i, j: (i, 0)),
                  pl.BlockSpec((bn, k), lambda i, j: (j, 0))],
        out_specs=pl.BlockSpec((bm, bn), lambda i, j: (i, j)),
        out_shape=jax.ShapeDtypeStruct((m, nb * bn), out_dtype),
        compiler_params=_cparams(("parallel", "parallel")),
        name="matmul",
    )(x, wt)


def _out_proj_kernel(ap_ref, dp_ref, as_ref, ds_ref, w_ref, op_ref, os_ref, wb_s):
    i = pl.program_id(1)
    last = pl.num_programs(1) - 1
    ka = ap_ref.shape[1]

    @pl.when(i == 0)
    def _():
        wb_s[...] = w_ref[...].astype(BF16)

    @pl.when(i < last)
    def _():
        op_ref[...] = _dot(ap_ref[...], wb_s[:ka, :]) + _dot(dp_ref[...], wb_s[ka:, :])

    @pl.when(i == last)
    def _():
        os_ref[...] = _dot(as_ref[...], wb_s[:ka, :]) + _dot(ds_ref[...], wb_s[ka:, :])


def _out_proj(a_p, d_p, a_s, d_s, w, bm, bn):
    m, ka = a_p.shape
    kd = d_p.shape[1]
    ms = a_s.shape[0]
    n = w.shape[1]
    nt = m // bm
    tile = lambda width: pl.BlockSpec((bm, width), lambda j, i: (jnp.minimum(i, nt - 1), 0))
    dec = lambda width: pl.BlockSpec((ms, width), lambda j, i: (0, 0))
    return pl.pallas_call(
        _out_proj_kernel,
        grid=(n // bn, nt + 1),
        in_specs=[tile(ka), tile(kd), dec(ka), dec(kd),
                  pl.BlockSpec((ka + kd, bn), lambda j, i: (0, j))],
        out_specs=[pl.BlockSpec((bm, bn), lambda j, i: (jnp.minimum(i, nt - 1), j)),
                   pl.BlockSpec((ms, bn), lambda j, i: (0, j))],
        out_shape=[jax.ShapeDtypeStruct((m, n), F32), jax.ShapeDtypeStruct((ms, n), F32)],
        scratch_shapes=[pltpu.VMEM((ka + kd, bn), BF16)],
        compiler_params=_cparams(("parallel", "arbitrary")),
        name="out_proj",
    )(a_p, d_p, a_s, d_s, w)


FFN_NCHUNK = 1024
FFN_ROW_ALIGN = 16


def _ffn_kernel(h_ref, wg_ref, wu_ref, wd_ref, o_ref):
    f = pl.program_id(1)

    @pl.when(f == 0)
    def _():
        o_ref[...] = jnp.zeros_like(o_ref)

    bm = h_ref.shape[0]
    cut = (bm // 2) // FFN_ROW_ALIGN * FFN_ROW_ALIGN
    groups = ((0, cut), (cut, bm))
    wg = wg_ref[...].astype(BF16)
    wu = wu_ref[...].astype(BF16)
    gates = [_dot(h_ref[a:b, :], wg) for a, b in groups]
    ups = [_dot(h_ref[a:b, :], wu) for a, b in groups]
    acts = [(_silu(g) * u).astype(BF16) for g, u in zip(gates, ups)]
    for n in range(0, o_ref.shape[1], FFN_NCHUNK):
        wd = wd_ref[:, n:n + FFN_NCHUNK].astype(BF16)
        for (a, b), act in zip(groups, acts):
            o_ref[a:b, n:n + FFN_NCHUNK] += _dot(act, wd)


def _ffn(h, w_gate, w_up, w_down, bm, bf):
    m, d = h.shape
    hidden = w_gate.shape[1]
    once = pl.Buffered(1)
    return pl.pallas_call(
        _ffn_kernel,
        grid=(m // bm, hidden // bf),
        in_specs=[pl.BlockSpec((bm, d), lambda i, f: (i, 0), pipeline_mode=once),
                  pl.BlockSpec((d, bf), lambda i, f: (0, f)),
                  pl.BlockSpec((d, bf), lambda i, f: (0, f)),
                  pl.BlockSpec((bf, d), lambda i, f: (f, 0))],
        out_specs=pl.BlockSpec((bm, d), lambda i, f: (i, 0), pipeline_mode=once),
        out_shape=jax.ShapeDtypeStruct((m, d), F32),
        compiler_params=_cparams(("parallel", "arbitrary"), VMEM_LIMIT_FFN),
        name="ffn",
    )(h, w_gate, w_up, w_down)


BR = WINDOW
NB_PROMPT = SEQ // BR


def _deepnorm(x, mixed, g, b):
    v = DEEPNORM_ALPHA * x + mixed
    mu = jnp.mean(v, axis=-1, keepdims=True)
    c = v - mu
    var = jnp.mean(c * c, axis=-1, keepdims=True)
    return c * lax.rsqrt(var + LN_EPS) * g + b


def _prompt_rows(width):
    return pl.BlockSpec((BR, width), lambda i: (jnp.minimum(i, NB_PROMPT - 1), 0))


def _decode_rows(width):
    return pl.BlockSpec((BR, width), lambda i: (0, 0))


def _xcast_kernel(xp_ref, xs_ref, o_ref):
    i = pl.program_id(0)
    o_ref[...] = jnp.where(i < NB_PROMPT, xp_ref[...], xs_ref[...]).astype(o_ref.dtype)


def _xcast(x_prompt, x_sample):
    d = x_prompt.shape[1]
    return pl.pallas_call(
        _xcast_kernel,
        grid=(NB_PROMPT + 1,),
        in_specs=[_prompt_rows(d), _decode_rows(d)],
        out_specs=pl.BlockSpec((BR, d), lambda i: (i, 0)),
        out_shape=jax.ShapeDtypeStruct((ROWS, d), BF16),
        compiler_params=_cparams(("arbitrary",)),
        name="xcast",
    )(x_prompt, x_sample)


def _ln1_kernel(xp_ref, xs_ref, mp_ref, ms_ref, g_ref, b_ref, o_ref, ob_ref):
    i = pl.program_id(0)
    is_prompt = i < NB_PROMPT
    x = jnp.where(is_prompt, xp_ref[...], xs_ref[...])
    mixed = jnp.where(is_prompt, mp_ref[...], ms_ref[...])
    y = _deepnorm(x, mixed, g_ref[...], b_ref[...])
    o_ref[...] = y
    ob_ref[...] = y.astype(BF16)


def _ln1(x_prompt, x_sample, mixed_p, mixed_s, g, b):
    d = mixed_p.shape[1]
    row = pl.BlockSpec((BR, d), lambda i: (i, 0))
    vec = pl.BlockSpec((1, d), lambda i: (0, 0))
    return pl.pallas_call(
        _ln1_kernel,
        grid=(NB_PROMPT + 1,),
        in_specs=[_prompt_rows(d), _decode_rows(d), _prompt_rows(d), _decode_rows(d), vec, vec],
        out_specs=[row, row],
        out_shape=[jax.ShapeDtypeStruct((ROWS, d), F32), jax.ShapeDtypeStruct((ROWS, d), BF16)],
        compiler_params=_cparams(("arbitrary",)),
        name="ln1",
    )(x_prompt, x_sample, mixed_p, mixed_s, g.reshape(1, d), b.reshape(1, d))


def _ln2_kernel(h_ref, f_ref, g_ref, b_ref, yp_ref, ys_ref):
    i = pl.program_id(0)
    y = _deepnorm(h_ref[...], f_ref[...], g_ref[...], b_ref[...])

    @pl.when(i < NB_PROMPT)
    def _():
        yp_ref[...] = y

    @pl.when(i >= NB_PROMPT)
    def _():
        ys_ref[...] = y


def _ln2(h32, ffn, g, b):
    d = h32.shape[1]
    row = pl.BlockSpec((BR, d), lambda i: (i, 0))
    vec = pl.BlockSpec((1, d), lambda i: (0, 0))
    return pl.pallas_call(
        _ln2_kernel,
        grid=(NB_PROMPT + 1,),
        in_specs=[row, row, vec, vec],
        out_specs=[_prompt_rows(d), _decode_rows(d)],
        out_shape=[jax.ShapeDtypeStruct((SEQ, d), F32), jax.ShapeDtypeStruct((DEC_BATCH, d), F32)],
        compiler_params=_cparams(("arbitrary",)),
        name="ln2",
    )(h32, ffn, g.reshape(1, d), b.reshape(1, d))


def _attn_prompt_kernel(sink_ref, q_ref, kc_ref, kp_ref, vc_ref, vp_ref, b_ref,
                        cc_ref, sc_ref, cp_ref, sp_ref, o_ref, ko_ref, vo_ref):
    i = pl.program_id(0)
    w = WINDOW
    cos_c, sin_c = cc_ref[...], sc_ref[...]
    cos_p, sin_p = cp_ref[...], sp_ref[...]
    n_kchunk = KV_WIDTH // LANES

    k_cur, k_prev, v_cur, v_prev = [], [], [], []
    for c in range(n_kchunk):
        sl = slice(c * LANES, (c + 1) * LANES)
        bk = b_ref[:, ATTN_WIDTH + c * LANES:ATTN_WIDTH + (c + 1) * LANES]
        bv = b_ref[:, ATTN_WIDTH + KV_WIDTH + c * LANES:ATTN_WIDTH + KV_WIDTH + (c + 1) * LANES]
        kc = _rope(kc_ref[:, sl] + bk, cos_c, sin_c)
        kp = _rope(kp_ref[:, sl] + bk, cos_p, sin_p)
        vc = vc_ref[:, sl] + bv
        vp = vp_ref[:, sl] + bv
        ko_ref[:, sl] = kc
        vo_ref[:, sl] = vc
        k_cur.append(kc)
        k_prev.append(kp)
        v_cur.append(vc)
        v_prev.append(vp)

    rows = ATTN_GROUP * w
    r = lax.broadcasted_iota(jnp.int32, (rows, 2 * w), 0) % w
    col = lax.broadcasted_iota(jnp.int32, (rows, 2 * w), 1)
    valid = (col > r) & (col <= r + w) & ((col >= w) | (i > 0))
    row_id = lax.broadcasted_iota(jnp.int32, (rows, 1), 0)
    lane_k = lax.broadcasted_iota(jnp.int32, (2 * w, LANES), 1)
    lane_o = lax.broadcasted_iota(jnp.int32, (w, LANES), 1)
    scale = ATTN_HEAD_DIM ** -0.5

    k2s, v2s, q4s = [], [], []
    for hk in range(ATTN_KV_HEADS):
        kchunk, khalf = hk // 2, hk % 2
        k2 = jnp.concatenate([k_prev[kchunk], k_cur[kchunk]], axis=0)
        in_half = (lane_k // ATTN_HEAD_DIM) == khalf
        k2s.append(jnp.where(in_half, k2, 0.0).astype(BF16))
        v2s.append(jnp.concatenate([v_prev[kchunk], v_cur[kchunk]], axis=0).astype(BF16))
        qs = []
        for g in range(ATTN_GROUP):
            hq = hk * ATTN_GROUP + g
            qchunk, qhalf = hq // 2, hq % 2
            sl = slice(qchunk * LANES, (qchunk + 1) * LANES)
            qc = _rope(q_ref[:, sl] + b_ref[:, sl], cos_c, sin_c) * scale
            if qhalf != khalf:
                qc = pltpu.roll(qc, ATTN_HEAD_DIM, 1)
            qs.append(qc)
        q4s.append(jnp.concatenate(qs, axis=0).astype(BF16))
    scores = [_dot_nt(q4, k2) for q4, k2 in zip(q4s, k2s)]
    probs = []
    for hk, s in enumerate(scores):
        s = jnp.where(valid, s, -jnp.inf)
        sink = jnp.zeros((rows, 1), F32)
        for g in range(ATTN_GROUP):
            sink = jnp.where(row_id // w == g, sink_ref[hk * ATTN_GROUP + g], sink)
        m = jnp.maximum(jnp.max(s, axis=-1, keepdims=True), sink)
        e = jnp.exp(s - m)
        den = jnp.sum(e, axis=-1, keepdims=True) + jnp.exp(sink - m)
        probs.append((e / den).astype(BF16))
    pvs = [_dot(p, v2) for p, v2 in zip(probs, v2s)]
    for hk, pv in enumerate(pvs):
        khalf = hk % 2
        outs = []
        for g in range(ATTN_GROUP):
            og = pv[g * w:(g + 1) * w, :]
            if (g % 2) != khalf:
                og = pltpu.roll(og, ATTN_HEAD_DIM, 1)
            outs.append(og)
        for j in range(ATTN_GROUP // 2):
            oc = jnp.where(lane_o < ATTN_HEAD_DIM, outs[2 * j], outs[2 * j + 1])
            c = hk * (ATTN_GROUP // 2) + j
            o_ref[:, c * LANES:(c + 1) * LANES] = oc.astype(o_ref.dtype)


def _attn_prompt(proj, b_attn, sinks, cos_t, sin_t):
    nb = SEQ // WINDOW
    kcol = ATTN_WIDTH // KV_WIDTH
    prev = lambda i: jnp.maximum(i - 1, 0)
    return pl.pallas_call(
        _attn_prompt_kernel,
        grid=(nb,),
        in_specs=[pl.BlockSpec(memory_space=pltpu.SMEM),
                  pl.BlockSpec((WINDOW, ATTN_WIDTH), lambda i: (i, 0)),
                  pl.BlockSpec((WINDOW, KV_WIDTH), lambda i: (i, kcol)),
                  pl.BlockSpec((WINDOW, KV_WIDTH), lambda i: (prev(i), kcol)),
                  pl.BlockSpec((WINDOW, KV_WIDTH), lambda i: (i, kcol + 1)),
                  pl.BlockSpec((WINDOW, KV_WIDTH), lambda i: (prev(i), kcol + 1)),
                  pl.BlockSpec((1, ATTN_QKV_WIDTH), lambda i: (0, 0)),
                  pl.BlockSpec((WINDOW, LANES), lambda i: (i, 0)),
                  pl.BlockSpec((WINDOW, LANES), lambda i: (i, 0)),
                  pl.BlockSpec((WINDOW, LANES), lambda i: (prev(i), 0)),
                  pl.BlockSpec((WINDOW, LANES), lambda i: (prev(i), 0))],
        out_specs=[pl.BlockSpec((WINDOW, ATTN_WIDTH), lambda i: (i, 0)),
                   pl.BlockSpec((WINDOW, KV_WIDTH), lambda i: (0, 0)),
                   pl.BlockSpec((WINDOW, KV_WIDTH), lambda i: (0, 0))],
        out_shape=[jax.ShapeDtypeStruct((SEQ, ATTN_WIDTH), BF16),
                   jax.ShapeDtypeStruct((WINDOW, KV_WIDTH), F32),
                   jax.ShapeDtypeStruct((WINDOW, KV_WIDTH), F32)],
        compiler_params=_cparams(("arbitrary",)),
        name="attn_prompt",
    )(sinks, proj, proj, proj, proj, proj, b_attn.reshape(1, ATTN_QKV_WIDTH),
      cos_t, sin_t, cos_t, sin_t)


ATTN_SB = 8


def _attn_sample_kernel(q_ref, k_ref, v_ref, bq_ref, bk_ref, bv_ref, cos_ref, sin_ref,
                        sink_ref, e_ref, et_ref, ck_ref, cv_ref, o_ref, ko_ref, vo_ref):
    cos, sin = cos_ref[...], sin_ref[...]
    scale = ATTN_HEAD_DIM ** -0.5
    qr = []
    for c in range(ATTN_WIDTH // LANES):
        sl = slice(c * LANES, (c + 1) * LANES)
        qr.append(_rope(q_ref[:, sl] + bq_ref[:, sl], cos, sin) * scale)
    q = jnp.concatenate(qr, axis=1)
    kn = jnp.concatenate(
        [_rope(k_ref[:, c * LANES:(c + 1) * LANES] + bk_ref[:, c * LANES:(c + 1) * LANES], cos, sin)
         for c in range(KV_WIDTH // LANES)], axis=1)
    vn = v_ref[...] + bv_ref[...]
    lb = WINDOW
    for b in range(ATTN_SB):
        ko_ref[b, 0:lb - 1, :] = ck_ref[b, 1:lb, :]
        ko_ref[b, lb - 1:lb, :] = kn[b:b + 1, :]
        vo_ref[b, 0:lb - 1, :] = cv_ref[b, 1:lb, :]
        vo_ref[b, lb - 1:lb, :] = vn[b:b + 1, :]
        kb = ko_ref[b]
        vb = vo_ref[b]
        prod = jnp.concatenate(
            [kb * q[b:b + 1, g * KV_WIDTH:(g + 1) * KV_WIDTH] for g in range(ATTN_GROUP)], axis=0)
        s = _dot(prod.astype(BF16), e_ref[...])
        ps = []
        for g in range(ATTN_GROUP):
            sg = s[g * lb:(g + 1) * lb, :]
            sink = sink_ref[g:g + 1, :]
            m = jnp.maximum(jnp.max(sg, axis=0, keepdims=True), sink)
            e = jnp.exp(sg - m)
            den = jnp.sum(e, axis=0, keepdims=True) + jnp.exp(sink - m)
            ps.append(e / den)
        pe = _dot(jnp.concatenate(ps, axis=0).astype(BF16), et_ref[...])
        for g in range(ATTN_GROUP):
            og = jnp.sum(pe[g * lb:(g + 1) * lb, :] * vb, axis=0, keepdims=True)
            o_ref[b:b + 1, g * KV_WIDTH:(g + 1) * KV_WIDTH] = og


def _attn_sample(q_perm, k_new, v_new, bq_perm, bk, bv, cos_row, sin_row, sinks_gk, cache_k, cache_v):
    n = DEC_BATCH
    sb = ATTN_SB
    head_of_lane = jnp.arange(KV_WIDTH) // ATTN_HEAD_DIM
    e_mat = (head_of_lane[:, None] == jnp.arange(LANES)[None, :]).astype(BF16)
    row = lambda w: pl.BlockSpec((sb, w), lambda i: (i, 0))
    vec = lambda w: pl.BlockSpec((1, w), lambda i: (0, 0))
    full = lambda a: pl.BlockSpec(a.shape, lambda i: (0,) * a.ndim)
    cache = pl.BlockSpec((sb, WINDOW, KV_WIDTH), lambda i: (i, 0, 0))
    return pl.pallas_call(
        _attn_sample_kernel,
        grid=(n // sb,),
        in_specs=[row(ATTN_WIDTH), row(KV_WIDTH), row(KV_WIDTH),
                  vec(ATTN_WIDTH), vec(KV_WIDTH), vec(KV_WIDTH), vec(LANES), vec(LANES),
                  full(sinks_gk), full(e_mat), full(e_mat.T), cache, cache],
        out_specs=[row(ATTN_WIDTH), cache, cache],
        out_shape=[jax.ShapeDtypeStruct((n, ATTN_WIDTH), F32),
                   jax.ShapeDtypeStruct(cache_k.shape, F32),
                   jax.ShapeDtypeStruct(cache_v.shape, F32)],
        compiler_params=_cparams(("parallel",)),
        name="attn_sample",
    )(q_perm, k_new, v_new, bq_perm, bk, bv, cos_row, sin_row, sinks_gk, e_mat, e_mat.T,
      cache_k, cache_v)


DN_ROWS = 512
DN_PAIR = 2 * DN_CHUNK
CONV_PAD = 8
DN_HPS = 4


def _dn_gates_kernel(ab_ref, alog_ref, dtb_ref, cum_ref, beta_ref):
    ab = ab_ref[...]
    g = -jnp.exp(alog_ref[...]) * _softplus(ab + dtb_ref[...])
    row = lax.broadcasted_iota(jnp.int32, ab.shape, 0) % DN_CHUNK
    shift = 1
    while shift < DN_CHUNK:
        g = g + jnp.where(row >= shift, pltpu.roll(g, shift, 0), 0.0)
        shift *= 2
    cum_ref[...] = g
    beta_ref[...] = pltpu.roll(_sigmoid(ab), LANES - DN_HEADS, 1)


def _dn_gates(proj_ab, alog_pad, dtb_pad):
    blk = pl.BlockSpec((DN_ROWS, LANES), lambda i: (i, 0))
    vec = pl.BlockSpec((1, LANES), lambda i: (0, 0))
    out = jax.ShapeDtypeStruct((SEQ, LANES), F32)
    return pl.pallas_call(
        _dn_gates_kernel,
        grid=(SEQ // DN_ROWS,),
        in_specs=[blk, vec, vec],
        out_specs=[blk, blk],
        out_shape=[out, out],
        compiler_params=_cparams(("parallel",)),
        name="dn_gates",
    )(proj_ab, alog_pad, dtb_pad)


def _dn_prompt_kernel(q_ref, qp_ref, k_ref, kp_ref, v_ref, vp_ref, z_ref, cum_ref, beta_ref,
                      wq_ref, wk_ref, wv_ref, nw_ref, o_ref, so_ref, xq_s, xk_s, xv_s, s_s):
    hp = pl.program_id(0)
    i = pl.program_id(1)
    pr = DN_PAIR
    cl = DN_CHUNK

    @pl.when(i == 0)
    def _():
        s_s[...] = jnp.zeros_like(s_s)

    for x_ref, prev_ref, pad_s in ((q_ref, qp_ref, xq_s), (k_ref, kp_ref, xk_s), (v_ref, vp_ref, xv_s)):
        pad_s[0:CONV_PAD, :] = jnp.where(i > 0, prev_ref[...], 0.0)
        pad_s[CONV_PAD:, :] = x_ref[...]

    def lanes_of(hh):
        return slice(hh * DN_DIM, (hh + 1) * DN_DIM)

    def conv(pad_s, w_ref, hh, r0):
        win = pad_s[r0:r0 + CONV_PAD + pr, lanes_of(hh)]
        acc = w_ref[0:1, lanes_of(hh)] * win
        for j in range(1, CONV_WIDTH):
            acc = pltpu.roll(acc, 1, 0) + w_ref[j:j + 1, lanes_of(hh)] * win
        return _silu(acc[CONV_PAD:])

    def l2n(x):
        return x * lax.rsqrt(jnp.sum(x * x, axis=-1, keepdims=True) + RMS_EPS)

    def gate(ref, hh, r0):
        pick = ci == hp * DN_HPS + hh
        col = jnp.sum(jnp.where(pick, ref[r0:r0 + pr, :], 0.0), axis=1, keepdims=True)
        return jnp.broadcast_to(col, (pr, pr))

    ri = lax.broadcasted_iota(jnp.int32, (pr, pr), 0)
    ci = lax.broadcasted_iota(jnp.int32, (pr, pr), 1)
    same_chunk = (ri // cl) == (ci // cl)
    causal = same_chunk & (ri >= ci)
    strict = same_chunk & (ri > ci)
    eye = (ri == ci).astype(F32)
    first_rows = ri < cl
    first_rows2 = lax.broadcasted_iota(jnp.int32, (pr, 2 * DN_DIM), 0) < cl
    nw = nw_ref[...]
    bf = lambda a: a.astype(BF16)
    cat0 = lambda *a: jnp.concatenate(a, axis=0)
    cat1 = lambda *a: jnp.concatenate(a, axis=1)

    n_pairs = DN_ROWS // pr
    probs = [(hh, a * pr) for a in range(n_pairs) for hh in range(DN_HPS)]

    q_p = [l2n(conv(xq_s, wq_ref, hh, r0)) * (DN_DIM ** -0.5) for hh, r0 in probs]
    k_p = [l2n(conv(xk_s, wk_ref, hh, r0)) for hh, r0 in probs]
    v_p = [conv(xv_s, wv_ref, hh, r0) for hh, r0 in probs]
    cum = [gate(cum_ref, hh, r0) for hh, r0 in probs]
    beta = [gate(beta_ref, hh, r0) for hh, r0 in probs]
    gamma = [jnp.exp(c) for c in cum]
    decay = [jnp.where(causal, jnp.exp(c - c.T), 0.0) for c in cum]
    kq = [_dot_nt(bf(cat0(k, q)), bf(k)) for k, q in zip(k_p, q_p)]
    x = [jnp.where(strict, -(b * m[:pr] * d), 0.0) for b, m, d in zip(beta, kq, decay)]
    att = [bf(m[pr:] * d) for m, d in zip(kq, decay)]
    t = [eye + m for m in x]
    xp = [_dot(bf(m), bf(m)) for m in x]
    for _ in range(int(math.log2(cl)) - 2):
        y = [_dot(bf(p), bf(cat1(p, m))) for p, m in zip(xp, t)]
        xp = [m[:, :pr] for m in y]
        t = [m + n[:, pr:] for m, n in zip(t, y)]
    t = [m + _dot(bf(p), bf(m)) for p, m in zip(xp, t)]
    wu = [_dot(bf(m), bf(cat1(k * (b * g), v * b)))
          for m, k, v, b, g in zip(t, k_p, v_p, beta, gamma)]
    awu = [_dot(a, bf(m)) for a, m in zip(att, wu)]
    q_prime = [bf(q * g - m[:, :DN_DIM]) for q, g, m in zip(q_p, gamma, awu)]
    au = [m[:, DN_DIM:] for m in awu]
    k_dec_t = [bf((k * jnp.exp(jnp.where(first_rows, c[cl - 1:cl, :], c[pr - 1:pr, :]) - c)).T)
               for k, c in zip(k_p, cum)]
    kwu1 = [_dot(kt, bf(jnp.where(first_rows2, m, 0.0))) for kt, m in zip(k_dec_t, wu)]
    kwu2 = [_dot(kt, bf(jnp.where(first_rows2, 0.0, m))) for kt, m in zip(k_dec_t, wu)]
    g1 = [jnp.exp(c[cl - 1:cl, :]) for c in cum]
    g2 = [jnp.exp(c[pr - 1:pr, :]) for c in cum]
    comp = [_dot(bf(-m2[:, :DN_DIM]), bf(cat1(-m1[:, :DN_DIM], m1[:, DN_DIM:]))) for m1, m2 in zip(kwu1, kwu2)]
    mp12 = [-(b * m1[:, :DN_DIM]) - a * m2[:, :DN_DIM] + c[:, :DN_DIM]
            for a, b, m1, m2, c in zip(g1, g2, kwu1, kwu2, comp)]
    n12 = [b * m1[:, DN_DIM:] + c[:, DN_DIM:] + m2[:, DN_DIM:] for b, m1, m2, c in zip(g2, kwu1, kwu2, comp)]
    lhs = [cat0(qp[:cl], bf(-m1[:, :DN_DIM]), bf(m)) for qp, m1, m in zip(q_prime, kwu1, mp12)]

    s = [s_s[hh] for hh in range(DN_HPS)]
    for a in range(n_pairs):
        js = [a * DN_HPS + hh for hh in range(DN_HPS)]
        r = [_dot(lhs[j], bf(s[hh])) for hh, j in enumerate(js)]
        s_mid = [g1[j] * s[hh] + r[hh][cl:cl + DN_DIM] + kwu1[j][:, DN_DIM:] for hh, j in enumerate(js)]
        s_new = [(g1[j] * g2[j]) * s[hh] + r[hh][cl + DN_DIM:] + n12[j] for hh, j in enumerate(js)]
        o2 = [_dot(q_prime[j][cl:], bf(s_mid[hh])) for hh, j in enumerate(js)]
        for hh, j in enumerate(js):
            o = cat0(r[hh][:cl], o2[hh]) + au[j]
            o = o * lax.rsqrt(jnp.mean(o * o, axis=-1, keepdims=True) + RMS_EPS) * nw
            r0 = a * pr
            o_ref[r0:r0 + pr, lanes_of(hh)] = (o * _silu(z_ref[r0:r0 + pr, lanes_of(hh)])).astype(o_ref.dtype)
        s = s_new

    for hh in range(DN_HPS):
        s_s[hh] = s[hh]

    @pl.when(i == pl.num_programs(1) - 1)
    def _():
        for hh in range(DN_HPS):
            so_ref[hh] = s[hh]


def _dn_prompt(proj, cum, beta, w_conv, norm_w):
    rows = DN_ROWS
    width = DN_HPS * DN_DIM
    q0 = ATTN_QKV_WIDTH // width
    k0 = q0 + DN_WIDTH // width
    v0 = k0 + DN_WIDTH // width
    z0 = v0 + DN_WIDTH // width
    per = rows // CONV_PAD

    def blk(c0):
        return pl.BlockSpec((rows, width), lambda h, i: (i, c0 + h))

    def prev(c0):
        return pl.BlockSpec((CONV_PAD, width), lambda h, i: (jnp.maximum(i * per - 1, 0), c0 + h))

    def wblk(c0):
        return pl.BlockSpec((CONV_WIDTH, width), lambda h, i: (0, c0 + h))

    gate = pl.BlockSpec((rows, LANES), lambda h, i: (i, 0))
    vec = pl.BlockSpec((1, DN_DIM), lambda h, i: (0, 0))
    pad = pltpu.VMEM((rows + CONV_PAD, width), F32)
    return pl.pallas_call(
        _dn_prompt_kernel,
        grid=(DN_HEADS // DN_HPS, SEQ // rows),
        in_specs=[blk(q0), prev(q0), blk(k0), prev(k0), blk(v0), prev(v0), blk(z0), gate, gate,
                  wblk(0), wblk(DN_WIDTH // width), wblk(2 * DN_WIDTH // width), vec],
        out_specs=[pl.BlockSpec((rows, width), lambda h, i: (i, h)),
                   pl.BlockSpec((DN_HPS, DN_DIM, DN_DIM), lambda h, i: (h, 0, 0))],
        out_shape=[jax.ShapeDtypeStruct((SEQ, DN_WIDTH), BF16),
                   jax.ShapeDtypeStruct((DN_HEADS, DN_DIM, DN_DIM), F32)],
        scratch_shapes=[pad, pad, pad, pltpu.VMEM((DN_HPS, DN_DIM, DN_DIM), F32)],
        compiler_params=_cparams(("parallel", "arbitrary")),
        name="dn_prompt",
    )(proj, proj, proj, proj, proj, proj, proj, cum, beta, w_conv, w_conv, w_conv,
      norm_w.reshape(1, DN_DIM))


DN_SB = 8


def _dn_gates_of_head(ab, alog, dtb, h):
    lane = lax.broadcasted_iota(jnp.int32, ab.shape, 1)
    g_all = -jnp.exp(alog) * _softplus(ab + dtb)
    g = jnp.sum(jnp.where(lane == h, g_all, 0.0), axis=1, keepdims=True)
    beta = jnp.sum(jnp.where(lane == h + DN_HEADS, _sigmoid(ab), 0.0), axis=1, keepdims=True)
    return jnp.broadcast_to(g, ab.shape), jnp.broadcast_to(beta, ab.shape)


def _dn_sample_prep_kernel(x_ref, sc_ref, ab_ref, w_ref, alog_ref, dtb_ref,
                           qt_ref, kt_ref, vb_ref, bg_ref, gam_ref):
    ab = ab_ref[...]
    alog, dtb = alog_ref[...], dtb_ref[...]

    def conv(col):
        sl = slice(col, col + LANES)
        y = w_ref[CONV_WIDTH - 1:CONV_WIDTH, sl] * x_ref[:, sl]
        for j in range(CONV_WIDTH - 1):
            y = y + w_ref[j:j + 1, sl] * sc_ref[:, j * CONV_DIM + col:j * CONV_DIM + col + LANES]
        return _silu(y)

    def l2n(x):
        return x * lax.rsqrt(jnp.sum(x * x, axis=-1, keepdims=True) + RMS_EPS)

    for h in range(DN_HEADS):
        q = l2n(conv(h * DN_DIM)) * (DN_DIM ** -0.5)
        k = l2n(conv(DN_WIDTH + h * DN_DIM))
        v = conv(2 * DN_WIDTH + h * DN_DIM)
        g_b, beta_b = _dn_gates_of_head(ab, alog, dtb, h)
        gamma = jnp.exp(g_b)
        qt_ref[h] = q.T
        kt_ref[h] = k.T
        vb_ref[h] = v * beta_b
        bg_ref[h] = beta_b * gamma
        gam_ref[h] = gamma


def _dn_sample_prep(qkv_s, conv_state, ab_s, w_conv, alog_pad, dtb_pad):
    n = DEC_BATCH
    full = lambda a: pl.BlockSpec(a.shape, lambda i: (0,) * a.ndim)
    hm = jax.ShapeDtypeStruct((DN_HEADS, n, DN_DIM), F32)
    hm_spec = pl.BlockSpec((DN_HEADS, n, DN_DIM), lambda i: (0, 0, 0))
    args = (qkv_s, conv_state, ab_s, w_conv, alog_pad, dtb_pad)
    return pl.pallas_call(
        _dn_sample_prep_kernel,
        grid=(1,),
        in_specs=[full(a) for a in args],
        out_specs=[hm_spec] * 5,
        out_shape=[hm] * 5,
        compiler_params=_cparams(("arbitrary",)),
        name="dn_sample_prep",
    )(*args)


def _dn_sample_kernel(s_ref, qt_ref, kt_ref, vb_ref, bg_ref, gam_ref, z_ref, nw_ref,
                      so_ref, o_ref, o_s):
    step = pl.program_id(0)
    lane = lax.broadcasted_iota(jnp.int32, (DN_DIM, DEC_BATCH), 1)

    def head(h, carry):
        kt = kt_ref[h]
        qt = qt_ref[h]
        for bb in range(DN_SB):
            b = step * DN_SB + bb
            pick = lane == b
            kcol = jnp.sum(jnp.where(pick, kt, 0.0), axis=1, keepdims=True)
            qcol = jnp.sum(jnp.where(pick, qt, 0.0), axis=1, keepdims=True)
            s = s_ref[bb, h]
            ks = jnp.sum(kcol * s, axis=0, keepdims=True)
            u = vb_ref[h, pl.ds(b, 1), :] - bg_ref[h, pl.ds(b, 1), :] * ks
            s_new = gam_ref[h, pl.ds(b, 1), :] * s + kcol * u
            so_ref[bb, h] = s_new
            o_s[h, pl.ds(bb, 1), :] = jnp.sum(qcol * s_new, axis=0, keepdims=True)
        return carry

    lax.fori_loop(0, DN_HEADS, head, 0)
    nw = nw_ref[...]
    for h in range(DN_HEADS):
        o = o_s[h]
        o = o * lax.rsqrt(jnp.mean(o * o, axis=-1, keepdims=True) + RMS_EPS) * nw
        sl = slice(h * DN_DIM, (h + 1) * DN_DIM)
        o_ref[:, sl] = o * _silu(z_ref[:, sl])


def _dn_sample(state, qt, kt, vb, bg, gam, z_s, norm_w):
    n = DEC_BATCH
    sb = DN_SB
    sblk = pl.BlockSpec((sb, DN_HEADS, DN_DIM, DN_DIM), lambda i: (i, 0, 0, 0))
    hm = pl.BlockSpec((DN_HEADS, n, DN_DIM), lambda i: (0, 0, 0))
    row = pl.BlockSpec((sb, DN_WIDTH), lambda i: (i, 0))
    return pl.pallas_call(
        _dn_sample_kernel,
        grid=(n // sb,),
        in_specs=[sblk, hm, hm, hm, hm, hm, row, pl.BlockSpec((1, DN_DIM), lambda i: (0, 0))],
        out_specs=[sblk, row],
        out_shape=[jax.ShapeDtypeStruct(state.shape, F32), jax.ShapeDtypeStruct((n, DN_WIDTH), F32)],
        scratch_shapes=[pltpu.VMEM((DN_HEADS, sb, DN_DIM), F32)],
        compiler_params=_cparams(("parallel",)),
        name="dn_sample",
    )(state, qt, kt, vb, bg, gam, z_s, norm_w.reshape(1, DN_DIM))


BM = 1040
BN_IN = 512
BM_OUT = 1024
BN_OUT = 512
BF_FFN = 256


def _rope_tables(pos):
    half = ATTN_HEAD_DIM // 2
    inv_freq = ROPE_THETA ** (-jnp.arange(half, dtype=F32) / half)
    ang = pos.astype(F32)[:, None] * inv_freq
    cos, sin = jnp.cos(ang), jnp.sin(ang)
    reps = LANES // ATTN_HEAD_DIM
    return jnp.tile(cos, (1, 2 * reps)), jnp.tile(jnp.concatenate([-sin, sin], axis=1), (1, reps))


def _group_major(a):
    lead = a.shape[:-1]
    a = a.reshape(lead + (ATTN_KV_HEADS, ATTN_GROUP, ATTN_HEAD_DIM))
    return jnp.swapaxes(a, -3, -2).reshape(lead + (ATTN_WIDTH,))


def _layer(x_prompt, x_sample, cache_k, cache_v, state_conv, state_delta, w_in, b_attn, attn_sinks,
           w_conv, dn_a_log, dn_dt_bias, dn_norm_w, w_out, ln1_g, ln1_b, w_gate, w_up, w_down,
           ln2_g, ln2_b):
    n_s = DEC_BATCH
    xb = _xcast(x_prompt, x_sample)
    w_in_t = w_in.T
    w_ab = jnp.pad(w_in_t[MAIN_WIDTH:], ((0, LANES - 2 * DN_HEADS), (0, 0)))
    proj = _matmul_nt(xb, w_in_t, BM, BN_IN, MAIN_WIDTH // BN_IN)
    proj_ab = _matmul_nt(xb, w_ab, BM, LANES, 1)

    pad16 = lambda v: jnp.pad(v, (0, LANES - DN_HEADS)).reshape(1, LANES)
    alog_pad, dtb_pad = pad16(dn_a_log), pad16(dn_dt_bias)

    cos_t, sin_t = _rope_tables(jnp.arange(SEQ, dtype=jnp.int32))
    attn_p, pk, pv = _attn_prompt(proj, b_attn, attn_sinks, cos_t, sin_t)
    cum, beta = _dn_gates(proj_ab, alog_pad, dtb_pad)
    dn_p, ps = _dn_prompt(proj, cum, beta, w_conv, dn_norm_w)
    pc = proj[SEQ - (CONV_WIDTH - 1):SEQ, ATTN_QKV_WIDTH:ATTN_QKV_WIDTH + CONV_DIM]

    proj_s = proj[SEQ:]
    cos_r, sin_r = _rope_tables(jnp.full((1,), PAST_LEN, jnp.int32))
    sinks_gk = jnp.pad(attn_sinks.reshape(ATTN_KV_HEADS, ATTN_GROUP).T,
                       ((0, 0), (0, LANES - ATTN_KV_HEADS)))
    attn_s_perm, sk, sv = _attn_sample(
        _group_major(proj_s[:, :ATTN_WIDTH]),
        proj_s[:, ATTN_WIDTH:ATTN_WIDTH + KV_WIDTH],
        proj_s[:, ATTN_WIDTH + KV_WIDTH:ATTN_QKV_WIDTH],
        _group_major(b_attn[:ATTN_WIDTH]).reshape(1, ATTN_WIDTH),
        b_attn[ATTN_WIDTH:ATTN_WIDTH + KV_WIDTH].reshape(1, KV_WIDTH),
        b_attn[ATTN_WIDTH + KV_WIDTH:].reshape(1, KV_WIDTH),
        cos_r, sin_r, sinks_gk, cache_k.reshape(n_s, WINDOW, KV_WIDTH),
        cache_v.reshape(n_s, WINDOW, KV_WIDTH))
    attn_s = jnp.swapaxes(attn_s_perm.reshape(n_s, ATTN_GROUP, ATTN_KV_HEADS, ATTN_HEAD_DIM), 1, 2)
    attn_s = attn_s.reshape(n_s, ATTN_WIDTH)

    qkv_s = proj_s[:, ATTN_QKV_WIDTH:ATTN_QKV_WIDTH + CONV_DIM]
    z_s = proj_s[:, ATTN_QKV_WIDTH + CONV_DIM:]
    qt, kt, vb, bg, gam = _dn_sample_prep(
        qkv_s, state_conv.reshape(n_s, (CONV_WIDTH - 1) * CONV_DIM), proj_ab[SEQ:],
        w_conv, alog_pad, dtb_pad)
    ss, dn_s = _dn_sample(state_delta, qt, kt, vb, bg, gam, z_s, dn_norm_w)
    sc = jnp.concatenate([state_conv[:, 1:], qkv_s[:, None, :]], axis=1)

    mixed_p, mixed_s = _out_proj(attn_p, dn_p, attn_s.astype(BF16), dn_s.astype(BF16), w_out,
                                 BM_OUT, BN_OUT)
    h32, hb = _ln1(x_prompt, x_sample, mixed_p, mixed_s, ln1_g, ln1_b)
    ffn = _ffn(hb, w_gate, w_up, w_down, BM, BF_FFN)
    y_p, y_s = _ln2(h32, ffn, ln2_g, ln2_b)
    return (y_p, y_s, pk.reshape(WINDOW, ATTN_KV_HEADS, ATTN_HEAD_DIM),
            pv.reshape(WINDOW, ATTN_KV_HEADS, ATTN_HEAD_DIM), pc, ps,
            sk.reshape(n_s, WINDOW, ATTN_KV_HEADS, ATTN_HEAD_DIM),
            sv.reshape(n_s, WINDOW, ATTN_KV_HEADS, ATTN_HEAD_DIM), sc, ss)


def kernel(x_prompt, x_sample, cache_swa_k, cache_swa_v, state_conv, state_delta, w_in, b_attn,
           attn_sinks, w_conv, dn_a_log, dn_dt_bias, dn_norm_w, w_out, ln1_g, ln1_b, w_gate, w_up,
           w_down, ln2_g, ln2_b):
    assert x_prompt.shape == (1, SEQ, D_MODEL) and x_sample.shape == (DEC_BATCH, 1, D_MODEL)
    assert w_in.shape[0] == 1, "one layer"
    y_p, y_s, pk, pv, pc, ps, sk, sv, sc, ss = _layer(
        x_prompt[0], x_sample[:, 0], cache_swa_k[0], cache_swa_v[0], state_conv[0], state_delta[0],
        w_in[0], b_attn[0], attn_sinks[0], w_conv[0], dn_a_log[0], dn_dt_bias[0], dn_norm_w[0],
        w_out[0], ln1_g[0], ln1_b[0], w_gate[0], w_up[0], w_down[0], ln2_g[0], ln2_b[0])
    return (y_p[None], y_s[:, None], pk[None, None], pv[None, None], pc[None, None],
            ps[None, None], sk[None], sv[None], sc[None], ss[None])
```

```python
import math

import jax
import jax.numpy as jnp
from jax import lax
from jax.experimental import pallas as pl
from jax.experimental.pallas import tpu as pltpu

D_MODEL = 4096
SEQ = 8192
DEC_BATCH = 128
PAST_LEN = 8192
ROWS = SEQ + DEC_BATCH

ATTN_HEADS = 32
ATTN_KV_HEADS = 8
ATTN_HEAD_DIM = 64
ATTN_GROUP = ATTN_HEADS // ATTN_KV_HEADS
ATTN_WIDTH = ATTN_HEADS * ATTN_HEAD_DIM
KV_WIDTH = ATTN_KV_HEADS * ATTN_HEAD_DIM
WINDOW = 128
ROPE_THETA = 10000.0
DN_HEADS = 16
DN_DIM = 128
DN_WIDTH = DN_HEADS * DN_DIM
CONV_WIDTH = 4
CONV_DIM = 3 * DN_WIDTH
DN_CHUNK = 64
ATTN_QKV_WIDTH = ATTN_WIDTH + 2 * KV_WIDTH
MAIN_WIDTH = ATTN_QKV_WIDTH + CONV_DIM + DN_WIDTH
FFN_HIDDEN = 11008
DEEPNORM_ALPHA = 2.0 ** 0.25
LN_EPS = 1e-5
RMS_EPS = 1e-6

LANES = 128
VMEM_LIMIT = 56 * 1024 * 1024
VMEM_LIMIT_BIG = 61 * 1024 * 1024

F32 = jnp.float32
BF16 = jnp.bfloat16


def _cparams(sem, vmem=VMEM_LIMIT):
    return pltpu.CompilerParams(dimension_semantics=sem, vmem_limit_bytes=vmem)


def _dot(a, b):
    return jnp.dot(a, b, preferred_element_type=F32)


def _dot_nt(a, b):
    return lax.dot_general(a, b, (((1,), (1,)), ((), ())), preferred_element_type=F32)


def _sigmoid(x):
    return 0.5 * jnp.tanh(0.5 * x) + 0.5


def _silu(x):
    return x * _sigmoid(x)


def _softplus(x):
    return jnp.maximum(x, 0.0) + jnp.log(1.0 + jnp.exp(-jnp.abs(x)))


def _rope(x, cos, sin_signed):
    lane = lax.broadcasted_iota(jnp.int32, x.shape, 1)
    first_half = (lane % ATTN_HEAD_DIM) < (ATTN_HEAD_DIM // 2)
    partner = jnp.where(first_half, pltpu.roll(x, LANES - ATTN_HEAD_DIM // 2, 1),
                        pltpu.roll(x, ATTN_HEAD_DIM // 2, 1))
    return x * cos + partner * sin_signed


def _mm_nt_kernel(x_ref, wt_ref, o_ref):
    o_ref[...] = _dot_nt(x_ref[...], wt_ref[...].astype(BF16)).astype(o_ref.dtype)


def _matmul_nt(x, wt, bm, bn, nb, out_dtype=F32):
    m, k = x.shape
    return pl.pallas_call(
        _mm_nt_kernel,
        grid=(m // bm, nb),
        in_specs=[pl.BlockSpec((bm, k), lambda i, j: (i, 0), pipeline_mode=pl.Buffered(1)),
                  pl.BlockSpec((bn, k), lambda i, j: (j, 0))],
        out_specs=pl.BlockSpec((bm, bn), lambda i, j: (i, j)),
        out_shape=jax.ShapeDtypeStruct((m, nb * bn), out_dtype),
        compiler_params=_cparams(("parallel", "parallel")),
        name="matmul",
    )(x, wt)


def _out_proj_kernel(ap_ref, dp_ref, as_ref, ds_ref, w_ref, op_ref, os_ref, wb_s):
    i = pl.program_id(1)
    last = pl.num_programs(1) - 1
    ka = ap_ref.shape[1]

    @pl.when(i == 0)
    def _():
        wb_s[...] = w_ref[...].astype(BF16)

    @pl.when(i < last)
    def _():
        op_ref[...] = _dot(ap_ref[...], wb_s[:ka, :]) + _dot(dp_ref[...], wb_s[ka:, :])

    @pl.when(i == last)
    def _():
        os_ref[...] = _dot(as_ref[...], wb_s[:ka, :]) + _dot(ds_ref[...], wb_s[ka:, :])


def _out_proj(a_p, d_p, a_s, d_s, w, bm, bn):
    m, ka = a_p.shape
    kd = d_p.shape[1]
    ms = a_s.shape[0]
    n = w.shape[1]
    nt = m // bm
    tile = lambda width: pl.BlockSpec((bm, width), lambda j, i: (jnp.minimum(i, nt - 1), 0))
    dec = lambda width: pl.BlockSpec((ms, width), lambda j, i: (0, 0))
    return pl.pallas_call(
        _out_proj_kernel,
        grid=(n // bn, nt + 1),
        in_specs=[tile(ka), tile(kd), dec(ka), dec(kd),
                  pl.BlockSpec((ka + kd, bn), lambda j, i: (0, j), pipeline_mode=pl.Buffered(1))],
        out_specs=[pl.BlockSpec((bm, bn), lambda j, i: (jnp.minimum(i, nt - 1), j)),
                   pl.BlockSpec((ms, bn), lambda j, i: (0, j))],
        out_shape=[jax.ShapeDtypeStruct((m, n), F32), jax.ShapeDtypeStruct((ms, n), F32)],
        scratch_shapes=[pltpu.VMEM((ka + kd, bn), BF16)],
        compiler_params=_cparams(("parallel", "arbitrary"), VMEM_LIMIT_BIG),
        name="out_proj",
    )(a_p, d_p, a_s, d_s, w)


FFN_NCHUNK = 1024
FFN_ROW_ALIGN = 16


def _ffn_kernel(h_ref, wg_ref, wu_ref, wd_ref, o_ref):
    f = pl.program_id(1)

    @pl.when(f == 0)
    def _():
        o_ref[...] = jnp.zeros_like(o_ref)

    bm = h_ref.shape[0]
    cut = (bm // 2) // FFN_ROW_ALIGN * FFN_ROW_ALIGN
    groups = ((0, cut), (cut, bm))
    wg = wg_ref[...].astype(BF16)
    wu = wu_ref[...].astype(BF16)
    gates = [_dot(h_ref[a:b, :], wg) for a, b in groups]
    ups = [_dot(h_ref[a:b, :], wu) for a, b in groups]
    acts = [(_silu(g) * u).astype(BF16) for g, u in zip(gates, ups)]
    for n in range(0, o_ref.shape[1], FFN_NCHUNK):
        wd = wd_ref[:, n:n + FFN_NCHUNK].astype(BF16)
        for (a, b), act in zip(groups, acts):
            o_ref[a:b, n:n + FFN_NCHUNK] += _dot(act, wd)


def _ffn(h, w_gate, w_up, w_down, bm, bf):
    m, d = h.shape
    hidden = w_gate.shape[1]
    once = pl.Buffered(1)
    return pl.pallas_call(
        _ffn_kernel,
        grid=(m // bm, hidden // bf),
        in_specs=[pl.BlockSpec((bm, d), lambda i, f: (i, 0), pipeline_mode=once),
                  pl.BlockSpec((d, bf), lambda i, f: (0, f)),
                  pl.BlockSpec((d, bf), lambda i, f: (0, f)),
                  pl.BlockSpec((bf, d), lambda i, f: (f, 0))],
        out_specs=pl.BlockSpec((bm, d), lambda i, f: (i, 0), pipeline_mode=once),
        out_shape=jax.ShapeDtypeStruct((m, d), F32),
        compiler_params=_cparams(("parallel", "arbitrary"), VMEM_LIMIT_BIG),
        name="ffn",
    )(h, w_gate, w_up, w_down)


BR = WINDOW
NB_PROMPT = SEQ // BR


def _deepnorm(x, mixed, g, b):
    v = DEEPNORM_ALPHA * x + mixed
    mu = jnp.mean(v, axis=-1, keepdims=True)
    c = v - mu
    var = jnp.mean(c * c, axis=-1, keepdims=True)
    return c * lax.rsqrt(var + LN_EPS) * g + b


def _prompt_rows(width):
    return pl.BlockSpec((BR, width), lambda i: (jnp.minimum(i, NB_PROMPT - 1), 0))


def _decode_rows(width):
    return pl.BlockSpec((BR, width), lambda i: (0, 0))


def _xcast_kernel(xp_ref, xs_ref, o_ref):
    i = pl.program_id(0)
    o_ref[...] = jnp.where(i < NB_PROMPT, xp_ref[...], xs_ref[...]).astype(o_ref.dtype)


def _xcast(x_prompt, x_sample):
    d = x_prompt.shape[1]
    return pl.pallas_call(
        _xcast_kernel,
        grid=(NB_PROMPT + 1,),
        in_specs=[_prompt_rows(d), _decode_rows(d)],
        out_specs=pl.BlockSpec((BR, d), lambda i: (i, 0)),
        out_shape=jax.ShapeDtypeStruct((ROWS, d), BF16),
        compiler_params=_cparams(("arbitrary",)),
        name="xcast",
    )(x_prompt, x_sample)


def _ln1_kernel(xp_ref, xs_ref, mp_ref, ms_ref, g_ref, b_ref, o_ref, ob_ref):
    i = pl.program_id(0)
    is_prompt = i < NB_PROMPT
    x = jnp.where(is_prompt, xp_ref[...], xs_ref[...])
    mixed = jnp.where(is_prompt, mp_ref[...], ms_ref[...])
    y = _deepnorm(x, mixed, g_ref[...], b_ref[...])
    o_ref[...] = y
    ob_ref[...] = y.astype(BF16)


def _ln1(x_prompt, x_sample, mixed_p, mixed_s, g, b):
    d = mixed_p.shape[1]
    row = pl.BlockSpec((BR, d), lambda i: (i, 0))
    vec = pl.BlockSpec((1, d), lambda i: (0, 0))
    return pl.pallas_call(
        _ln1_kernel,
        grid=(NB_PROMPT + 1,),
        in_specs=[_prompt_rows(d), _decode_rows(d), _prompt_rows(d), _decode_rows(d), vec, vec],
        out_specs=[row, row],
        out_shape=[jax.ShapeDtypeStruct((ROWS, d), F32), jax.ShapeDtypeStruct((ROWS, d), BF16)],
        compiler_params=_cparams(("arbitrary",)),
        name="ln1",
    )(x_prompt, x_sample, mixed_p, mixed_s, g.reshape(1, d), b.reshape(1, d))


def _ln2_kernel(h_ref, f_ref, g_ref, b_ref, yp_ref, ys_ref):
    i = pl.program_id(0)
    y = _deepnorm(h_ref[...], f_ref[...], g_ref[...], b_ref[...])

    @pl.when(i < NB_PROMPT)
    def _():
        yp_ref[...] = y

    @pl.when(i >= NB_PROMPT)
    def _():
        ys_ref[...] = y


def _ln2(h32, ffn, g, b):
    d = h32.shape[1]
    row = pl.BlockSpec((BR, d), lambda i: (i, 0))
    vec = pl.BlockSpec((1, d), lambda i: (0, 0))
    return pl.pallas_call(
        _ln2_kernel,
        grid=(NB_PROMPT + 1,),
        in_specs=[row, row, vec, vec],
        out_specs=[_prompt_rows(d), _decode_rows(d)],
        out_shape=[jax.ShapeDtypeStruct((SEQ, d), F32), jax.ShapeDtypeStruct((DEC_BATCH, d), F32)],
        compiler_params=_cparams(("arbitrary",)),
        name="ln2",
    )(h32, ffn, g.reshape(1, d), b.reshape(1, d))


def _attn_prompt_kernel(sink_ref, q_ref, kc_ref, kp_ref, vc_ref, vp_ref, b_ref,
                        cc_ref, sc_ref, cp_ref, sp_ref, o_ref, ko_ref, vo_ref):
    i = pl.program_id(0)
    w = WINDOW
    cos_c, sin_c = cc_ref[...], sc_ref[...]
    cos_p, sin_p = cp_ref[...], sp_ref[...]
    n_kchunk = KV_WIDTH // LANES

    k_cur, k_prev, v_cur, v_prev = [], [], [], []
    for c in range(n_kchunk):
        sl = slice(c * LANES, (c + 1) * LANES)
        bk = b_ref[:, ATTN_WIDTH + c * LANES:ATTN_WIDTH + (c + 1) * LANES]
        bv = b_ref[:, ATTN_WIDTH + KV_WIDTH + c * LANES:ATTN_WIDTH + KV_WIDTH + (c + 1) * LANES]
        kc = _rope(kc_ref[:, sl] + bk, cos_c, sin_c)
        kp = _rope(kp_ref[:, sl] + bk, cos_p, sin_p)
        vc = vc_ref[:, sl] + bv
        vp = vp_ref[:, sl] + bv
        ko_ref[:, sl] = kc
        vo_ref[:, sl] = vc
        k_cur.append(kc)
        k_prev.append(kp)
        v_cur.append(vc)
        v_prev.append(vp)

    rows = ATTN_GROUP * w
    r = lax.broadcasted_iota(jnp.int32, (rows, 2 * w), 0) % w
    col = lax.broadcasted_iota(jnp.int32, (rows, 2 * w), 1)
    valid = (col > r) & (col <= r + w) & ((col >= w) | (i > 0))
    row_id = lax.broadcasted_iota(jnp.int32, (rows, 1), 0)
    lane_k = lax.broadcasted_iota(jnp.int32, (2 * w, LANES), 1)
    lane_o = lax.broadcasted_iota(jnp.int32, (w, LANES), 1)
    scale = ATTN_HEAD_DIM ** -0.5

    k2s, v2s, q4s = [], [], []
    for hk in range(ATTN_KV_HEADS):
        kchunk, khalf = hk // 2, hk % 2
        k2 = jnp.concatenate([k_prev[kchunk], k_cur[kchunk]], axis=0)
        in_half = (lane_k // ATTN_HEAD_DIM) == khalf
        k2s.append(jnp.where(in_half, k2, 0.0).astype(BF16))
        v2s.append(jnp.concatenate([v_prev[kchunk], v_cur[kchunk]], axis=0).astype(BF16))
        qs = []
        for g in range(ATTN_GROUP):
            hq = hk * ATTN_GROUP + g
            qchunk, qhalf = hq // 2, hq % 2
            sl = slice(qchunk * LANES, (qchunk + 1) * LANES)
            qc = _rope(q_ref[:, sl] + b_ref[:, sl], cos_c, sin_c) * scale
            if qhalf != khalf:
                qc = pltpu.roll(qc, ATTN_HEAD_DIM, 1)
            qs.append(qc)
        q4s.append(jnp.concatenate(qs, axis=0).astype(BF16))
    def softmax_with_sink(hk, s):
        s = jnp.where(valid, s, -jnp.inf)
        sink = jnp.zeros((rows, 1), F32)
        for g in range(ATTN_GROUP):
            sink = jnp.where(row_id // w == g, sink_ref[hk * ATTN_GROUP + g], sink)
        m = jnp.maximum(jnp.max(jnp.maximum(s[:, :w], s[:, w:]), axis=-1, keepdims=True), sink)
        e = jnp.exp(s - m)
        den = jnp.sum(e[:, :w] + e[:, w:], axis=-1, keepdims=True) + jnp.exp(sink - m)
        return (e / den).astype(BF16)

    pvs = []
    scores = _dot_nt(q4s[0], k2s[0])
    for hk in range(ATTN_KV_HEADS):
        nxt = _dot_nt(q4s[hk + 1], k2s[hk + 1]) if hk + 1 < ATTN_KV_HEADS else None
        pvs.append(_dot(softmax_with_sink(hk, scores), v2s[hk]))
        scores = nxt
    for hk, pv in enumerate(pvs):
        khalf = hk % 2
        outs = []
        for g in range(ATTN_GROUP):
            og = pv[g * w:(g + 1) * w, :]
            if (g % 2) != khalf:
                og = pltpu.roll(og, ATTN_HEAD_DIM, 1)
            outs.append(og)
        for j in range(ATTN_GROUP // 2):
            oc = jnp.where(lane_o < ATTN_HEAD_DIM, outs[2 * j], outs[2 * j + 1])
            c = hk * (ATTN_GROUP // 2) + j
            o_ref[:, c * LANES:(c + 1) * LANES] = oc.astype(o_ref.dtype)


def _attn_prompt(proj, b_attn, sinks, cos_t, sin_t):
    nb = SEQ // WINDOW
    kcol = ATTN_WIDTH // KV_WIDTH
    prev = lambda i: jnp.maximum(i - 1, 0)
    return pl.pallas_call(
        _attn_prompt_kernel,
        grid=(nb,),
        in_specs=[pl.BlockSpec(memory_space=pltpu.SMEM),
                  pl.BlockSpec((WINDOW, ATTN_WIDTH), lambda i: (i, 0)),
                  pl.BlockSpec((WINDOW, KV_WIDTH), lambda i: (i, kcol)),
                  pl.BlockSpec((WINDOW, KV_WIDTH), lambda i: (prev(i), kcol)),
                  pl.BlockSpec((WINDOW, KV_WIDTH), lambda i: (i, kcol + 1)),
                  pl.BlockSpec((WINDOW, KV_WIDTH), lambda i: (prev(i), kcol + 1)),
                  pl.BlockSpec((1, ATTN_QKV_WIDTH), lambda i: (0, 0)),
                  pl.BlockSpec((WINDOW, LANES), lambda i: (i, 0)),
                  pl.BlockSpec((WINDOW, LANES), lambda i: (i, 0)),
                  pl.BlockSpec((WINDOW, LANES), lambda i: (prev(i), 0)),
                  pl.BlockSpec((WINDOW, LANES), lambda i: (prev(i), 0))],
        out_specs=[pl.BlockSpec((WINDOW, ATTN_WIDTH), lambda i: (i, 0)),
                   pl.BlockSpec((WINDOW, KV_WIDTH), lambda i: (0, 0)),
                   pl.BlockSpec((WINDOW, KV_WIDTH), lambda i: (0, 0))],
        out_shape=[jax.ShapeDtypeStruct((SEQ, ATTN_WIDTH), BF16),
                   jax.ShapeDtypeStruct((WINDOW, KV_WIDTH), F32),
                   jax.ShapeDtypeStruct((WINDOW, KV_WIDTH), F32)],
        compiler_params=_cparams(("arbitrary",)),
        name="attn_prompt",
    )(sinks, proj, proj, proj, proj, proj, b_attn.reshape(1, ATTN_QKV_WIDTH),
      cos_t, sin_t, cos_t, sin_t)


ATTN_SB = 8


def _attn_sample_kernel(q_ref, k_ref, v_ref, bq_ref, bk_ref, bv_ref, cos_ref, sin_ref,
                        sink_ref, e_ref, et_ref, ck_ref, cv_ref, o_ref, ko_ref, vo_ref):
    cos, sin = cos_ref[...], sin_ref[...]
    scale = ATTN_HEAD_DIM ** -0.5
    qr = []
    for c in range(ATTN_WIDTH // LANES):
        sl = slice(c * LANES, (c + 1) * LANES)
        qr.append(_rope(q_ref[:, sl] + bq_ref[:, sl], cos, sin) * scale)
    q = jnp.concatenate(qr, axis=1)
    kn = jnp.concatenate(
        [_rope(k_ref[:, c * LANES:(c + 1) * LANES] + bk_ref[:, c * LANES:(c + 1) * LANES], cos, sin)
         for c in range(KV_WIDTH // LANES)], axis=1)
    vn = v_ref[...] + bv_ref[...]
    lb = WINDOW
    for b in range(ATTN_SB):
        ko_ref[b, 0:lb - 1, :] = ck_ref[b, 1:lb, :]
        ko_ref[b, lb - 1:lb, :] = kn[b:b + 1, :]
        vo_ref[b, 0:lb - 1, :] = cv_ref[b, 1:lb, :]
        vo_ref[b, lb - 1:lb, :] = vn[b:b + 1, :]
        kb = ko_ref[b]
        vb = vo_ref[b]
        prod = jnp.concatenate(
            [kb * q[b:b + 1, g * KV_WIDTH:(g + 1) * KV_WIDTH] for g in range(ATTN_GROUP)], axis=0)
        s = _dot(prod.astype(BF16), e_ref[...])
        ps = []
        for g in range(ATTN_GROUP):
            sg = s[g * lb:(g + 1) * lb, :]
            sink = sink_ref[g:g + 1, :]
            m = jnp.maximum(jnp.max(sg, axis=0, keepdims=True), sink)
            e = jnp.exp(sg - m)
            den = jnp.sum(e, axis=0, keepdims=True) + jnp.exp(sink - m)
            ps.append(e / den)
        pe = _dot(jnp.concatenate(ps, axis=0).astype(BF16), et_ref[...])
        for g in range(ATTN_GROUP):
            og = jnp.sum(pe[g * lb:(g + 1) * lb, :] * vb, axis=0, keepdims=True)
            o_ref[b:b + 1, g * KV_WIDTH:(g + 1) * KV_WIDTH] = og


def _attn_sample(q_perm, k_new, v_new, bq_perm, bk, bv, cos_row, sin_row, sinks_gk, cache_k, cache_v):
    n = DEC_BATCH
    sb = ATTN_SB
    head_of_lane = jnp.arange(KV_WIDTH) // ATTN_HEAD_DIM
    e_mat = (head_of_lane[:, None] == jnp.arange(LANES)[None, :]).astype(BF16)
    row = lambda w: pl.BlockSpec((sb, w), lambda i: (i, 0))
    vec = lambda w: pl.BlockSpec((1, w), lambda i: (0, 0))
    full = lambda a: pl.BlockSpec(a.shape, lambda i: (0,) * a.ndim)
    cache = pl.BlockSpec((sb, WINDOW, KV_WIDTH), lambda i: (i, 0, 0))
    return pl.pallas_call(
        _attn_sample_kernel,
        grid=(n // sb,),
        in_specs=[row(ATTN_WIDTH), row(KV_WIDTH), row(KV_WIDTH),
                  vec(ATTN_WIDTH), vec(KV_WIDTH), vec(KV_WIDTH), vec(LANES), vec(LANES),
                  full(sinks_gk), full(e_mat), full(e_mat.T), cache, cache],
        out_specs=[row(ATTN_WIDTH), cache, cache],
        out_shape=[jax.ShapeDtypeStruct((n, ATTN_WIDTH), F32),
                   jax.ShapeDtypeStruct(cache_k.shape, F32),
                   jax.ShapeDtypeStruct(cache_v.shape, F32)],
        compiler_params=_cparams(("parallel",)),
        name="attn_sample",
    )(q_perm, k_new, v_new, bq_perm, bk, bv, cos_row, sin_row, sinks_gk, e_mat, e_mat.T,
      cache_k, cache_v)


DN_ROWS = 512
DN_PAIR = 2 * DN_CHUNK
CONV_PAD = 8
DN_HPS = 4


def _dn_gates_kernel(ab_ref, alog_ref, dtb_ref, cum_ref, beta_ref):
    ab = ab_ref[...]
    g = -jnp.exp(alog_ref[...]) * _softplus(ab + dtb_ref[...])
    row = lax.broadcasted_iota(jnp.int32, ab.shape, 0) % DN_CHUNK
    shift = 1
    while shift < DN_CHUNK:
        g = g + jnp.where(row >= shift, pltpu.roll(g, shift, 0), 0.0)
        shift *= 2
    cum_ref[...] = g
    beta_ref[...] = pltpu.roll(_sigmoid(ab), LANES - DN_HEADS, 1)


def _dn_gates(proj_ab, alog_pad, dtb_pad):
    blk = pl.BlockSpec((DN_ROWS, LANES), lambda i: (i, 0))
    vec = pl.BlockSpec((1, LANES), lambda i: (0, 0))
    out = jax.ShapeDtypeStruct((SEQ, LANES), F32)
    return pl.pallas_call(
        _dn_gates_kernel,
        grid=(SEQ // DN_ROWS,),
        in_specs=[blk, vec, vec],
        out_specs=[blk, blk],
        out_shape=[out, out],
        compiler_params=_cparams(("parallel",)),
        name="dn_gates",
    )(proj_ab, alog_pad, dtb_pad)


def _dn_prompt_kernel(q_ref, qp_ref, k_ref, kp_ref, v_ref, vp_ref, z_ref, cum_ref, beta_ref,
                      wq_ref, wk_ref, wv_ref, nw_ref, o_ref, so_ref, xq_s, xk_s, xv_s, s_s):
    hp = pl.program_id(0)
    i = pl.program_id(1)
    pr = DN_PAIR
    cl = DN_CHUNK

    @pl.when(i == 0)
    def _():
        s_s[...] = jnp.zeros_like(s_s)

    for x_ref, prev_ref, pad_s in ((q_ref, qp_ref, xq_s), (k_ref, kp_ref, xk_s), (v_ref, vp_ref, xv_s)):
        pad_s[0:CONV_PAD, :] = jnp.where(i > 0, prev_ref[...], 0.0)
        pad_s[CONV_PAD:, :] = x_ref[...]

    def lanes_of(hh):
        return slice(hh * DN_DIM, (hh + 1) * DN_DIM)

    def conv(pad_s, w_ref, hh, r0):
        win = pad_s[r0:r0 + CONV_PAD + pr, lanes_of(hh)]
        acc = w_ref[0:1, lanes_of(hh)] * win
        for j in range(1, CONV_WIDTH):
            acc = pltpu.roll(acc, 1, 0) + w_ref[j:j + 1, lanes_of(hh)] * win
        return _silu(acc[CONV_PAD:])

    def l2n(x):
        return x * lax.rsqrt(jnp.sum(x * x, axis=-1, keepdims=True) + RMS_EPS)

    def gate(ref, hh, r0):
        pick = ci == hp * DN_HPS + hh
        col = jnp.sum(jnp.where(pick, ref[r0:r0 + pr, :], 0.0), axis=1, keepdims=True)
        return jnp.broadcast_to(col, (pr, pr))

    ri = lax.broadcasted_iota(jnp.int32, (pr, pr), 0)
    ci = lax.broadcasted_iota(jnp.int32, (pr, pr), 1)
    same_chunk = (ri // cl) == (ci // cl)
    causal = same_chunk & (ri >= ci)
    strict = same_chunk & (ri > ci)
    eye = (ri == ci).astype(F32)
    first_rows = ri < cl
    first_rows2 = lax.broadcasted_iota(jnp.int32, (pr, 2 * DN_DIM), 0) < cl
    nw = nw_ref[...]
    bf = lambda a: a.astype(BF16)
    cat0 = lambda *a: jnp.concatenate(a, axis=0)
    cat1 = lambda *a: jnp.concatenate(a, axis=1)

    n_pairs = DN_ROWS // pr
    probs = [(hh, a * pr) for a in range(n_pairs) for hh in range(DN_HPS)]

    q_p = [l2n(conv(xq_s, wq_ref, hh, r0)) * (DN_DIM ** -0.5) for hh, r0 in probs]
    k_p = [l2n(conv(xk_s, wk_ref, hh, r0)) for hh, r0 in probs]
    v_p = [conv(xv_s, wv_ref, hh, r0) for hh, r0 in probs]
    cum = [gate(cum_ref, hh, r0) for hh, r0 in probs]
    beta = [gate(beta_ref, hh, r0) for hh, r0 in probs]
    gamma = [jnp.exp(c) for c in cum]
    decay = [jnp.where(causal, jnp.exp(c - c.T), 0.0) for c in cum]
    kq = [_dot_nt(bf(cat0(k, q)), bf(k)) for k, q in zip(k_p, q_p)]
    x = [jnp.where(strict, -(b * m[:pr] * d), 0.0) for b, m, d in zip(beta, kq, decay)]
    att = [bf(m[pr:] * d) for m, d in zip(kq, decay)]
    t = [eye + m for m in x]
    xp = [_dot(bf(m), bf(m)) for m in x]
    for _ in range(int(math.log2(cl)) - 2):
        y = [_dot(bf(p), bf(cat1(p, m))) for p, m in zip(xp, t)]
        xp = [m[:, :pr] for m in y]
        t = [m + n[:, pr:] for m, n in zip(t, y)]
    t = [m + _dot(bf(p), bf(m)) for p, m in zip(xp, t)]
    wu = [_dot(bf(m), bf(cat1(k * (b * g), v * b)))
          for m, k, v, b, g in zip(t, k_p, v_p, beta, gamma)]
    awu = [_dot(a, bf(m)) for a, m in zip(att, wu)]
    q_prime = [bf(q * g - m[:, :DN_DIM]) for q, g, m in zip(q_p, gamma, awu)]
    au = [m[:, DN_DIM:] for m in awu]
    k_dec_t = [bf((k * jnp.exp(jnp.where(first_rows, c[cl - 1:cl, :], c[pr - 1:pr, :]) - c)).T)
               for k, c in zip(k_p, cum)]
    kwu1 = [_dot(kt, bf(jnp.where(first_rows2, m, 0.0))) for kt, m in zip(k_dec_t, wu)]
    kwu2 = [_dot(kt, bf(jnp.where(first_rows2, 0.0, m))) for kt, m in zip(k_dec_t, wu)]
    g1 = [jnp.exp(c[cl - 1:cl, :]) for c in cum]
    g2 = [jnp.exp(c[pr - 1:pr, :]) for c in cum]
    comp = [_dot(bf(-m2[:, :DN_DIM]), bf(cat1(-m1[:, :DN_DIM], m1[:, DN_DIM:]))) for m1, m2 in zip(kwu1, kwu2)]
    mp12 = [-(b * m1[:, :DN_DIM]) - a * m2[:, :DN_DIM] + c[:, :DN_DIM]
            for a, b, m1, m2, c in zip(g1, g2, kwu1, kwu2, comp)]
    n12 = [b * m1[:, DN_DIM:] + c[:, DN_DIM:] + m2[:, DN_DIM:] for b, m1, m2, c in zip(g2, kwu1, kwu2, comp)]
    lhs = [cat0(qp[:cl], bf(-m1[:, :DN_DIM]), bf(m)) for qp, m1, m in zip(q_prime, kwu1, mp12)]

    s = [s_s[hh] for hh in range(DN_HPS)]
    for a in range(n_pairs):
        js = [a * DN_HPS + hh for hh in range(DN_HPS)]
        r = [_dot(lhs[j], bf(s[hh])) for hh, j in enumerate(js)]
        s_mid = [g1[j] * s[hh] + r[hh][cl:cl + DN_DIM] + kwu1[j][:, DN_DIM:] for hh, j in enumerate(js)]
        s_new = [(g1[j] * g2[j]) * s[hh] + r[hh][cl + DN_DIM:] + n12[j] for hh, j in enumerate(js)]
        o2 = [_dot(q_prime[j][cl:], bf(s_mid[hh])) for hh, j in enumerate(js)]
        for hh, j in enumerate(js):
            o = cat0(r[hh][:cl], o2[hh]) + au[j]
            o = o * lax.rsqrt(jnp.mean(o * o, axis=-1, keepdims=True) + RMS_EPS) * nw
            r0 = a * pr
            o_ref[r0:r0 + pr, lanes_of(hh)] = (o * _silu(z_ref[r0:r0 + pr, lanes_of(hh)])).astype(o_ref.dtype)
        s = s_new

    for hh in range(DN_HPS):
        s_s[hh] = s[hh]

    @pl.when(i == pl.num_programs(1) - 1)
    def _():
        for hh in range(DN_HPS):
            so_ref[hh] = s[hh]


def _dn_prompt(proj, cum, beta, w_conv, norm_w):
    rows = DN_ROWS
    width = DN_HPS * DN_DIM
    q0 = ATTN_QKV_WIDTH // width
    k0 = q0 + DN_WIDTH // width
    v0 = k0 + DN_WIDTH // width
    z0 = v0 + DN_WIDTH // width
    per = rows // CONV_PAD

    def blk(c0):
        return pl.BlockSpec((rows, width), lambda h, i: (i, c0 + h))

    def prev(c0):
        return pl.BlockSpec((CONV_PAD, width), lambda h, i: (jnp.maximum(i * per - 1, 0), c0 + h))

    def wblk(c0):
        return pl.BlockSpec((CONV_WIDTH, width), lambda h, i: (0, c0 + h))

    gate = pl.BlockSpec((rows, LANES), lambda h, i: (i, 0))
    vec = pl.BlockSpec((1, DN_DIM), lambda h, i: (0, 0))
    pad = pltpu.VMEM((rows + CONV_PAD, width), F32)
    return pl.pallas_call(
        _dn_prompt_kernel,
        grid=(DN_HEADS // DN_HPS, SEQ // rows),
        in_specs=[blk(q0), prev(q0), blk(k0), prev(k0), blk(v0), prev(v0), blk(z0), gate, gate,
                  wblk(0), wblk(DN_WIDTH // width), wblk(2 * DN_WIDTH // width), vec],
        out_specs=[pl.BlockSpec((rows, width), lambda h, i: (i, h)),
                   pl.BlockSpec((DN_HPS, DN_DIM, DN_DIM), lambda h, i: (h, 0, 0))],
        out_shape=[jax.ShapeDtypeStruct((SEQ, DN_WIDTH), BF16),
                   jax.ShapeDtypeStruct((DN_HEADS, DN_DIM, DN_DIM), F32)],
        scratch_shapes=[pad, pad, pad, pltpu.VMEM((DN_HPS, DN_DIM, DN_DIM), F32)],
        compiler_params=_cparams(("parallel", "arbitrary")),
        name="dn_prompt",
    )(proj, proj, proj, proj, proj, proj, proj, cum, beta, w_conv, w_conv, w_conv,
      norm_w.reshape(1, DN_DIM))


DN_SB = 8


def _dn_gates_of_head(ab, alog, dtb, h):
    lane = lax.broadcasted_iota(jnp.int32, ab.shape, 1)
    g_all = -jnp.exp(alog) * _softplus(ab + dtb)
    g = jnp.sum(jnp.where(lane == h, g_all, 0.0), axis=1, keepdims=True)
    beta = jnp.sum(jnp.where(lane == h + DN_HEADS, _sigmoid(ab), 0.0), axis=1, keepdims=True)
    return jnp.broadcast_to(g, ab.shape), jnp.broadcast_to(beta, ab.shape)


def _dn_sample_prep_kernel(x_ref, sc_ref, ab_ref, w_ref, alog_ref, dtb_ref,
                           qt_ref, kt_ref, vb_ref, bg_ref, gam_ref):
    ab = ab_ref[...]
    alog, dtb = alog_ref[...], dtb_ref[...]

    def conv(col):
        sl = slice(col, col + LANES)
        y = w_ref[CONV_WIDTH - 1:CONV_WIDTH, sl] * x_ref[:, sl]
        for j in range(CONV_WIDTH - 1):
            y = y + w_ref[j:j + 1, sl] * sc_ref[:, j * CONV_DIM + col:j * CONV_DIM + col + LANES]
        return _silu(y)

    def l2n(x):
        return x * lax.rsqrt(jnp.sum(x * x, axis=-1, keepdims=True) + RMS_EPS)

    for h in range(DN_HEADS):
        q = l2n(conv(h * DN_DIM)) * (DN_DIM ** -0.5)
        k = l2n(conv(DN_WIDTH + h * DN_DIM))
        v = conv(2 * DN_WIDTH + h * DN_DIM)
        g_b, beta_b = _dn_gates_of_head(ab, alog, dtb, h)
        gamma = jnp.exp(g_b)
        qt_ref[h] = q.T
        kt_ref[h] = k.T
        vb_ref[h] = v * beta_b
        bg_ref[h] = beta_b * gamma
        gam_ref[h] = gamma


def _dn_sample_prep(qkv_s, conv_state, ab_s, w_conv, alog_pad, dtb_pad):
    n = DEC_BATCH
    full = lambda a: pl.BlockSpec(a.shape, lambda i: (0,) * a.ndim)
    hm = jax.ShapeDtypeStruct((DN_HEADS, n, DN_DIM), F32)
    hm_spec = pl.BlockSpec((DN_HEADS, n, DN_DIM), lambda i: (0, 0, 0))
    args = (qkv_s, conv_state, ab_s, w_conv, alog_pad, dtb_pad)
    return pl.pallas_call(
        _dn_sample_prep_kernel,
        grid=(1,),
        in_specs=[full(a) for a in args],
        out_specs=[hm_spec] * 5,
        out_shape=[hm] * 5,
        compiler_params=_cparams(("arbitrary",)),
        name="dn_sample_prep",
    )(*args)


def _dn_sample_kernel(s_ref, qt_ref, kt_ref, vb_ref, bg_ref, gam_ref, z_ref, nw_ref,
                      so_ref, o_ref, o_s):
    step = pl.program_id(0)
    lane = lax.broadcasted_iota(jnp.int32, (DN_DIM, DEC_BATCH), 1)

    def head(h, carry):
        kt = kt_ref[h]
        qt = qt_ref[h]
        for bb in range(DN_SB):
            b = step * DN_SB + bb
            pick = lane == b
            kcol = jnp.sum(jnp.where(pick, kt, 0.0), axis=1, keepdims=True)
            qcol = jnp.sum(jnp.where(pick, qt, 0.0), axis=1, keepdims=True)
            s = s_ref[bb, h]
            ks = jnp.sum(kcol * s, axis=0, keepdims=True)
            u = vb_ref[h, pl.ds(b, 1), :] - bg_ref[h, pl.ds(b, 1), :] * ks
            s_new = gam_ref[h, pl.ds(b, 1), :] * s + kcol * u
            so_ref[bb, h] = s_new
            o_s[h, pl.ds(bb, 1), :] = jnp.sum(qcol * s_new, axis=0, keepdims=True)
        return carry

    lax.fori_loop(0, DN_HEADS, head, 0)
    nw = nw_ref[...]
    for h in range(DN_HEADS):
        o = o_s[h]
        o = o * lax.rsqrt(jnp.mean(o * o, axis=-1, keepdims=True) + RMS_EPS) * nw
        sl = slice(h * DN_DIM, (h + 1) * DN_DIM)
        o_ref[:, sl] = o * _silu(z_ref[:, sl])


def _dn_sample(state, qt, kt, vb, bg, gam, z_s, norm_w):
    n = DEC_BATCH
    sb = DN_SB
    sblk = pl.BlockSpec((sb, DN_HEADS, DN_DIM, DN_DIM), lambda i: (i, 0, 0, 0))
    hm = pl.BlockSpec((DN_HEADS, n, DN_DIM), lambda i: (0, 0, 0))
    row = pl.BlockSpec((sb, DN_WIDTH), lambda i: (i, 0))
    return pl.pallas_call(
        _dn_sample_kernel,
        grid=(n // sb,),
        in_specs=[sblk, hm, hm, hm, hm, hm, row, pl.BlockSpec((1, DN_DIM), lambda i: (0, 0))],
        out_specs=[sblk, row],
        out_shape=[jax.ShapeDtypeStruct(state.shape, F32), jax.ShapeDtypeStruct((n, DN_WIDTH), F32)],
        scratch_shapes=[pltpu.VMEM((DN_HEADS, sb, DN_DIM), F32)],
        compiler_params=_cparams(("parallel",)),
        name="dn_sample",
    )(state, qt, kt, vb, bg, gam, z_s, norm_w.reshape(1, DN_DIM))


BM = 1040
BM_IN = 2080
BN_IN = 512
BM_OUT = 1024
BN_OUT = 1024
BF_FFN = 256


def _rope_tables(pos):
    half = ATTN_HEAD_DIM // 2
    inv_freq = ROPE_THETA ** (-jnp.arange(half, dtype=F32) / half)
    ang = pos.astype(F32)[:, None] * inv_freq
    cos, sin = jnp.cos(ang), jnp.sin(ang)
    reps = LANES // ATTN_HEAD_DIM
    return jnp.tile(cos, (1, 2 * reps)), jnp.tile(jnp.concatenate([-sin, sin], axis=1), (1, reps))


def _group_major(a):
    lead = a.shape[:-1]
    a = a.reshape(lead + (ATTN_KV_HEADS, ATTN_GROUP, ATTN_HEAD_DIM))
    return jnp.swapaxes(a, -3, -2).reshape(lead + (ATTN_WIDTH,))


def _layer(x_prompt, x_sample, cache_k, cache_v, state_conv, state_delta, w_in, b_attn, attn_sinks,
           w_conv, dn_a_log, dn_dt_bias, dn_norm_w, w_out, ln1_g, ln1_b, w_gate, w_up, w_down,
           ln2_g, ln2_b):
    n_s = DEC_BATCH
    xb = _xcast(x_prompt, x_sample)
    w_in_t = w_in.T
    w_ab = jnp.pad(w_in_t[MAIN_WIDTH:], ((0, LANES - 2 * DN_HEADS), (0, 0)))
    proj = _matmul_nt(xb, w_in_t, BM_IN, BN_IN, MAIN_WIDTH // BN_IN)
    proj_ab = _matmul_nt(xb, w_ab, BM_IN, LANES, 1)

    pad16 = lambda v: jnp.pad(v, (0, LANES - DN_HEADS)).reshape(1, LANES)
    alog_pad, dtb_pad = pad16(dn_a_log), pad16(dn_dt_bias)

    cos_t, sin_t = _rope_tables(jnp.arange(SEQ, dtype=jnp.int32))
    attn_p, pk, pv = _attn_prompt(proj, b_attn, attn_sinks, cos_t, sin_t)
    cum, beta = _dn_gates(proj_ab, alog_pad, dtb_pad)
    dn_p, ps = _dn_prompt(proj, cum, beta, w_conv, dn_norm_w)
    pc = proj[SEQ - (CONV_WIDTH - 1):SEQ, ATTN_QKV_WIDTH:ATTN_QKV_WIDTH + CONV_DIM]

    proj_s = proj[SEQ:]
    cos_r, sin_r = _rope_tables(jnp.full((1,), PAST_LEN, jnp.int32))
    sinks_gk = jnp.pad(attn_sinks.reshape(ATTN_KV_HEADS, ATTN_GROUP).T,
                       ((0, 0), (0, LANES - ATTN_KV_HEADS)))
    attn_s_perm, sk, sv = _attn_sample(
        _group_major(proj_s[:, :ATTN_WIDTH]),
        proj_s[:, ATTN_WIDTH:ATTN_WIDTH + KV_WIDTH],
        proj_s[:, ATTN_WIDTH + KV_WIDTH:ATTN_QKV_WIDTH],
        _group_major(b_attn[:ATTN_WIDTH]).reshape(1, ATTN_WIDTH),
        b_attn[ATTN_WIDTH:ATTN_WIDTH + KV_WIDTH].reshape(1, KV_WIDTH),
        b_attn[ATTN_WIDTH + KV_WIDTH:].reshape(1, KV_WIDTH),
        cos_r, sin_r, sinks_gk, cache_k.reshape(n_s, WINDOW, KV_WIDTH),
        cache_v.reshape(n_s, WINDOW, KV_WIDTH))
    attn_s = jnp.swapaxes(attn_s_perm.reshape(n_s, ATTN_GROUP, ATTN_KV_HEADS, ATTN_HEAD_DIM), 1, 2)
    attn_s = attn_s.reshape(n_s, ATTN_WIDTH)

    qkv_s = proj_s[:, ATTN_QKV_WIDTH:ATTN_QKV_WIDTH + CONV_DIM]
    z_s = proj_s[:, ATTN_QKV_WIDTH + CONV_DIM:]
    qt, kt, vb, bg, gam = _dn_sample_prep(
        qkv_s, state_conv.reshape(n_s, (CONV_WIDTH - 1) * CONV_DIM), proj_ab[SEQ:],
        w_conv, alog_pad, dtb_pad)
    ss, dn_s = _dn_sample(state_delta, qt, kt, vb, bg, gam, z_s, dn_norm_w)
    sc = jnp.concatenate([state_conv[:, 1:], qkv_s[:, None, :]], axis=1)

    mixed_p, mixed_s = _out_proj(attn_p, dn_p, attn_s.astype(BF16), dn_s.astype(BF16), w_out,
                                 BM_OUT, BN_OUT)
    h32, hb = _ln1(x_prompt, x_sample, mixed_p, mixed_s, ln1_g, ln1_b)
    ffn = _ffn(hb, w_gate, w_up, w_down, BM, BF_FFN)
    y_p, y_s = _ln2(h32, ffn, ln2_g, ln2_b)
    return (y_p, y_s, pk.reshape(WINDOW, ATTN_KV_HEADS, ATTN_HEAD_DIM),
            pv.reshape(WINDOW, ATTN_KV_HEADS, ATTN_HEAD_DIM), pc, ps,
            sk.reshape(n_s, WINDOW, ATTN_KV_HEADS, ATTN_HEAD_DIM),
            sv.reshape(n_s, WINDOW, ATTN_KV_HEADS, ATTN_HEAD_DIM), sc, ss)


def kernel(x_prompt, x_sample, cache_swa_k, cache_swa_v, state_conv, state_delta, w_in, b_attn,
           attn_sinks, w_conv, dn_a_log, dn_dt_bias, dn_norm_w, w_out, ln1_g, ln1_b, w_gate, w_up,
           w_down, ln2_g, ln2_b):
    assert x_prompt.shape == (1, SEQ, D_MODEL) and x_sample.shape == (DEC_BATCH, 1, D_MODEL)
    assert w_in.shape[0] == 1, "one layer"
    y_p, y_s, pk, pv, pc, ps, sk, sv, sc, ss = _layer(
        x_prompt[0], x_sample[:, 0], cache_swa_k[0], cache_swa_v[0], state_conv[0], state_delta[0],
        w_in[0], b_attn[0], attn_sinks[0], w_conv[0], dn_a_log[0], dn_dt_bias[0], dn_norm_w[0],
        w_out[0], ln1_g[0], ln1_b[0], w_gate[0], w_up[0], w_down[0], ln2_g[0], ln2_b[0])
    return (y_p[None], y_s[:, None], pk[None, None], pv[None, None], pc[None, None],
            ps[None, None], sk[None], sv[None], sc[None], ss[None])
```

```python
import math

import jax
import jax.numpy as jnp
from jax import lax
from jax.experimental import pallas as pl
from jax.experimental.pallas import tpu as pltpu

D_MODEL = 4096
SEQ = 8192
DEC_BATCH = 128
PAST_LEN = 8192
ROWS = SEQ + DEC_BATCH

ATTN_HEADS = 32
ATTN_KV_HEADS = 8
ATTN_HEAD_DIM = 64
ATTN_GROUP = ATTN_HEADS // ATTN_KV_HEADS
ATTN_WIDTH = ATTN_HEADS * ATTN_HEAD_DIM
KV_WIDTH = ATTN_KV_HEADS * ATTN_HEAD_DIM
WINDOW = 128
ROPE_THETA = 10000.0
DN_HEADS = 16
DN_DIM = 128
DN_WIDTH = DN_HEADS * DN_DIM
CONV_WIDTH = 4
CONV_DIM = 3 * DN_WIDTH
DN_CHUNK = 64
ATTN_QKV_WIDTH = ATTN_WIDTH + 2 * KV_WIDTH
MAIN_WIDTH = ATTN_QKV_WIDTH + CONV_DIM + DN_WIDTH
FFN_HIDDEN = 11008
DEEPNORM_ALPHA = 2.0 ** 0.25
LN_EPS = 1e-5
RMS_EPS = 1e-6

LANES = 128
VMEM_LIMIT = 56 * 1024 * 1024
VMEM_LIMIT_BIG = 61 * 1024 * 1024

F32 = jnp.float32
BF16 = jnp.bfloat16


def _cparams(sem, vmem=VMEM_LIMIT):
    return pltpu.CompilerParams(dimension_semantics=sem, vmem_limit_bytes=vmem)


def _dot(a, b):
    return jnp.dot(a, b, preferred_element_type=F32)


def _dot_nt(a, b):
    return lax.dot_general(a, b, (((1,), (1,)), ((), ())), preferred_element_type=F32)


def _sigmoid(x):
    return 0.5 * jnp.tanh(0.5 * x) + 0.5


def _silu(x):
    return x * _sigmoid(x)


def _softplus(x):
    return jnp.maximum(x, 0.0) + jnp.log(1.0 + jnp.exp(-jnp.abs(x)))


def _rope(x, cos, sin_signed):
    lane = lax.broadcasted_iota(jnp.int32, x.shape, 1)
    first_half = (lane % ATTN_HEAD_DIM) < (ATTN_HEAD_DIM // 2)
    partner = jnp.where(first_half, pltpu.roll(x, LANES - ATTN_HEAD_DIM // 2, 1),
                        pltpu.roll(x, ATTN_HEAD_DIM // 2, 1))
    return x * cos + partner * sin_signed


def _mm_nt_kernel(x_ref, wt_ref, wt_tail_ref, o_ref, o_tail_ref):
    o_ref[...] = _dot_nt(x_ref[...], wt_ref[...].astype(BF16))

    @pl.when(pl.program_id(1) == 0)
    def _():
        o_tail_ref[...] = _dot_nt(x_ref[...], wt_tail_ref[...].astype(BF16))


def _matmul_nt(x, wt, wt_tail, bm, bn, nb):
    m, k = x.shape
    nt = wt_tail.shape[0]
    return pl.pallas_call(
        _mm_nt_kernel,
        grid=(m // bm, nb),
        in_specs=[pl.BlockSpec((bm, k), lambda i, j: (i, 0), pipeline_mode=pl.Buffered(1)),
                  pl.BlockSpec((bn, k), lambda i, j: (j, 0)),
                  pl.BlockSpec((nt, k), lambda i, j: (0, 0))],
        out_specs=[pl.BlockSpec((bm, bn), lambda i, j: (i, j)),
                   pl.BlockSpec((bm, nt), lambda i, j: (i, 0))],
        out_shape=[jax.ShapeDtypeStruct((m, nb * bn), F32), jax.ShapeDtypeStruct((m, nt), F32)],
        compiler_params=_cparams(("parallel", "arbitrary")),
        name="matmul",
    )(x, wt, wt_tail)


def _out_proj_kernel(ap_ref, dp_ref, as_ref, ds_ref, w_ref, op_ref, os_ref, wb_s):
    i = pl.program_id(1)
    last = pl.num_programs(1) - 1
    ka = ap_ref.shape[1]

    @pl.when(i == 0)
    def _():
        wb_s[...] = w_ref[...].astype(BF16)

    @pl.when(i < last)
    def _():
        op_ref[...] = _dot(ap_ref[...], wb_s[:ka, :]) + _dot(dp_ref[...], wb_s[ka:, :])

    @pl.when(i == last)
    def _():
        os_ref[...] = _dot(as_ref[...], wb_s[:ka, :]) + _dot(ds_ref[...], wb_s[ka:, :])


def _out_proj(a_p, d_p, a_s, d_s, w, bm, bn):
    m, ka = a_p.shape
    kd = d_p.shape[1]
    ms = a_s.shape[0]
    n = w.shape[1]
    nt = m // bm
    tile = lambda width: pl.BlockSpec((bm, width), lambda j, i: (jnp.minimum(i, nt - 1), 0))
    dec = lambda width: pl.BlockSpec((ms, width), lambda j, i: (0, 0))
    return pl.pallas_call(
        _out_proj_kernel,
        grid=(n // bn, nt + 1),
        in_specs=[tile(ka), tile(kd), dec(ka), dec(kd),
                  pl.BlockSpec((ka + kd, bn), lambda j, i: (0, j), pipeline_mode=pl.Buffered(1))],
        out_specs=[pl.BlockSpec((bm, bn), lambda j, i: (jnp.minimum(i, nt - 1), j)),
                   pl.BlockSpec((ms, bn), lambda j, i: (0, j))],
        out_shape=[jax.ShapeDtypeStruct((m, n), F32), jax.ShapeDtypeStruct((ms, n), F32)],
        scratch_shapes=[pltpu.VMEM((ka + kd, bn), BF16)],
        compiler_params=_cparams(("parallel", "arbitrary"), VMEM_LIMIT_BIG),
        name="out_proj",
    )(a_p, d_p, a_s, d_s, w)


FFN_NCHUNK = 1024
FFN_ROW_ALIGN = 16


def _ffn_kernel(h_ref, wg_ref, wu_ref, wd_ref, o_ref):
    f = pl.program_id(1)

    @pl.when(f == 0)
    def _():
        o_ref[...] = jnp.zeros_like(o_ref)

    bm = h_ref.shape[0]
    cut = (bm // 2) // FFN_ROW_ALIGN * FFN_ROW_ALIGN
    groups = ((0, cut), (cut, bm))
    wg = wg_ref[...].astype(BF16)
    wu = wu_ref[...].astype(BF16)
    gates = [_dot(h_ref[a:b, :], wg) for a, b in groups]
    ups = [_dot(h_ref[a:b, :], wu) for a, b in groups]
    acts = [(_silu(g) * u).astype(BF16) for g, u in zip(gates, ups)]
    for n in range(0, o_ref.shape[1], FFN_NCHUNK):
        wd = wd_ref[:, n:n + FFN_NCHUNK].astype(BF16)
        for (a, b), act in zip(groups, acts):
            o_ref[a:b, n:n + FFN_NCHUNK] += _dot(act, wd)


def _ffn(h, w_gate, w_up, w_down, bm, bf):
    m, d = h.shape
    hidden = w_gate.shape[1]
    once = pl.Buffered(1)
    return pl.pallas_call(
        _ffn_kernel,
        grid=(m // bm, hidden // bf),
        in_specs=[pl.BlockSpec((bm, d), lambda i, f: (i, 0), pipeline_mode=once),
                  pl.BlockSpec((d, bf), lambda i, f: (0, f)),
                  pl.BlockSpec((d, bf), lambda i, f: (0, f)),
                  pl.BlockSpec((bf, d), lambda i, f: (f, 0))],
        out_specs=pl.BlockSpec((bm, d), lambda i, f: (i, 0), pipeline_mode=once),
        out_shape=jax.ShapeDtypeStruct((m, d), F32),
        compiler_params=_cparams(("parallel", "arbitrary"), VMEM_LIMIT_BIG),
        name="ffn",
    )(h, w_gate, w_up, w_down)


BR = WINDOW
NB_PROMPT = SEQ // BR


def _deepnorm(x, mixed, g, b):
    v = DEEPNORM_ALPHA * x + mixed
    mu = jnp.mean(v, axis=-1, keepdims=True)
    c = v - mu
    var = jnp.mean(c * c, axis=-1, keepdims=True)
    return c * lax.rsqrt(var + LN_EPS) * g + b


def _prompt_rows(width):
    return pl.BlockSpec((BR, width), lambda i: (jnp.minimum(i, NB_PROMPT - 1), 0))


def _decode_rows(width):
    return pl.BlockSpec((BR, width), lambda i: (0, 0))


def _xcast_kernel(xp_ref, xs_ref, o_ref):
    i = pl.program_id(0)
    o_ref[...] = jnp.where(i < NB_PROMPT, xp_ref[...], xs_ref[...]).astype(o_ref.dtype)


def _xcast(x_prompt, x_sample):
    d = x_prompt.shape[1]
    return pl.pallas_call(
        _xcast_kernel,
        grid=(NB_PROMPT + 1,),
        in_specs=[_prompt_rows(d), _decode_rows(d)],
        out_specs=pl.BlockSpec((BR, d), lambda i: (i, 0)),
        out_shape=jax.ShapeDtypeStruct((ROWS, d), BF16),
        compiler_params=_cparams(("arbitrary",)),
        name="xcast",
    )(x_prompt, x_sample)


def _ln1_kernel(xp_ref, xs_ref, mp_ref, ms_ref, g_ref, b_ref, o_ref, ob_ref):
    i = pl.program_id(0)
    is_prompt = i < NB_PROMPT
    x = jnp.where(is_prompt, xp_ref[...], xs_ref[...])
    mixed = jnp.where(is_prompt, mp_ref[...], ms_ref[...])
    y = _deepnorm(x, mixed, g_ref[...], b_ref[...])
    o_ref[...] = y
    ob_ref[...] = y.astype(BF16)


def _ln1(x_prompt, x_sample, mixed_p, mixed_s, g, b):
    d = mixed_p.shape[1]
    row = pl.BlockSpec((BR, d), lambda i: (i, 0))
    vec = pl.BlockSpec((1, d), lambda i: (0, 0))
    return pl.pallas_call(
        _ln1_kernel,
        grid=(NB_PROMPT + 1,),
        in_specs=[_prompt_rows(d), _decode_rows(d), _prompt_rows(d), _decode_rows(d), vec, vec],
        out_specs=[row, row],
        out_shape=[jax.ShapeDtypeStruct((ROWS, d), F32), jax.ShapeDtypeStruct((ROWS, d), BF16)],
        compiler_params=_cparams(("arbitrary",)),
        name="ln1",
    )(x_prompt, x_sample, mixed_p, mixed_s, g.reshape(1, d), b.reshape(1, d))


def _ln2_kernel(h_ref, f_ref, g_ref, b_ref, yp_ref, ys_ref):
    i = pl.program_id(0)
    y = _deepnorm(h_ref[...], f_ref[...], g_ref[...], b_ref[...])

    @pl.when(i < NB_PROMPT)
    def _():
        yp_ref[...] = y

    @pl.when(i >= NB_PROMPT)
    def _():
        ys_ref[...] = y


def _ln2(h32, ffn, g, b):
    d = h32.shape[1]
    row = pl.BlockSpec((BR, d), lambda i: (i, 0))
    vec = pl.BlockSpec((1, d), lambda i: (0, 0))
    return pl.pallas_call(
        _ln2_kernel,
        grid=(NB_PROMPT + 1,),
        in_specs=[row, row, vec, vec],
        out_specs=[_prompt_rows(d), _decode_rows(d)],
        out_shape=[jax.ShapeDtypeStruct((SEQ, d), F32), jax.ShapeDtypeStruct((DEC_BATCH, d), F32)],
        compiler_params=_cparams(("arbitrary",)),
        name="ln2",
    )(h32, ffn, g.reshape(1, d), b.reshape(1, d))


def _attn_prompt_kernel(sink_ref, q_ref, kc_ref, kp_ref, vc_ref, vp_ref, b_ref,
                        cc_ref, sc_ref, cp_ref, sp_ref, o_ref, ko_ref, vo_ref):
    i = pl.program_id(0)
    w = WINDOW
    cos_c, sin_c = cc_ref[...], sc_ref[...]
    cos_p, sin_p = cp_ref[...], sp_ref[...]
    n_kchunk = KV_WIDTH // LANES

    k_cur, k_prev, v_cur, v_prev = [], [], [], []
    for c in range(n_kchunk):
        sl = slice(c * LANES, (c + 1) * LANES)
        bk = b_ref[:, ATTN_WIDTH + c * LANES:ATTN_WIDTH + (c + 1) * LANES]
        bv = b_ref[:, ATTN_WIDTH + KV_WIDTH + c * LANES:ATTN_WIDTH + KV_WIDTH + (c + 1) * LANES]
        kc = _rope(kc_ref[:, sl] + bk, cos_c, sin_c)
        kp = _rope(kp_ref[:, sl] + bk, cos_p, sin_p)
        vc = vc_ref[:, sl] + bv
        vp = vp_ref[:, sl] + bv
        ko_ref[:, sl] = kc
        vo_ref[:, sl] = vc
        k_cur.append(kc)
        k_prev.append(kp)
        v_cur.append(vc)
        v_prev.append(vp)

    rows = ATTN_GROUP * w
    r = lax.broadcasted_iota(jnp.int32, (rows, 2 * w), 0) % w
    col = lax.broadcasted_iota(jnp.int32, (rows, 2 * w), 1)
    valid = (col > r) & (col <= r + w) & ((col >= w) | (i > 0))
    row_id = lax.broadcasted_iota(jnp.int32, (rows, 1), 0)
    lane_k = lax.broadcasted_iota(jnp.int32, (2 * w, LANES), 1)
    lane_o = lax.broadcasted_iota(jnp.int32, (w, LANES), 1)
    scale = ATTN_HEAD_DIM ** -0.5

    k2s, v2s, q4s = [], [], []
    for hk in range(ATTN_KV_HEADS):
        kchunk, khalf = hk // 2, hk % 2
        k2 = jnp.concatenate([k_prev[kchunk], k_cur[kchunk]], axis=0)
        in_half = (lane_k // ATTN_HEAD_DIM) == khalf
        k2s.append(jnp.where(in_half, k2, 0.0).astype(BF16))
        v2s.append(jnp.concatenate([v_prev[kchunk], v_cur[kchunk]], axis=0).astype(BF16))
        qs = []
        for g in range(ATTN_GROUP):
            hq = hk * ATTN_GROUP + g
            qchunk, qhalf = hq // 2, hq % 2
            sl = slice(qchunk * LANES, (qchunk + 1) * LANES)
            qc = _rope(q_ref[:, sl] + b_ref[:, sl], cos_c, sin_c) * scale
            if qhalf != khalf:
                qc = pltpu.roll(qc, ATTN_HEAD_DIM, 1)
            qs.append(qc)
        q4s.append(jnp.concatenate(qs, axis=0).astype(BF16))
    def softmax_with_sink(hk, s):
        s = jnp.where(valid, s, -jnp.inf)
        sink = jnp.zeros((rows, 1), F32)
        for g in range(ATTN_GROUP):
            sink = jnp.where(row_id // w == g, sink_ref[hk * ATTN_GROUP + g], sink)
        m = jnp.maximum(jnp.max(jnp.maximum(s[:, :w], s[:, w:]), axis=-1, keepdims=True), sink)
        e = jnp.exp(s - m)
        den = jnp.sum(e[:, :w] + e[:, w:], axis=-1, keepdims=True) + jnp.exp(sink - m)
        return (e / den).astype(BF16)

    pvs = []
    scores = _dot_nt(q4s[0], k2s[0])
    for hk in range(ATTN_KV_HEADS):
        nxt = _dot_nt(q4s[hk + 1], k2s[hk + 1]) if hk + 1 < ATTN_KV_HEADS else None
        pvs.append(_dot(softmax_with_sink(hk, scores), v2s[hk]))
        scores = nxt
    for hk, pv in enumerate(pvs):
        khalf = hk % 2
        outs = []
        for g in range(ATTN_GROUP):
            og = pv[g * w:(g + 1) * w, :]
            if (g % 2) != khalf:
                og = pltpu.roll(og, ATTN_HEAD_DIM, 1)
            outs.append(og)
        for j in range(ATTN_GROUP // 2):
            oc = jnp.where(lane_o < ATTN_HEAD_DIM, outs[2 * j], outs[2 * j + 1])
            c = hk * (ATTN_GROUP // 2) + j
            o_ref[:, c * LANES:(c + 1) * LANES] = oc.astype(o_ref.dtype)


def _attn_prompt(proj, b_attn, sinks, cos_t, sin_t):
    nb = SEQ // WINDOW
    kcol = ATTN_WIDTH // KV_WIDTH
    prev = lambda i: jnp.maximum(i - 1, 0)
    return pl.pallas_call(
        _attn_prompt_kernel,
        grid=(nb,),
        in_specs=[pl.BlockSpec(memory_space=pltpu.SMEM),
                  pl.BlockSpec((WINDOW, ATTN_WIDTH), lambda i: (i, 0)),
                  pl.BlockSpec((WINDOW, KV_WIDTH), lambda i: (i, kcol)),
                  pl.BlockSpec((WINDOW, KV_WIDTH), lambda i: (prev(i), kcol)),
                  pl.BlockSpec((WINDOW, KV_WIDTH), lambda i: (i, kcol + 1)),
                  pl.BlockSpec((WINDOW, KV_WIDTH), lambda i: (prev(i), kcol + 1)),
                  pl.BlockSpec((1, ATTN_QKV_WIDTH), lambda i: (0, 0)),
                  pl.BlockSpec((WINDOW, LANES), lambda i: (i, 0)),
                  pl.BlockSpec((WINDOW, LANES), lambda i: (i, 0)),
                  pl.BlockSpec((WINDOW, LANES), lambda i: (prev(i), 0)),
                  pl.BlockSpec((WINDOW, LANES), lambda i: (prev(i), 0))],
        out_specs=[pl.BlockSpec((WINDOW, ATTN_WIDTH), lambda i: (i, 0)),
                   pl.BlockSpec((WINDOW, KV_WIDTH), lambda i: (0, 0)),
                   pl.BlockSpec((WINDOW, KV_WIDTH), lambda i: (0, 0))],
        out_shape=[jax.ShapeDtypeStruct((SEQ, ATTN_WIDTH), BF16),
                   jax.ShapeDtypeStruct((WINDOW, KV_WIDTH), F32),
                   jax.ShapeDtypeStruct((WINDOW, KV_WIDTH), F32)],
        compiler_params=_cparams(("arbitrary",)),
        name="attn_prompt",
    )(sinks, proj, proj, proj, proj, proj, b_attn.reshape(1, ATTN_QKV_WIDTH),
      cos_t, sin_t, cos_t, sin_t)


ATTN_SB = 8


def _attn_sample_kernel(q_ref, k_ref, v_ref, bq_ref, bk_ref, bv_ref, cos_ref, sin_ref,
                        sink_ref, e_ref, et_ref, ck_ref, cv_ref, o_ref, ko_ref, vo_ref):
    cos, sin = cos_ref[...], sin_ref[...]
    scale = ATTN_HEAD_DIM ** -0.5
    qr = []
    for c in range(ATTN_WIDTH // LANES):
        sl = slice(c * LANES, (c + 1) * LANES)
        qr.append(_rope(q_ref[:, sl] + bq_ref[:, sl], cos, sin) * scale)
    q = jnp.concatenate(qr, axis=1)
    kn = jnp.concatenate(
        [_rope(k_ref[:, c * LANES:(c + 1) * LANES] + bk_ref[:, c * LANES:(c + 1) * LANES], cos, sin)
         for c in range(KV_WIDTH // LANES)], axis=1)
    vn = v_ref[...] + bv_ref[...]
    lb = WINDOW
    for b in range(ATTN_SB):
        ko_ref[b, 0:lb - 1, :] = ck_ref[b, 1:lb, :]
        ko_ref[b, lb - 1:lb, :] = kn[b:b + 1, :]
        vo_ref[b, 0:lb - 1, :] = cv_ref[b, 1:lb, :]
        vo_ref[b, lb - 1:lb, :] = vn[b:b + 1, :]
        kb = ko_ref[b]
        vb = vo_ref[b]
        prod = jnp.concatenate(
            [kb * q[b:b + 1, g * KV_WIDTH:(g + 1) * KV_WIDTH] for g in range(ATTN_GROUP)], axis=0)
        s = _dot(prod.astype(BF16), e_ref[...])
        ps = []
        for g in range(ATTN_GROUP):
            sg = s[g * lb:(g + 1) * lb, :]
            sink = sink_ref[g:g + 1, :]
            m = jnp.maximum(jnp.max(sg, axis=0, keepdims=True), sink)
            e = jnp.exp(sg - m)
            den = jnp.sum(e, axis=0, keepdims=True) + jnp.exp(sink - m)
            ps.append(e / den)
        pe = _dot(jnp.concatenate(ps, axis=0).astype(BF16), et_ref[...])
        for g in range(ATTN_GROUP):
            og = jnp.sum(pe[g * lb:(g + 1) * lb, :] * vb, axis=0, keepdims=True)
            o_ref[b:b + 1, g * KV_WIDTH:(g + 1) * KV_WIDTH] = og


def _attn_sample(q_perm, k_new, v_new, bq_perm, bk, bv, cos_row, sin_row, sinks_gk, cache_k, cache_v):
    n = DEC_BATCH
    sb = ATTN_SB
    head_of_lane = jnp.arange(KV_WIDTH) // ATTN_HEAD_DIM
    e_mat = (head_of_lane[:, None] == jnp.arange(LANES)[None, :]).astype(BF16)
    row = lambda w: pl.BlockSpec((sb, w), lambda i: (i, 0))
    vec = lambda w: pl.BlockSpec((1, w), lambda i: (0, 0))
    full = lambda a: pl.BlockSpec(a.shape, lambda i: (0,) * a.ndim)
    cache = pl.BlockSpec((sb, WINDOW, KV_WIDTH), lambda i: (i, 0, 0))
    return pl.pallas_call(
        _attn_sample_kernel,
        grid=(n // sb,),
        in_specs=[row(ATTN_WIDTH), row(KV_WIDTH), row(KV_WIDTH),
                  vec(ATTN_WIDTH), vec(KV_WIDTH), vec(KV_WIDTH), vec(LANES), vec(LANES),
                  full(sinks_gk), full(e_mat), full(e_mat.T), cache, cache],
        out_specs=[row(ATTN_WIDTH), cache, cache],
        out_shape=[jax.ShapeDtypeStruct((n, ATTN_WIDTH), F32),
                   jax.ShapeDtypeStruct(cache_k.shape, F32),
                   jax.ShapeDtypeStruct(cache_v.shape, F32)],
        compiler_params=_cparams(("parallel",)),
        name="attn_sample",
    )(q_perm, k_new, v_new, bq_perm, bk, bv, cos_row, sin_row, sinks_gk, e_mat, e_mat.T,
      cache_k, cache_v)


DN_ROWS = 512
DN_PAIR = 2 * DN_CHUNK
CONV_PAD = 8
DN_HPS = 4


def _dn_gates_kernel(ab_ref, alog_ref, dtb_ref, cum_ref, beta_ref):
    ab = ab_ref[...]
    g = -jnp.exp(alog_ref[...]) * _softplus(ab + dtb_ref[...])
    row = lax.broadcasted_iota(jnp.int32, ab.shape, 0) % DN_CHUNK
    shift = 1
    while shift < DN_CHUNK:
        g = g + jnp.where(row >= shift, pltpu.roll(g, shift, 0), 0.0)
        shift *= 2
    cum_ref[...] = g
    beta_ref[...] = pltpu.roll(_sigmoid(ab), LANES - DN_HEADS, 1)


def _dn_gates(proj_ab, alog_pad, dtb_pad):
    blk = pl.BlockSpec((DN_ROWS, LANES), lambda i: (i, 0))
    vec = pl.BlockSpec((1, LANES), lambda i: (0, 0))
    out = jax.ShapeDtypeStruct((SEQ, LANES), F32)
    return pl.pallas_call(
        _dn_gates_kernel,
        grid=(SEQ // DN_ROWS,),
        in_specs=[blk, vec, vec],
        out_specs=[blk, blk],
        out_shape=[out, out],
        compiler_params=_cparams(("parallel",)),
        name="dn_gates",
    )(proj_ab, alog_pad, dtb_pad)


def _dn_prompt_kernel(q_ref, qp_ref, k_ref, kp_ref, v_ref, vp_ref, z_ref, cum_ref, beta_ref,
                      wq_ref, wk_ref, wv_ref, nw_ref, o_ref, so_ref, xq_s, xk_s, xv_s, s_s):
    hp = pl.program_id(0)
    i = pl.program_id(1)
    pr = DN_PAIR
    cl = DN_CHUNK

    @pl.when(i == 0)
    def _():
        s_s[...] = jnp.zeros_like(s_s)

    for x_ref, prev_ref, pad_s in ((q_ref, qp_ref, xq_s), (k_ref, kp_ref, xk_s), (v_ref, vp_ref, xv_s)):
        pad_s[0:CONV_PAD, :] = jnp.where(i > 0, prev_ref[...], 0.0)
        pad_s[CONV_PAD:, :] = x_ref[...]

    def lanes_of(hh):
        return slice(hh * DN_DIM, (hh + 1) * DN_DIM)

    def conv(pad_s, w_ref, hh, r0):
        win = pad_s[r0:r0 + CONV_PAD + pr, lanes_of(hh)]
        acc = w_ref[0:1, lanes_of(hh)] * win
        for j in range(1, CONV_WIDTH):
            acc = pltpu.roll(acc, 1, 0) + w_ref[j:j + 1, lanes_of(hh)] * win
        return _silu(acc[CONV_PAD:])

    def l2n(x):
        return x * lax.rsqrt(jnp.sum(x * x, axis=-1, keepdims=True) + RMS_EPS)

    def gate(ref, hh, r0):
        pick = ci == hp * DN_HPS + hh
        col = jnp.sum(jnp.where(pick, ref[r0:r0 + pr, :], 0.0), axis=1, keepdims=True)
        return jnp.broadcast_to(col, (pr, pr))

    ri = lax.broadcasted_iota(jnp.int32, (pr, pr), 0)
    ci = lax.broadcasted_iota(jnp.int32, (pr, pr), 1)
    same_chunk = (ri // cl) == (ci // cl)
    causal = same_chunk & (ri >= ci)
    strict = same_chunk & (ri > ci)
    eye = (ri == ci).astype(F32)
    first_rows = ri < cl
    first_rows2 = lax.broadcasted_iota(jnp.int32, (pr, 2 * DN_DIM), 0) < cl
    nw = nw_ref[...]
    bf = lambda a: a.astype(BF16)
    cat0 = lambda *a: jnp.concatenate(a, axis=0)
    cat1 = lambda *a: jnp.concatenate(a, axis=1)

    n_pairs = DN_ROWS // pr
    probs = [(hh, a * pr) for a in range(n_pairs) for hh in range(DN_HPS)]

    q_p = [l2n(conv(xq_s, wq_ref, hh, r0)) * (DN_DIM ** -0.5) for hh, r0 in probs]
    k_p = [l2n(conv(xk_s, wk_ref, hh, r0)) for hh, r0 in probs]
    v_p = [conv(xv_s, wv_ref, hh, r0) for hh, r0 in probs]
    cum = [gate(cum_ref, hh, r0) for hh, r0 in probs]
    beta = [gate(beta_ref, hh, r0) for hh, r0 in probs]
    gamma = [jnp.exp(c) for c in cum]
    decay = [jnp.where(causal, jnp.exp(c - c.T), 0.0) for c in cum]
    kq = [_dot_nt(bf(cat0(k, q)), bf(k)) for k, q in zip(k_p, q_p)]
    x = [jnp.where(strict, -(b * m[:pr] * d), 0.0) for b, m, d in zip(beta, kq, decay)]
    att = [bf(m[pr:] * d) for m, d in zip(kq, decay)]
    t = [eye + m for m in x]
    xp = [_dot(bf(m), bf(m)) for m in x]
    for _ in range(int(math.log2(cl)) - 2):
        y = [_dot(bf(p), bf(cat1(p, m))) for p, m in zip(xp, t)]
        xp = [m[:, :pr] for m in y]
        t = [m + n[:, pr:] for m, n in zip(t, y)]
    t = [m + _dot(bf(p), bf(m)) for p, m in zip(xp, t)]
    wu = [_dot(bf(m), bf(cat1(k * (b * g), v * b)))
          for m, k, v, b, g in zip(t, k_p, v_p, beta, gamma)]
    awu = [_dot(a, bf(m)) for a, m in zip(att, wu)]
    q_prime = [bf(q * g - m[:, :DN_DIM]) for q, g, m in zip(q_p, gamma, awu)]
    au = [m[:, DN_DIM:] for m in awu]
    k_dec_t = [bf((k * jnp.exp(jnp.where(first_rows, c[cl - 1:cl, :], c[pr - 1:pr, :]) - c)).T)
               for k, c in zip(k_p, cum)]
    kwu1 = [_dot(kt, bf(jnp.where(first_rows2, m, 0.0))) for kt, m in zip(k_dec_t, wu)]
    kwu2 = [_dot(kt, bf(jnp.where(first_rows2, 0.0, m))) for kt, m in zip(k_dec_t, wu)]
    g1 = [jnp.exp(c[cl - 1:cl, :]) for c in cum]
    g2 = [jnp.exp(c[pr - 1:pr, :]) for c in cum]
    comp = [_dot(bf(-m2[:, :DN_DIM]), bf(cat1(-m1[:, :DN_DIM], m1[:, DN_DIM:]))) for m1, m2 in zip(kwu1, kwu2)]
    mp12 = [-(b * m1[:, :DN_DIM]) - a * m2[:, :DN_DIM] + c[:, :DN_DIM]
            for a, b, m1, m2, c in zip(g1, g2, kwu1, kwu2, comp)]
    n12 = [b * m1[:, DN_DIM:] + c[:, DN_DIM:] + m2[:, DN_DIM:] for b, m1, m2, c in zip(g2, kwu1, kwu2, comp)]
    lhs = [cat0(qp[:cl], bf(-m1[:, :DN_DIM]), bf(m)) for qp, m1, m in zip(q_prime, kwu1, mp12)]

    s = [s_s[hh] for hh in range(DN_HPS)]
    for a in range(n_pairs):
        js = [a * DN_HPS + hh for hh in range(DN_HPS)]
        r = [_dot(lhs[j], bf(s[hh])) for hh, j in enumerate(js)]
        s_mid = [g1[j] * s[hh] + r[hh][cl:cl + DN_DIM] + kwu1[j][:, DN_DIM:] for hh, j in enumerate(js)]
        s_new = [(g1[j] * g2[j]) * s[hh] + r[hh][cl + DN_DIM:] + n12[j] for hh, j in enumerate(js)]
        o2 = [_dot(q_prime[j][cl:], bf(s_mid[hh])) for hh, j in enumerate(js)]
        for hh, j in enumerate(js):
            o = cat0(r[hh][:cl], o2[hh]) + au[j]
            o = o * lax.rsqrt(jnp.mean(o * o, axis=-1, keepdims=True) + RMS_EPS) * nw
            r0 = a * pr
            o_ref[r0:r0 + pr, lanes_of(hh)] = (o * _silu(z_ref[r0:r0 + pr, lanes_of(hh)])).astype(o_ref.dtype)
        s = s_new

    for hh in range(DN_HPS):
        s_s[hh] = s[hh]

    @pl.when(i == pl.num_programs(1) - 1)
    def _():
        for hh in range(DN_HPS):
            so_ref[hh] = s[hh]


def _dn_prompt(proj, cum, beta, w_conv, norm_w):
    rows = DN_ROWS
    width = DN_HPS * DN_DIM
    q0 = ATTN_QKV_WIDTH // width
    k0 = q0 + DN_WIDTH // width
    v0 = k0 + DN_WIDTH // width
    z0 = v0 + DN_WIDTH // width
    per = rows // CONV_PAD

    def blk(c0):
        return pl.BlockSpec((rows, width), lambda h, i: (i, c0 + h))

    def prev(c0):
        return pl.BlockSpec((CONV_PAD, width), lambda h, i: (jnp.maximum(i * per - 1, 0), c0 + h))

    def wblk(c0):
        return pl.BlockSpec((CONV_WIDTH, width), lambda h, i: (0, c0 + h))

    gate = pl.BlockSpec((rows, LANES), lambda h, i: (i, 0))
    vec = pl.BlockSpec((1, DN_DIM), lambda h, i: (0, 0))
    pad = pltpu.VMEM((rows + CONV_PAD, width), F32)
    return pl.pallas_call(
        _dn_prompt_kernel,
        grid=(DN_HEADS // DN_HPS, SEQ // rows),
        in_specs=[blk(q0), prev(q0), blk(k0), prev(k0), blk(v0), prev(v0), blk(z0), gate, gate,
                  wblk(0), wblk(DN_WIDTH // width), wblk(2 * DN_WIDTH // width), vec],
        out_specs=[pl.BlockSpec((rows, width), lambda h, i: (i, h)),
                   pl.BlockSpec((DN_HPS, DN_DIM, DN_DIM), lambda h, i: (h, 0, 0))],
        out_shape=[jax.ShapeDtypeStruct((SEQ, DN_WIDTH), BF16),
                   jax.ShapeDtypeStruct((DN_HEADS, DN_DIM, DN_DIM), F32)],
        scratch_shapes=[pad, pad, pad, pltpu.VMEM((DN_HPS, DN_DIM, DN_DIM), F32)],
        compiler_params=_cparams(("parallel", "arbitrary")),
        name="dn_prompt",
    )(proj, proj, proj, proj, proj, proj, proj, cum, beta, w_conv, w_conv, w_conv,
      norm_w.reshape(1, DN_DIM))


DN_SB = 8


def _dn_gates_of_head(ab, alog, dtb, h):
    lane = lax.broadcasted_iota(jnp.int32, ab.shape, 1)
    g_all = -jnp.exp(alog) * _softplus(ab + dtb)
    g = jnp.sum(jnp.where(lane == h, g_all, 0.0), axis=1, keepdims=True)
    beta = jnp.sum(jnp.where(lane == h + DN_HEADS, _sigmoid(ab), 0.0), axis=1, keepdims=True)
    return jnp.broadcast_to(g, ab.shape), jnp.broadcast_to(beta, ab.shape)


def _dn_sample_prep_kernel(x_ref, sc_ref, ab_ref, w_ref, alog_ref, dtb_ref,
                           qt_ref, kt_ref, vb_ref, bg_ref, gam_ref):
    ab = ab_ref[...]
    alog, dtb = alog_ref[...], dtb_ref[...]

    def conv(col):
        sl = slice(col, col + LANES)
        y = w_ref[CONV_WIDTH - 1:CONV_WIDTH, sl] * x_ref[:, sl]
        for j in range(CONV_WIDTH - 1):
            y = y + w_ref[j:j + 1, sl] * sc_ref[:, j * CONV_DIM + col:j * CONV_DIM + col + LANES]
        return _silu(y)

    def l2n(x):
        return x * lax.rsqrt(jnp.sum(x * x, axis=-1, keepdims=True) + RMS_EPS)

    for h in range(DN_HEADS):
        q = l2n(conv(h * DN_DIM)) * (DN_DIM ** -0.5)
        k = l2n(conv(DN_WIDTH + h * DN_DIM))
        v = conv(2 * DN_WIDTH + h * DN_DIM)
        g_b, beta_b = _dn_gates_of_head(ab, alog, dtb, h)
        gamma = jnp.exp(g_b)
        qt_ref[h] = q.T
        kt_ref[h] = k.T
        vb_ref[h] = v * beta_b
        bg_ref[h] = beta_b * gamma
        gam_ref[h] = gamma


def _dn_sample_prep(qkv_s, conv_state, ab_s, w_conv, alog_pad, dtb_pad):
    n = DEC_BATCH
    full = lambda a: pl.BlockSpec(a.shape, lambda i: (0,) * a.ndim)
    hm = jax.ShapeDtypeStruct((DN_HEADS, n, DN_DIM), F32)
    hm_spec = pl.BlockSpec((DN_HEADS, n, DN_DIM), lambda i: (0, 0, 0))
    args = (qkv_s, conv_state, ab_s, w_conv, alog_pad, dtb_pad)
    return pl.pallas_call(
        _dn_sample_prep_kernel,
        grid=(1,),
        in_specs=[full(a) for a in args],
        out_specs=[hm_spec] * 5,
        out_shape=[hm] * 5,
        compiler_params=_cparams(("arbitrary",)),
        name="dn_sample_prep",
    )(*args)


def _dn_sample_kernel(s_ref, qt_ref, kt_ref, vb_ref, bg_ref, gam_ref, z_ref, nw_ref,
                      so_ref, o_ref, o_s):
    step = pl.program_id(0)
    lane = lax.broadcasted_iota(jnp.int32, (DN_DIM, DEC_BATCH), 1)

    def head(h, carry):
        kt = kt_ref[h]
        qt = qt_ref[h]
        for bb in range(DN_SB):
            b = step * DN_SB + bb
            pick = lane == b
            kcol = jnp.sum(jnp.where(pick, kt, 0.0), axis=1, keepdims=True)
            qcol = jnp.sum(jnp.where(pick, qt, 0.0), axis=1, keepdims=True)
            s = s_ref[bb, h]
            ks = jnp.sum(kcol * s, axis=0, keepdims=True)
            u = vb_ref[h, pl.ds(b, 1), :] - bg_ref[h, pl.ds(b, 1), :] * ks
            s_new = gam_ref[h, pl.ds(b, 1), :] * s + kcol * u
            so_ref[bb, h] = s_new
            o_s[h, pl.ds(bb, 1), :] = jnp.sum(qcol * s_new, axis=0, keepdims=True)
        return carry

    lax.fori_loop(0, DN_HEADS, head, 0)
    nw = nw_ref[...]
    for h in range(DN_HEADS):
        o = o_s[h]
        o = o * lax.rsqrt(jnp.mean(o * o, axis=-1, keepdims=True) + RMS_EPS) * nw
        sl = slice(h * DN_DIM, (h + 1) * DN_DIM)
        o_ref[:, sl] = o * _silu(z_ref[:, sl])


def _dn_sample(state, qt, kt, vb, bg, gam, z_s, norm_w):
    n = DEC_BATCH
    sb = DN_SB
    sblk = pl.BlockSpec((sb, DN_HEADS, DN_DIM, DN_DIM), lambda i: (i, 0, 0, 0))
    hm = pl.BlockSpec((DN_HEADS, n, DN_DIM), lambda i: (0, 0, 0))
    row = pl.BlockSpec((sb, DN_WIDTH), lambda i: (i, 0))
    return pl.pallas_call(
        _dn_sample_kernel,
        grid=(n // sb,),
        in_specs=[sblk, hm, hm, hm, hm, hm, row, pl.BlockSpec((1, DN_DIM), lambda i: (0, 0))],
        out_specs=[sblk, row],
        out_shape=[jax.ShapeDtypeStruct(state.shape, F32), jax.ShapeDtypeStruct((n, DN_WIDTH), F32)],
        scratch_shapes=[pltpu.VMEM((DN_HEADS, sb, DN_DIM), F32)],
        compiler_params=_cparams(("parallel",)),
        name="dn_sample",
    )(state, qt, kt, vb, bg, gam, z_s, norm_w.reshape(1, DN_DIM))


BM = 1040
BM_IN = 2080
BN_IN = 512
BM_OUT = 1024
BN_OUT = 1024
BF_FFN = 256


def _rope_tables(pos):
    half = ATTN_HEAD_DIM // 2
    inv_freq = ROPE_THETA ** (-jnp.arange(half, dtype=F32) / half)
    ang = pos.astype(F32)[:, None] * inv_freq
    cos, sin = jnp.cos(ang), jnp.sin(ang)
    reps = LANES // ATTN_HEAD_DIM
    return jnp.tile(cos, (1, 2 * reps)), jnp.tile(jnp.concatenate([-sin, sin], axis=1), (1, reps))


def _group_major(a):
    lead = a.shape[:-1]
    a = a.reshape(lead + (ATTN_KV_HEADS, ATTN_GROUP, ATTN_HEAD_DIM))
    return jnp.swapaxes(a, -3, -2).reshape(lead + (ATTN_WIDTH,))


def _layer(x_prompt, x_sample, cache_k, cache_v, state_conv, state_delta, w_in, b_attn, attn_sinks,
           w_conv, dn_a_log, dn_dt_bias, dn_norm_w, w_out, ln1_g, ln1_b, w_gate, w_up, w_down,
           ln2_g, ln2_b):
    n_s = DEC_BATCH
    xb = _xcast(x_prompt, x_sample)
    w_in_t = w_in.T
    w_ab = jnp.pad(w_in_t[MAIN_WIDTH:], ((0, LANES - 2 * DN_HEADS), (0, 0)))
    proj, proj_ab = _matmul_nt(xb, w_in_t, w_ab, BM_IN, BN_IN, MAIN_WIDTH // BN_IN)

    pad16 = lambda v: jnp.pad(v, (0, LANES - DN_HEADS)).reshape(1, LANES)
    alog_pad, dtb_pad = pad16(dn_a_log), pad16(dn_dt_bias)

    cos_t, sin_t = _rope_tables(jnp.arange(SEQ, dtype=jnp.int32))
    attn_p, pk, pv = _attn_prompt(proj, b_attn, attn_sinks, cos_t, sin_t)
    cum, beta = _dn_gates(proj_ab, alog_pad, dtb_pad)
    dn_p, ps = _dn_prompt(proj, cum, beta, w_conv, dn_norm_w)
    pc = proj[SEQ - (CONV_WIDTH - 1):SEQ, ATTN_QKV_WIDTH:ATTN_QKV_WIDTH + CONV_DIM]

    proj_s = proj[SEQ:]
    cos_r, sin_r = _rope_tables(jnp.full((1,), PAST_LEN, jnp.int32))
    sinks_gk = jnp.pad(attn_sinks.reshape(ATTN_KV_HEADS, ATTN_GROUP).T,
                       ((0, 0), (0, LANES - ATTN_KV_HEADS)))
    attn_s_perm, sk, sv = _attn_sample(
        _group_major(proj_s[:, :ATTN_WIDTH]),
        proj_s[:, ATTN_WIDTH:ATTN_WIDTH + KV_WIDTH],
        proj_s[:, ATTN_WIDTH + KV_WIDTH:ATTN_QKV_WIDTH],
        _group_major(b_attn[:ATTN_WIDTH]).reshape(1, ATTN_WIDTH),
        b_attn[ATTN_WIDTH:ATTN_WIDTH + KV_WIDTH].reshape(1, KV_WIDTH),
        b_attn[ATTN_WIDTH + KV_WIDTH:].reshape(1, KV_WIDTH),
        cos_r, sin_r, sinks_gk, cache_k.reshape(n_s, WINDOW, KV_WIDTH),
        cache_v.reshape(n_s, WINDOW, KV_WIDTH))
    attn_s = jnp.swapaxes(attn_s_perm.reshape(n_s, ATTN_GROUP, ATTN_KV_HEADS, ATTN_HEAD_DIM), 1, 2)
    attn_s = attn_s.reshape(n_s, ATTN_WIDTH)

    qkv_s = proj_s[:, ATTN_QKV_WIDTH:ATTN_QKV_WIDTH + CONV_DIM]
    z_s = proj_s[:, ATTN_QKV_WIDTH + CONV_DIM:]
    qt, kt, vb, bg, gam = _dn_sample_prep(
        qkv_s, state_conv.reshape(n_s, (CONV_WIDTH - 1) * CONV_DIM), proj_ab[SEQ:],
        w_conv, alog_pad, dtb_pad)
    ss, dn_s = _dn_sample(state_delta, qt, kt, vb, bg, gam, z_s, dn_norm_w)
    sc = jnp.concatenate([state_conv[:, 1:], qkv_s[:, None, :]], axis=1)

    mixed_p, mixed_s = _out_proj(attn_p, dn_p, attn_s.astype(BF16), dn_s.astype(BF16), w_out,
                                 BM_OUT, BN_OUT)
    h32, hb = _ln1(x_prompt, x_sample, mixed_p, mixed_s, ln1_g, ln1_b)
    ffn = _ffn(hb, w_gate, w_up, w_down, BM, BF_FFN)
    y_p, y_s = _ln2(h32, ffn, ln2_g, ln2_b)
    return (y_p, y_s, pk.reshape(WINDOW, ATTN_KV_HEADS, ATTN_HEAD_DIM),
            pv.reshape(WINDOW, ATTN_KV_HEADS, ATTN_HEAD_DIM), pc, ps,
            sk.reshape(n_s, WINDOW, ATTN_KV_HEADS, ATTN_HEAD_DIM),
            sv.reshape(n_s, WINDOW, ATTN_KV_HEADS, ATTN_HEAD_DIM), sc, ss)


def kernel(x_prompt, x_sample, cache_swa_k, cache_swa_v, state_conv, state_delta, w_in, b_attn,
           attn_sinks, w_conv, dn_a_log, dn_dt_bias, dn_norm_w, w_out, ln1_g, ln1_b, w_gate, w_up,
           w_down, ln2_g, ln2_b):
    assert x_prompt.shape == (1, SEQ, D_MODEL) and x_sample.shape == (DEC_BATCH, 1, D_MODEL)
    assert w_in.shape[0] == 1, "one layer"
    y_p, y_s, pk, pv, pc, ps, sk, sv, sc, ss = _layer(
        x_prompt[0], x_sample[:, 0], cache_swa_k[0], cache_swa_v[0], state_conv[0], state_delta[0],
        w_in[0], b_attn[0], attn_sinks[0], w_conv[0], dn_a_log[0], dn_dt_bias[0], dn_norm_w[0],
        w_out[0], ln1_g[0], ln1_b[0], w_gate[0], w_up[0], w_down[0], ln2_g[0], ln2_b[0])
    return (y_p[None], y_s[:, None], pk[None, None], pv[None, None], pc[None, None],
            ps[None, None], sk[None], sv[None], sc[None], ss[None])
```

```python
import math

import jax
import jax.numpy as jnp
from jax import lax
from jax.experimental import pallas as pl
from jax.experimental.pallas import tpu as pltpu

D_MODEL = 4096
SEQ = 8192
DEC_BATCH = 128
PAST_LEN = 8192
ROWS = SEQ + DEC_BATCH

ATTN_HEADS = 32
ATTN_KV_HEADS = 8
ATTN_HEAD_DIM = 64
ATTN_GROUP = ATTN_HEADS // ATTN_KV_HEADS
ATTN_WIDTH = ATTN_HEADS * ATTN_HEAD_DIM
KV_WIDTH = ATTN_KV_HEADS * ATTN_HEAD_DIM
WINDOW = 128
ROPE_THETA = 10000.0
DN_HEADS = 16
DN_DIM = 128
DN_WIDTH = DN_HEADS * DN_DIM
CONV_WIDTH = 4
CONV_DIM = 3 * DN_WIDTH
DN_CHUNK = 64
ATTN_QKV_WIDTH = ATTN_WIDTH + 2 * KV_WIDTH
MAIN_WIDTH = ATTN_QKV_WIDTH + CONV_DIM + DN_WIDTH
FFN_HIDDEN = 11008
DEEPNORM_ALPHA = 2.0 ** 0.25
LN_EPS = 1e-5
RMS_EPS = 1e-6

LANES = 128
VMEM_LIMIT = 56 * 1024 * 1024
VMEM_LIMIT_BIG = 61 * 1024 * 1024

F32 = jnp.float32
BF16 = jnp.bfloat16


def _cparams(sem, vmem=VMEM_LIMIT):
    return pltpu.CompilerParams(dimension_semantics=sem, vmem_limit_bytes=vmem)


def _dot(a, b):
    return jnp.dot(a, b, preferred_element_type=F32)


def _dot_nt(a, b):
    return lax.dot_general(a, b, (((1,), (1,)), ((), ())), preferred_element_type=F32)


def _sigmoid(x):
    return 0.5 * jnp.tanh(0.5 * x) + 0.5


def _silu(x):
    return x * _sigmoid(x)


def _softplus(x):
    return jnp.maximum(x, 0.0) + jnp.log(1.0 + jnp.exp(-jnp.abs(x)))


def _rope(x, cos, sin_signed):
    lane = lax.broadcasted_iota(jnp.int32, x.shape, 1)
    first_half = (lane % ATTN_HEAD_DIM) < (ATTN_HEAD_DIM // 2)
    partner = jnp.where(first_half, pltpu.roll(x, LANES - ATTN_HEAD_DIM // 2, 1),
                        pltpu.roll(x, ATTN_HEAD_DIM // 2, 1))
    return x * cos + partner * sin_signed


def _mm_nt_kernel(x_ref, wt_ref, wt_tail_ref, o_ref, o_tail_ref):
    o_ref[...] = _dot_nt(x_ref[...], wt_ref[...].astype(BF16))

    @pl.when(pl.program_id(1) == 0)
    def _():
        o_tail_ref[...] = _dot_nt(x_ref[...], wt_tail_ref[...].astype(BF16))


def _matmul_nt(x, wt, wt_tail, bm, bn, nb):
    m, k = x.shape
    nt = wt_tail.shape[0]
    return pl.pallas_call(
        _mm_nt_kernel,
        grid=(m // bm, nb),
        in_specs=[pl.BlockSpec((bm, k), lambda i, j: (i, 0), pipeline_mode=pl.Buffered(1)),
                  pl.BlockSpec((bn, k), lambda i, j: (j, 0)),
                  pl.BlockSpec((nt, k), lambda i, j: (0, 0))],
        out_specs=[pl.BlockSpec((bm, bn), lambda i, j: (i, j)),
                   pl.BlockSpec((bm, nt), lambda i, j: (i, 0))],
        out_shape=[jax.ShapeDtypeStruct((m, nb * bn), F32), jax.ShapeDtypeStruct((m, nt), F32)],
        compiler_params=_cparams(("parallel", "arbitrary")),
        name="matmul",
    )(x, wt, wt_tail)


def _out_proj_kernel(ap_ref, dp_ref, as_ref, ds_ref, w_ref, op_ref, os_ref, wb_s):
    i = pl.program_id(1)
    last = pl.num_programs(1) - 1
    ka = ap_ref.shape[1]

    @pl.when(i == 0)
    def _():
        wb_s[...] = w_ref[...].astype(BF16)

    @pl.when(i < last)
    def _():
        op_ref[...] = _dot(ap_ref[...], wb_s[:ka, :]) + _dot(dp_ref[...], wb_s[ka:, :])

    @pl.when(i == last)
    def _():
        os_ref[...] = _dot(as_ref[...], wb_s[:ka, :]) + _dot(ds_ref[...], wb_s[ka:, :])


def _out_proj(a_p, d_p, a_s, d_s, w, bm, bn):
    m, ka = a_p.shape
    kd = d_p.shape[1]
    ms = a_s.shape[0]
    n = w.shape[1]
    nt = m // bm
    tile = lambda width: pl.BlockSpec((bm, width), lambda j, i: (jnp.minimum(i, nt - 1), 0))
    dec = lambda width: pl.BlockSpec((ms, width), lambda j, i: (0, 0))
    return pl.pallas_call(
        _out_proj_kernel,
        grid=(n // bn, nt + 1),
        in_specs=[tile(ka), tile(kd), dec(ka), dec(kd),
                  pl.BlockSpec((ka + kd, bn), lambda j, i: (0, j), pipeline_mode=pl.Buffered(1))],
        out_specs=[pl.BlockSpec((bm, bn), lambda j, i: (jnp.minimum(i, nt - 1), j)),
                   pl.BlockSpec((ms, bn), lambda j, i: (0, j))],
        out_shape=[jax.ShapeDtypeStruct((m, n), F32), jax.ShapeDtypeStruct((ms, n), F32)],
        scratch_shapes=[pltpu.VMEM((ka + kd, bn), BF16)],
        compiler_params=_cparams(("parallel", "arbitrary"), VMEM_LIMIT_BIG),
        name="out_proj",
    )(a_p, d_p, a_s, d_s, w)


FFN_NCHUNK = 1024
FFN_ROW_ALIGN = 16


def _ffn_kernel(h_ref, wg_ref, wu_ref, wd_ref, o_ref):
    f = pl.program_id(1)

    @pl.when(f == 0)
    def _():
        o_ref[...] = jnp.zeros_like(o_ref)

    bm = h_ref.shape[0]
    cut = (bm // 2) // FFN_ROW_ALIGN * FFN_ROW_ALIGN
    groups = ((0, cut), (cut, bm))
    wg = wg_ref[...].astype(BF16)
    wu = wu_ref[...].astype(BF16)
    gates = [_dot(h_ref[a:b, :], wg) for a, b in groups]
    ups = [_dot(h_ref[a:b, :], wu) for a, b in groups]
    acts = [(_silu(g) * u).astype(BF16) for g, u in zip(gates, ups)]
    for n in range(0, o_ref.shape[1], FFN_NCHUNK):
        wd = wd_ref[:, n:n + FFN_NCHUNK].astype(BF16)
        for (a, b), act in zip(groups, acts):
            o_ref[a:b, n:n + FFN_NCHUNK] += _dot(act, wd)


def _ffn(h, w_gate, w_up, w_down, bm, bf):
    m, d = h.shape
    hidden = w_gate.shape[1]
    once = pl.Buffered(1)
    return pl.pallas_call(
        _ffn_kernel,
        grid=(m // bm, hidden // bf),
        in_specs=[pl.BlockSpec((bm, d), lambda i, f: (i, 0), pipeline_mode=once),
                  pl.BlockSpec((d, bf), lambda i, f: (0, f)),
                  pl.BlockSpec((d, bf), lambda i, f: (0, f)),
                  pl.BlockSpec((bf, d), lambda i, f: (f, 0))],
        out_specs=pl.BlockSpec((bm, d), lambda i, f: (i, 0), pipeline_mode=once),
        out_shape=jax.ShapeDtypeStruct((m, d), F32),
        compiler_params=_cparams(("parallel", "arbitrary"), VMEM_LIMIT_BIG),
        name="ffn",
    )(h, w_gate, w_up, w_down)


BR = WINDOW
NB_PROMPT = SEQ // BR


def _deepnorm(x, mixed, g, b):
    v = DEEPNORM_ALPHA * x + mixed
    mu = jnp.mean(v, axis=-1, keepdims=True)
    c = v - mu
    var = jnp.mean(c * c, axis=-1, keepdims=True)
    return c * lax.rsqrt(var + LN_EPS) * g + b


def _prompt_rows(width):
    return pl.BlockSpec((BR, width), lambda i: (jnp.minimum(i, NB_PROMPT - 1), 0))


def _decode_rows(width):
    return pl.BlockSpec((BR, width), lambda i: (0, 0))


def _xcast_kernel(xp_ref, xs_ref, o_ref):
    i = pl.program_id(0)
    o_ref[...] = jnp.where(i < NB_PROMPT, xp_ref[...], xs_ref[...]).astype(o_ref.dtype)


def _xcast(x_prompt, x_sample):
    d = x_prompt.shape[1]
    return pl.pallas_call(
        _xcast_kernel,
        grid=(NB_PROMPT + 1,),
        in_specs=[_prompt_rows(d), _decode_rows(d)],
        out_specs=pl.BlockSpec((BR, d), lambda i: (i, 0)),
        out_shape=jax.ShapeDtypeStruct((ROWS, d), BF16),
        compiler_params=_cparams(("arbitrary",)),
        name="xcast",
    )(x_prompt, x_sample)


def _ln1_kernel(xp_ref, xs_ref, mp_ref, ms_ref, g_ref, b_ref, o_ref, ob_ref):
    i = pl.program_id(0)
    is_prompt = i < NB_PROMPT
    x = jnp.where(is_prompt, xp_ref[...], xs_ref[...])
    mixed = jnp.where(is_prompt, mp_ref[...], ms_ref[...])
    y = _deepnorm(x, mixed, g_ref[...], b_ref[...])
    o_ref[...] = y
    ob_ref[...] = y.astype(BF16)


def _ln1(x_prompt, x_sample, mixed_p, mixed_s, g, b):
    d = mixed_p.shape[1]
    row = pl.BlockSpec((BR, d), lambda i: (i, 0))
    vec = pl.BlockSpec((1, d), lambda i: (0, 0))
    return pl.pallas_call(
        _ln1_kernel,
        grid=(NB_PROMPT + 1,),
        in_specs=[_prompt_rows(d), _decode_rows(d), _prompt_rows(d), _decode_rows(d), vec, vec],
        out_specs=[row, row],
        out_shape=[jax.ShapeDtypeStruct((ROWS, d), F32), jax.ShapeDtypeStruct((ROWS, d), BF16)],
        compiler_params=_cparams(("arbitrary",)),
        name="ln1",
    )(x_prompt, x_sample, mixed_p, mixed_s, g.reshape(1, d), b.reshape(1, d))


def _ln2_kernel(h_ref, f_ref, g_ref, b_ref, yp_ref, ys_ref):
    i = pl.program_id(0)
    y = _deepnorm(h_ref[...], f_ref[...], g_ref[...], b_ref[...])

    @pl.when(i < NB_PROMPT)
    def _():
        yp_ref[...] = y

    @pl.when(i >= NB_PROMPT)
    def _():
        ys_ref[...] = y


def _ln2(h32, ffn, g, b):
    d = h32.shape[1]
    row = pl.BlockSpec((BR, d), lambda i: (i, 0))
    vec = pl.BlockSpec((1, d), lambda i: (0, 0))
    return pl.pallas_call(
        _ln2_kernel,
        grid=(NB_PROMPT + 1,),
        in_specs=[row, row, vec, vec],
        out_specs=[_prompt_rows(d), _decode_rows(d)],
        out_shape=[jax.ShapeDtypeStruct((SEQ, d), F32), jax.ShapeDtypeStruct((DEC_BATCH, d), F32)],
        compiler_params=_cparams(("arbitrary",)),
        name="ln2",
    )(h32, ffn, g.reshape(1, d), b.reshape(1, d))


def _attn_prompt_kernel(sink_ref, q_ref, kc_ref, kp_ref, vc_ref, vp_ref, b_ref,
                        cc_ref, sc_ref, cp_ref, sp_ref, o_ref, ko_ref, vo_ref):
    i = pl.program_id(0)
    w = WINDOW
    cos_c, sin_c = cc_ref[...], sc_ref[...]
    cos_p, sin_p = cp_ref[...], sp_ref[...]
    n_kchunk = KV_WIDTH // LANES

    k_cur, k_prev, v_cur, v_prev = [], [], [], []
    for c in range(n_kchunk):
        sl = slice(c * LANES, (c + 1) * LANES)
        bk = b_ref[:, ATTN_WIDTH + c * LANES:ATTN_WIDTH + (c + 1) * LANES]
        bv = b_ref[:, ATTN_WIDTH + KV_WIDTH + c * LANES:ATTN_WIDTH + KV_WIDTH + (c + 1) * LANES]
        kc = _rope(kc_ref[:, sl] + bk, cos_c, sin_c)
        kp = _rope(kp_ref[:, sl] + bk, cos_p, sin_p)
        vc = vc_ref[:, sl] + bv
        vp = vp_ref[:, sl] + bv
        ko_ref[:, sl] = kc
        vo_ref[:, sl] = vc
        k_cur.append(kc)
        k_prev.append(kp)
        v_cur.append(vc)
        v_prev.append(vp)

    rows = ATTN_GROUP * w
    r = lax.broadcasted_iota(jnp.int32, (rows, 2 * w), 0) % w
    col = lax.broadcasted_iota(jnp.int32, (rows, 2 * w), 1)
    valid = (col > r) & (col <= r + w) & ((col >= w) | (i > 0))
    row_id = lax.broadcasted_iota(jnp.int32, (rows, 1), 0)
    lane_k = lax.broadcasted_iota(jnp.int32, (2 * w, LANES), 1)
    lane_o = lax.broadcasted_iota(jnp.int32, (w, LANES), 1)
    scale = ATTN_HEAD_DIM ** -0.5

    k2s, v2s, q4s = [], [], []
    for hk in range(ATTN_KV_HEADS):
        kchunk, khalf = hk // 2, hk % 2
        k2 = jnp.concatenate([k_prev[kchunk], k_cur[kchunk]], axis=0)
        in_half = (lane_k // ATTN_HEAD_DIM) == khalf
        k2s.append(jnp.where(in_half, k2, 0.0).astype(BF16))
        v2s.append(jnp.concatenate([v_prev[kchunk], v_cur[kchunk]], axis=0).astype(BF16))
        qs = []
        for g in range(ATTN_GROUP):
            hq = hk * ATTN_GROUP + g
            qchunk, qhalf = hq // 2, hq % 2
            sl = slice(qchunk * LANES, (qchunk + 1) * LANES)
            qc = _rope(q_ref[:, sl] + b_ref[:, sl], cos_c, sin_c) * scale
            if qhalf != khalf:
                qc = pltpu.roll(qc, ATTN_HEAD_DIM, 1)
            qs.append(qc)
        q4s.append(jnp.concatenate(qs, axis=0).astype(BF16))
    def softmax_with_sink(hk, s):
        s = jnp.where(valid, s, -jnp.inf)
        sink = jnp.zeros((rows, 1), F32)
        for g in range(ATTN_GROUP):
            sink = jnp.where(row_id // w == g, sink_ref[hk * ATTN_GROUP + g], sink)
        m = jnp.maximum(jnp.max(jnp.maximum(s[:, :w], s[:, w:]), axis=-1, keepdims=True), sink)
        e = jnp.exp(s - m)
        den = jnp.sum(e[:, :w] + e[:, w:], axis=-1, keepdims=True) + jnp.exp(sink - m)
        return (e / den).astype(BF16)

    pvs = []
    scores = _dot_nt(q4s[0], k2s[0])
    for hk in range(ATTN_KV_HEADS):
        nxt = _dot_nt(q4s[hk + 1], k2s[hk + 1]) if hk + 1 < ATTN_KV_HEADS else None
        pvs.append(_dot(softmax_with_sink(hk, scores), v2s[hk]))
        scores = nxt
    for hk, pv in enumerate(pvs):
        khalf = hk % 2
        outs = []
        for g in range(ATTN_GROUP):
            og = pv[g * w:(g + 1) * w, :]
            if (g % 2) != khalf:
                og = pltpu.roll(og, ATTN_HEAD_DIM, 1)
            outs.append(og)
        for j in range(ATTN_GROUP // 2):
            oc = jnp.where(lane_o < ATTN_HEAD_DIM, outs[2 * j], outs[2 * j + 1])
            c = hk * (ATTN_GROUP // 2) + j
            o_ref[:, c * LANES:(c + 1) * LANES] = oc.astype(o_ref.dtype)


def _attn_prompt(proj, b_attn, sinks, cos_t, sin_t):
    nb = SEQ // WINDOW
    kcol = ATTN_WIDTH // KV_WIDTH
    prev = lambda i: jnp.maximum(i - 1, 0)
    return pl.pallas_call(
        _attn_prompt_kernel,
        grid=(nb,),
        in_specs=[pl.BlockSpec(memory_space=pltpu.SMEM),
                  pl.BlockSpec((WINDOW, ATTN_WIDTH), lambda i: (i, 0)),
                  pl.BlockSpec((WINDOW, KV_WIDTH), lambda i: (i, kcol)),
                  pl.BlockSpec((WINDOW, KV_WIDTH), lambda i: (prev(i), kcol)),
                  pl.BlockSpec((WINDOW, KV_WIDTH), lambda i: (i, kcol + 1)),
                  pl.BlockSpec((WINDOW, KV_WIDTH), lambda i: (prev(i), kcol + 1)),
                  pl.BlockSpec((1, ATTN_QKV_WIDTH), lambda i: (0, 0)),
                  pl.BlockSpec((WINDOW, LANES), lambda i: (i, 0)),
                  pl.BlockSpec((WINDOW, LANES), lambda i: (i, 0)),
                  pl.BlockSpec((WINDOW, LANES), lambda i: (prev(i), 0)),
                  pl.BlockSpec((WINDOW, LANES), lambda i: (prev(i), 0))],
        out_specs=[pl.BlockSpec((WINDOW, ATTN_WIDTH), lambda i: (i, 0)),
                   pl.BlockSpec((WINDOW, KV_WIDTH), lambda i: (0, 0)),
                   pl.BlockSpec((WINDOW, KV_WIDTH), lambda i: (0, 0))],
        out_shape=[jax.ShapeDtypeStruct((SEQ, ATTN_WIDTH), BF16),
                   jax.ShapeDtypeStruct((WINDOW, KV_WIDTH), F32),
                   jax.ShapeDtypeStruct((WINDOW, KV_WIDTH), F32)],
        compiler_params=_cparams(("arbitrary",)),
        name="attn_prompt",
    )(sinks, proj, proj, proj, proj, proj, b_attn.reshape(1, ATTN_QKV_WIDTH),
      cos_t, sin_t, cos_t, sin_t)


ATTN_SB = 8


def _attn_sample_kernel(q_ref, k_ref, v_ref, bq_ref, bk_ref, bv_ref, cos_ref, sin_ref,
                        sink_ref, e_ref, et_ref, ck_ref, cv_ref, o_ref, ko_ref, vo_ref):
    cos, sin = cos_ref[...], sin_ref[...]
    scale = ATTN_HEAD_DIM ** -0.5
    qr = []
    for c in range(ATTN_WIDTH // LANES):
        sl = slice(c * LANES, (c + 1) * LANES)
        qr.append(_rope(q_ref[:, sl] + bq_ref[:, sl], cos, sin) * scale)
    q = jnp.concatenate(qr, axis=1)
    kn = jnp.concatenate(
        [_rope(k_ref[:, c * LANES:(c + 1) * LANES] + bk_ref[:, c * LANES:(c + 1) * LANES], cos, sin)
         for c in range(KV_WIDTH // LANES)], axis=1)
    vn = v_ref[...] + bv_ref[...]
    lb = WINDOW
    for b in range(ATTN_SB):
        ko_ref[b, 0:lb - 1, :] = ck_ref[b, 1:lb, :]
        ko_ref[b, lb - 1:lb, :] = kn[b:b + 1, :]
        vo_ref[b, 0:lb - 1, :] = cv_ref[b, 1:lb, :]
        vo_ref[b, lb - 1:lb, :] = vn[b:b + 1, :]
        kb = ko_ref[b]
        vb = vo_ref[b]
        prod = jnp.concatenate(
            [kb * q[b:b + 1, g * KV_WIDTH:(g + 1) * KV_WIDTH] for g in range(ATTN_GROUP)], axis=0)
        s = _dot(prod.astype(BF16), e_ref[...])
        ps = []
        for g in range(ATTN_GROUP):
            sg = s[g * lb:(g + 1) * lb, :]
            sink = sink_ref[g:g + 1, :]
            m = jnp.maximum(jnp.max(sg, axis=0, keepdims=True), sink)
            e = jnp.exp(sg - m)
            den = jnp.sum(e, axis=0, keepdims=True) + jnp.exp(sink - m)
            ps.append(e / den)
        pe = _dot(jnp.concatenate(ps, axis=0).astype(BF16), et_ref[...])
        for g in range(ATTN_GROUP):
            og = jnp.sum(pe[g * lb:(g + 1) * lb, :] * vb, axis=0, keepdims=True)
            o_ref[b:b + 1, g * KV_WIDTH:(g + 1) * KV_WIDTH] = og


def _attn_sample(q_perm, k_new, v_new, bq_perm, bk, bv, cos_row, sin_row, sinks_gk, cache_k, cache_v):
    n = DEC_BATCH
    sb = ATTN_SB
    head_of_lane = jnp.arange(KV_WIDTH) // ATTN_HEAD_DIM
    e_mat = (head_of_lane[:, None] == jnp.arange(LANES)[None, :]).astype(BF16)
    row = lambda w: pl.BlockSpec((sb, w), lambda i: (i, 0))
    vec = lambda w: pl.BlockSpec((1, w), lambda i: (0, 0))
    full = lambda a: pl.BlockSpec(a.shape, lambda i: (0,) * a.ndim)
    cache = pl.BlockSpec((sb, WINDOW, KV_WIDTH), lambda i: (i, 0, 0))
    return pl.pallas_call(
        _attn_sample_kernel,
        grid=(n // sb,),
        in_specs=[row(ATTN_WIDTH), row(KV_WIDTH), row(KV_WIDTH),
                  vec(ATTN_WIDTH), vec(KV_WIDTH), vec(KV_WIDTH), vec(LANES), vec(LANES),
                  full(sinks_gk), full(e_mat), full(e_mat.T), cache, cache],
        out_specs=[row(ATTN_WIDTH), cache, cache],
        out_shape=[jax.ShapeDtypeStruct((n, ATTN_WIDTH), F32),
                   jax.ShapeDtypeStruct(cache_k.shape, F32),
                   jax.ShapeDtypeStruct(cache_v.shape, F32)],
        compiler_params=_cparams(("parallel",)),
        name="attn_sample",
    )(q_perm, k_new, v_new, bq_perm, bk, bv, cos_row, sin_row, sinks_gk, e_mat, e_mat.T,
      cache_k, cache_v)


DN_ROWS = 512
DN_PAIR = 2 * DN_CHUNK
CONV_PAD = 8
DN_HPS = 4


def _dn_gates_kernel(ab_ref, alog_ref, dtb_ref, cum_ref, beta_ref):
    ab = ab_ref[...]
    g = -jnp.exp(alog_ref[...]) * _softplus(ab + dtb_ref[...])
    row = lax.broadcasted_iota(jnp.int32, ab.shape, 0) % DN_CHUNK
    shift = 1
    while shift < DN_CHUNK:
        g = g + jnp.where(row >= shift, pltpu.roll(g, shift, 0), 0.0)
        shift *= 2
    cum_ref[...] = g
    beta_ref[...] = pltpu.roll(_sigmoid(ab), LANES - DN_HEADS, 1)


def _dn_gates(proj_ab, alog_pad, dtb_pad):
    blk = pl.BlockSpec((DN_ROWS, LANES), lambda i: (i, 0))
    vec = pl.BlockSpec((1, LANES), lambda i: (0, 0))
    out = jax.ShapeDtypeStruct((SEQ, LANES), F32)
    return pl.pallas_call(
        _dn_gates_kernel,
        grid=(SEQ // DN_ROWS,),
        in_specs=[blk, vec, vec],
        out_specs=[blk, blk],
        out_shape=[out, out],
        compiler_params=_cparams(("parallel",)),
        name="dn_gates",
    )(proj_ab, alog_pad, dtb_pad)


def _dn_prompt_kernel(q_ref, qp_ref, k_ref, kp_ref, v_ref, vp_ref, z_ref, cum_ref, beta_ref,
                      wq_ref, wk_ref, wv_ref, nw_ref, o_ref, so_ref, xq_s, xk_s, xv_s, s_s):
    hp = pl.program_id(0)
    i = pl.program_id(1)
    pr = DN_PAIR
    cl = DN_CHUNK

    @pl.when(i == 0)
    def _():
        s_s[...] = jnp.zeros_like(s_s)

    for x_ref, prev_ref, pad_s in ((q_ref, qp_ref, xq_s), (k_ref, kp_ref, xk_s), (v_ref, vp_ref, xv_s)):
        pad_s[0:CONV_PAD, :] = jnp.where(i > 0, prev_ref[...], 0.0)
        pad_s[CONV_PAD:, :] = x_ref[...]

    def lanes_of(hh):
        return slice(hh * DN_DIM, (hh + 1) * DN_DIM)

    def conv(pad_s, w_ref, hh, r0):
        win = pad_s[r0:r0 + CONV_PAD + pr, lanes_of(hh)]
        acc = w_ref[0:1, lanes_of(hh)] * win
        for j in range(1, CONV_WIDTH):
            acc = pltpu.roll(acc, 1, 0) + w_ref[j:j + 1, lanes_of(hh)] * win
        return _silu(acc[CONV_PAD:])

    def l2n(x):
        return x * lax.rsqrt(jnp.sum(x * x, axis=-1, keepdims=True) + RMS_EPS)

    def gate(ref, hh, r0):
        pick = ci == hp * DN_HPS + hh
        col = jnp.sum(jnp.where(pick, ref[r0:r0 + pr, :], 0.0), axis=1, keepdims=True)
        return jnp.broadcast_to(col, (pr, pr))

    ri = lax.broadcasted_iota(jnp.int32, (pr, pr), 0)
    ci = lax.broadcasted_iota(jnp.int32, (pr, pr), 1)
    same_chunk = (ri // cl) == (ci // cl)
    causal = same_chunk & (ri >= ci)
    strict = same_chunk & (ri > ci)
    eye = (ri == ci).astype(F32)
    first_rows = ri < cl
    first_rows2 = lax.broadcasted_iota(jnp.int32, (pr, 2 * DN_DIM), 0) < cl
    nw = nw_ref[...]
    bf = lambda a: a.astype(BF16)
    cat0 = lambda *a: jnp.concatenate(a, axis=0)
    cat1 = lambda *a: jnp.concatenate(a, axis=1)

    n_pairs = DN_ROWS // pr
    probs = [(hh, a * pr) for a in range(n_pairs) for hh in range(DN_HPS)]

    q_p = [l2n(conv(xq_s, wq_ref, hh, r0)) * (DN_DIM ** -0.5) for hh, r0 in probs]
    k_p = [l2n(conv(xk_s, wk_ref, hh, r0)) for hh, r0 in probs]
    v_p = [conv(xv_s, wv_ref, hh, r0) for hh, r0 in probs]
    cum = [gate(cum_ref, hh, r0) for hh, r0 in probs]
    beta = [gate(beta_ref, hh, r0) for hh, r0 in probs]
    gamma = [jnp.exp(c) for c in cum]
    decay = [jnp.where(causal, jnp.exp(c - c.T), 0.0) for c in cum]
    kq = [_dot_nt(bf(cat0(k, q)), bf(k)) for k, q in zip(k_p, q_p)]
    x = [jnp.where(strict, -(b * m[:pr] * d), 0.0) for b, m, d in zip(beta, kq, decay)]
    att = [bf(m[pr:] * d) for m, d in zip(kq, decay)]
    t = [eye + m for m in x]
    xp = [_dot(bf(m), bf(m)) for m in x]
    for _ in range(int(math.log2(cl)) - 2):
        y = [_dot(bf(p), bf(cat1(p, m))) for p, m in zip(xp, t)]
        xp = [m[:, :pr] for m in y]
        t = [m + n[:, pr:] for m, n in zip(t, y)]
    t = [m + _dot(bf(p), bf(m)) for p, m in zip(xp, t)]
    wu = [_dot(bf(m), bf(cat1(k * (b * g), v * b)))
          for m, k, v, b, g in zip(t, k_p, v_p, beta, gamma)]
    awu = [_dot(a, bf(m)) for a, m in zip(att, wu)]
    q_prime = [bf(q * g - m[:, :DN_DIM]) for q, g, m in zip(q_p, gamma, awu)]
    au = [m[:, DN_DIM:] for m in awu]
    k_dec_t = [bf((k * jnp.exp(jnp.where(first_rows, c[cl - 1:cl, :], c[pr - 1:pr, :]) - c)).T)
               for k, c in zip(k_p, cum)]
    kwu1 = [_dot(kt, bf(jnp.where(first_rows2, m, 0.0))) for kt, m in zip(k_dec_t, wu)]
    kwu2 = [_dot(kt, bf(jnp.where(first_rows2, 0.0, m))) for kt, m in zip(k_dec_t, wu)]
    g1 = [jnp.exp(c[cl - 1:cl, :]) for c in cum]
    g2 = [jnp.exp(c[pr - 1:pr, :]) for c in cum]
    comp = [_dot(bf(-m2[:, :DN_DIM]), bf(cat1(-m1[:, :DN_DIM], m1[:, DN_DIM:]))) for m1, m2 in zip(kwu1, kwu2)]
    mp12 = [-(b * m1[:, :DN_DIM]) - a * m2[:, :DN_DIM] + c[:, :DN_DIM]
            for a, b, m1, m2, c in zip(g1, g2, kwu1, kwu2, comp)]
    n12 = [b * m1[:, DN_DIM:] + c[:, DN_DIM:] + m2[:, DN_DIM:] for b, m1, m2, c in zip(g2, kwu1, kwu2, comp)]
    lhs = [cat0(qp[:cl], bf(-m1[:, :DN_DIM]), bf(m)) for qp, m1, m in zip(q_prime, kwu1, mp12)]

    s = [s_s[hh] for hh in range(DN_HPS)]
    for a in range(n_pairs):
        js = [a * DN_HPS + hh for hh in range(DN_HPS)]
        r = [_dot(lhs[j], bf(s[hh])) for hh, j in enumerate(js)]
        s_mid = [g1[j] * s[hh] + r[hh][cl:cl + DN_DIM] + kwu1[j][:, DN_DIM:] for hh, j in enumerate(js)]
        s_new = [(g1[j] * g2[j]) * s[hh] + r[hh][cl + DN_DIM:] + n12[j] for hh, j in enumerate(js)]
        o2 = [_dot(q_prime[j][cl:], bf(s_mid[hh])) for hh, j in enumerate(js)]
        for hh, j in enumerate(js):
            o = cat0(r[hh][:cl], o2[hh]) + au[j]
            o = o * lax.rsqrt(jnp.mean(o * o, axis=-1, keepdims=True) + RMS_EPS) * nw
            r0 = a * pr
            o_ref[r0:r0 + pr, lanes_of(hh)] = (o * _silu(z_ref[r0:r0 + pr, lanes_of(hh)])).astype(o_ref.dtype)
        s = s_new

    for hh in range(DN_HPS):
        s_s[hh] = s[hh]

    @pl.when(i == pl.num_programs(1) - 1)
    def _():
        for hh in range(DN_HPS):
            so_ref[hh] = s[hh]


def _dn_prompt(proj, cum, beta, w_conv, norm_w):
    rows = DN_ROWS
    width = DN_HPS * DN_DIM
    q0 = ATTN_QKV_WIDTH // width
    k0 = q0 + DN_WIDTH // width
    v0 = k0 + DN_WIDTH // width
    z0 = v0 + DN_WIDTH // width
    per = rows // CONV_PAD

    def blk(c0):
        return pl.BlockSpec((rows, width), lambda h, i: (i, c0 + h))

    def prev(c0):
        return pl.BlockSpec((CONV_PAD, width), lambda h, i: (jnp.maximum(i * per - 1, 0), c0 + h))

    def wblk(c0):
        return pl.BlockSpec((CONV_WIDTH, width), lambda h, i: (0, c0 + h))

    gate = pl.BlockSpec((rows, LANES), lambda h, i: (i, 0))
    vec = pl.BlockSpec((1, DN_DIM), lambda h, i: (0, 0))
    pad = pltpu.VMEM((rows + CONV_PAD, width), F32)
    return pl.pallas_call(
        _dn_prompt_kernel,
        grid=(DN_HEADS // DN_HPS, SEQ // rows),
        in_specs=[blk(q0), prev(q0), blk(k0), prev(k0), blk(v0), prev(v0), blk(z0), gate, gate,
                  wblk(0), wblk(DN_WIDTH // width), wblk(2 * DN_WIDTH // width), vec],
        out_specs=[pl.BlockSpec((rows, width), lambda h, i: (i, h)),
                   pl.BlockSpec((DN_HPS, DN_DIM, DN_DIM), lambda h, i: (h, 0, 0))],
        out_shape=[jax.ShapeDtypeStruct((SEQ, DN_WIDTH), BF16),
                   jax.ShapeDtypeStruct((DN_HEADS, DN_DIM, DN_DIM), F32)],
        scratch_shapes=[pad, pad, pad, pltpu.VMEM((DN_HPS, DN_DIM, DN_DIM), F32)],
        compiler_params=_cparams(("parallel", "arbitrary")),
        name="dn_prompt",
    )(proj, proj, proj, proj, proj, proj, proj, cum, beta, w_conv, w_conv, w_conv,
      norm_w.reshape(1, DN_DIM))


DN_SB = 8


def _dn_gates_of_head(ab, alog, dtb, h):
    lane = lax.broadcasted_iota(jnp.int32, ab.shape, 1)
    g_all = -jnp.exp(alog) * _softplus(ab + dtb)
    g = jnp.sum(jnp.where(lane == h, g_all, 0.0), axis=1, keepdims=True)
    beta = jnp.sum(jnp.where(lane == h + DN_HEADS, _sigmoid(ab), 0.0), axis=1, keepdims=True)
    return jnp.broadcast_to(g, ab.shape), jnp.broadcast_to(beta, ab.shape)


def _dn_sample_prep_kernel(x_ref, sc_ref, ab_ref, w_ref, alog_ref, dtb_ref,
                           qt_ref, kt_ref, vb_ref, bg_ref, gam_ref):
    ab = ab_ref[...]
    alog, dtb = alog_ref[...], dtb_ref[...]

    def conv(col):
        sl = slice(col, col + LANES)
        y = w_ref[CONV_WIDTH - 1:CONV_WIDTH, sl] * x_ref[:, sl]
        for j in range(CONV_WIDTH - 1):
            y = y + w_ref[j:j + 1, sl] * sc_ref[j, :, sl]
        return _silu(y)

    def l2n(x):
        return x * lax.rsqrt(jnp.sum(x * x, axis=-1, keepdims=True) + RMS_EPS)

    for h in range(DN_HEADS):
        q = l2n(conv(h * DN_DIM)) * (DN_DIM ** -0.5)
        k = l2n(conv(DN_WIDTH + h * DN_DIM))
        v = conv(2 * DN_WIDTH + h * DN_DIM)
        g_b, beta_b = _dn_gates_of_head(ab, alog, dtb, h)
        gamma = jnp.exp(g_b)
        qt_ref[h] = q.T
        kt_ref[h] = k.T
        vb_ref[h] = v * beta_b
        bg_ref[h] = beta_b * gamma
        gam_ref[h] = gamma


def _dn_sample_prep(qkv_s, conv_state, ab_s, w_conv, alog_pad, dtb_pad):
    n = DEC_BATCH
    full = lambda a: pl.BlockSpec(a.shape, lambda i: (0,) * a.ndim)
    hm = jax.ShapeDtypeStruct((DN_HEADS, n, DN_DIM), F32)
    hm_spec = pl.BlockSpec((DN_HEADS, n, DN_DIM), lambda i: (0, 0, 0))
    args = (qkv_s, conv_state, ab_s, w_conv, alog_pad, dtb_pad)
    return pl.pallas_call(
        _dn_sample_prep_kernel,
        grid=(1,),
        in_specs=[full(a) for a in args],
        out_specs=[hm_spec] * 5,
        out_shape=[hm] * 5,
        compiler_params=_cparams(("arbitrary",)),
        name="dn_sample_prep",
    )(*args)


def _dn_sample_kernel(s_ref, qt_ref, kt_ref, vb_ref, bg_ref, gam_ref, z_ref, nw_ref,
                      so_ref, o_ref, o_s):
    step = pl.program_id(0)
    lane = lax.broadcasted_iota(jnp.int32, (DN_DIM, DEC_BATCH), 1)

    def head(h, carry):
        kt = kt_ref[h]
        qt = qt_ref[h]
        for bb in range(DN_SB):
            b = step * DN_SB + bb
            pick = lane == b
            kcol = jnp.sum(jnp.where(pick, kt, 0.0), axis=1, keepdims=True)
            qcol = jnp.sum(jnp.where(pick, qt, 0.0), axis=1, keepdims=True)
            s = s_ref[bb, h]
            ks = jnp.sum(kcol * s, axis=0, keepdims=True)
            u = vb_ref[h, pl.ds(b, 1), :] - bg_ref[h, pl.ds(b, 1), :] * ks
            s_new = gam_ref[h, pl.ds(b, 1), :] * s + kcol * u
            so_ref[bb, h] = s_new
            o_s[h, pl.ds(bb, 1), :] = jnp.sum(qcol * s_new, axis=0, keepdims=True)
        return carry

    lax.fori_loop(0, DN_HEADS, head, 0)
    nw = nw_ref[...]
    for h in range(DN_HEADS):
        o = o_s[h]
        o = o * lax.rsqrt(jnp.mean(o * o, axis=-1, keepdims=True) + RMS_EPS) * nw
        sl = slice(h * DN_DIM, (h + 1) * DN_DIM)
        o_ref[:, sl] = o * _silu(z_ref[:, sl])


def _dn_sample(state, qt, kt, vb, bg, gam, z_s, norm_w):
    n = DEC_BATCH
    sb = DN_SB
    sblk = pl.BlockSpec((sb, DN_HEADS, DN_DIM, DN_DIM), lambda i: (i, 0, 0, 0))
    hm = pl.BlockSpec((DN_HEADS, n, DN_DIM), lambda i: (0, 0, 0))
    row = pl.BlockSpec((sb, DN_WIDTH), lambda i: (i, 0))
    return pl.pallas_call(
        _dn_sample_kernel,
        grid=(n // sb,),
        in_specs=[sblk, hm, hm, hm, hm, hm, row, pl.BlockSpec((1, DN_DIM), lambda i: (0, 0))],
        out_specs=[sblk, row],
        out_shape=[jax.ShapeDtypeStruct(state.shape, F32), jax.ShapeDtypeStruct((n, DN_WIDTH), F32)],
        scratch_shapes=[pltpu.VMEM((DN_HEADS, sb, DN_DIM), F32)],
        compiler_params=_cparams(("parallel",)),
        name="dn_sample",
    )(state, qt, kt, vb, bg, gam, z_s, norm_w.reshape(1, DN_DIM))


BM = 1040
BM_IN = 2080
BN_IN = 512
BM_OUT = 1024
BN_OUT = 1024
BF_FFN = 256


def _rope_tables(pos):
    half = ATTN_HEAD_DIM // 2
    lane = jnp.arange(LANES)
    inv_freq = ROPE_THETA ** (-(lane % half).astype(F32) / half)
    sign = jnp.where((lane % ATTN_HEAD_DIM) < half, -1.0, 1.0).astype(F32)
    ang = pos.astype(F32)[:, None] * inv_freq
    return jnp.cos(ang), jnp.sin(ang) * sign


def _group_major(a):
    lead = a.shape[:-1]
    a = a.reshape(lead + (ATTN_KV_HEADS, ATTN_GROUP, ATTN_HEAD_DIM))
    return jnp.swapaxes(a, -3, -2).reshape(lead + (ATTN_WIDTH,))


def _layer(x_prompt, x_sample, cache_k, cache_v, state_conv, state_delta, w_in, b_attn, attn_sinks,
           w_conv, dn_a_log, dn_dt_bias, dn_norm_w, w_out, ln1_g, ln1_b, w_gate, w_up, w_down,
           ln2_g, ln2_b):
    n_s = DEC_BATCH
    xb = _xcast(x_prompt, x_sample)
    w_in_t = w_in.T
    w_ab = jnp.pad(w_in_t[MAIN_WIDTH:], ((0, LANES - 2 * DN_HEADS), (0, 0)))
    proj, proj_ab = _matmul_nt(xb, w_in_t, w_ab, BM_IN, BN_IN, MAIN_WIDTH // BN_IN)

    pad16 = lambda v: jnp.pad(v, (0, LANES - DN_HEADS)).reshape(1, LANES)
    alog_pad, dtb_pad = pad16(dn_a_log), pad16(dn_dt_bias)

    cos_t, sin_t = _rope_tables(jnp.arange(SEQ, dtype=jnp.int32))
    attn_p, pk, pv = _attn_prompt(proj, b_attn, attn_sinks, cos_t, sin_t)
    cum, beta = _dn_gates(proj_ab, alog_pad, dtb_pad)
    dn_p, ps = _dn_prompt(proj, cum, beta, w_conv, dn_norm_w)
    pc = proj[SEQ - (CONV_WIDTH - 1):SEQ, ATTN_QKV_WIDTH:ATTN_QKV_WIDTH + CONV_DIM]

    proj_s = proj[SEQ:]
    cos_r, sin_r = _rope_tables(jnp.full((1,), PAST_LEN, jnp.int32))
    sinks_gk = jnp.pad(attn_sinks.reshape(ATTN_KV_HEADS, ATTN_GROUP).T,
                       ((0, 0), (0, LANES - ATTN_KV_HEADS)))
    attn_s_perm, sk, sv = _attn_sample(
        _group_major(proj_s[:, :ATTN_WIDTH]),
        proj_s[:, ATTN_WIDTH:ATTN_WIDTH + KV_WIDTH],
        proj_s[:, ATTN_WIDTH + KV_WIDTH:ATTN_QKV_WIDTH],
        _group_major(b_attn[:ATTN_WIDTH]).reshape(1, ATTN_WIDTH),
        b_attn[ATTN_WIDTH:ATTN_WIDTH + KV_WIDTH].reshape(1, KV_WIDTH),
        b_attn[ATTN_WIDTH + KV_WIDTH:].reshape(1, KV_WIDTH),
        cos_r, sin_r, sinks_gk, cache_k.reshape(n_s, WINDOW, KV_WIDTH),
        cache_v.reshape(n_s, WINDOW, KV_WIDTH))
    attn_s = jnp.swapaxes(attn_s_perm.reshape(n_s, ATTN_GROUP, ATTN_KV_HEADS, ATTN_HEAD_DIM), 1, 2)
    attn_s = attn_s.reshape(n_s, ATTN_WIDTH)

    qkv_s = proj_s[:, ATTN_QKV_WIDTH:ATTN_QKV_WIDTH + CONV_DIM]
    z_s = proj_s[:, ATTN_QKV_WIDTH + CONV_DIM:]
    qt, kt, vb, bg, gam = _dn_sample_prep(
        qkv_s, jnp.swapaxes(state_conv, 0, 1), proj_ab[SEQ:],
        w_conv, alog_pad, dtb_pad)
    ss, dn_s = _dn_sample(state_delta, qt, kt, vb, bg, gam, z_s, dn_norm_w)
    sc = jnp.concatenate([state_conv[:, 1:], qkv_s[:, None, :]], axis=1)

    mixed_p, mixed_s = _out_proj(attn_p, dn_p, attn_s.astype(BF16), dn_s.astype(BF16), w_out,
                                 BM_OUT, BN_OUT)
    h32, hb = _ln1(x_prompt, x_sample, mixed_p, mixed_s, ln1_g, ln1_b)
    ffn = _ffn(hb, w_gate, w_up, w_down, BM, BF_FFN)
    y_p, y_s = _ln2(h32, ffn, ln2_g, ln2_b)
    return (y_p, y_s, pk.reshape(WINDOW, ATTN_KV_HEADS, ATTN_HEAD_DIM),
            pv.reshape(WINDOW, ATTN_KV_HEADS, ATTN_HEAD_DIM), pc, ps,
            sk.reshape(n_s, WINDOW, ATTN_KV_HEADS, ATTN_HEAD_DIM),
            sv.reshape(n_s, WINDOW, ATTN_KV_HEADS, ATTN_HEAD_DIM), sc, ss)


def kernel(x_prompt, x_sample, cache_swa_k, cache_swa_v, state_conv, state_delta, w_in, b_attn,
           attn_sinks, w_conv, dn_a_log, dn_dt_bias, dn_norm_w, w_out, ln1_g, ln1_b, w_gate, w_up,
           w_down, ln2_g, ln2_b):
    assert x_prompt.shape == (1, SEQ, D_MODEL) and x_sample.shape == (DEC_BATCH, 1, D_MODEL)
    assert w_in.shape[0] == 1, "one layer"
    y_p, y_s, pk, pv, pc, ps, sk, sv, sc, ss = _layer(
        x_prompt[0], x_sample[:, 0], cache_swa_k[0], cache_swa_v[0], state_conv[0], state_delta[0],
        w_in[0], b_attn[0], attn_sinks[0], w_conv[0], dn_a_log[0], dn_dt_bias[0], dn_norm_w[0],
        w_out[0], ln1_g[0], ln1_b[0], w_gate[0], w_up[0], w_down[0], ln2_g[0], ln2_b[0])
    return (y_p[None], y_s[:, None], pk[None, None], pv[None, None], pc[None, None],
            ps[None, None], sk[None], sv[None], sc[None], ss[None])
```

```python
import math

import jax
import jax.numpy as jnp
from jax import lax
from jax.experimental import pallas as pl
from jax.experimental.pallas import tpu as pltpu

D_MODEL = 4096
SEQ = 8192
DEC_BATCH = 128
PAST_LEN = 8192
ROWS = SEQ + DEC_BATCH

ATTN_HEADS = 32
ATTN_KV_HEADS = 8
ATTN_HEAD_DIM = 64
ATTN_GROUP = ATTN_HEADS // ATTN_KV_HEADS
ATTN_WIDTH = ATTN_HEADS * ATTN_HEAD_DIM
KV_WIDTH = ATTN_KV_HEADS * ATTN_HEAD_DIM
WINDOW = 128
ROPE_THETA = 10000.0
DN_HEADS = 16
DN_DIM = 128
DN_WIDTH = DN_HEADS * DN_DIM
CONV_WIDTH = 4
CONV_DIM = 3 * DN_WIDTH
DN_CHUNK = 64
ATTN_QKV_WIDTH = ATTN_WIDTH + 2 * KV_WIDTH
MAIN_WIDTH = ATTN_QKV_WIDTH + CONV_DIM + DN_WIDTH
FFN_HIDDEN = 11008
DEEPNORM_ALPHA = 2.0 ** 0.25
LN_EPS = 1e-5
RMS_EPS = 1e-6

LANES = 128
VMEM_LIMIT = 56 * 1024 * 1024
VMEM_LIMIT_BIG = 61 * 1024 * 1024

F32 = jnp.float32
BF16 = jnp.bfloat16


def _cparams(sem, vmem=VMEM_LIMIT):
    return pltpu.CompilerParams(dimension_semantics=sem, vmem_limit_bytes=vmem)


def _dot(a, b):
    return jnp.dot(a, b, preferred_element_type=F32)


def _dot_nt(a, b):
    return lax.dot_general(a, b, (((1,), (1,)), ((), ())), preferred_element_type=F32)


def _sigmoid(x):
    return 0.5 * jnp.tanh(0.5 * x) + 0.5


def _silu(x):
    return x * _sigmoid(x)


def _softplus(x):
    return jnp.maximum(x, 0.0) + jnp.log(1.0 + jnp.exp(-jnp.abs(x)))


def _rope(x, cos, sin_signed):
    lane = lax.broadcasted_iota(jnp.int32, x.shape, 1)
    first_half = (lane % ATTN_HEAD_DIM) < (ATTN_HEAD_DIM // 2)
    partner = jnp.where(first_half, pltpu.roll(x, LANES - ATTN_HEAD_DIM // 2, 1),
                        pltpu.roll(x, ATTN_HEAD_DIM // 2, 1))
    return x * cos + partner * sin_signed


def _mm_nt_kernel(x_ref, wt_ref, wt_tail_ref, o_ref, o_tail_ref):
    o_ref[...] = _dot_nt(x_ref[...], wt_ref[...].astype(BF16))

    @pl.when(pl.program_id(1) == 0)
    def _():
        o_tail_ref[...] = _dot_nt(x_ref[...], wt_tail_ref[...].astype(BF16))


def _matmul_nt(x, wt, wt_tail, bm, bn, nb):
    m, k = x.shape
    nt = wt_tail.shape[0]
    return pl.pallas_call(
        _mm_nt_kernel,
        grid=(m // bm, nb),
        in_specs=[pl.BlockSpec((bm, k), lambda i, j: (i, 0), pipeline_mode=pl.Buffered(1)),
                  pl.BlockSpec((bn, k), lambda i, j: (j, 0)),
                  pl.BlockSpec((nt, k), lambda i, j: (0, 0))],
        out_specs=[pl.BlockSpec((bm, bn), lambda i, j: (i, j)),
                   pl.BlockSpec((bm, nt), lambda i, j: (i, 0))],
        out_shape=[jax.ShapeDtypeStruct((m, nb * bn), F32), jax.ShapeDtypeStruct((m, nt), F32)],
        compiler_params=_cparams(("parallel", "arbitrary")),
        name="matmul",
    )(x, wt, wt_tail)


def _out_proj_kernel(ap_ref, dp_ref, as_ref, ds_ref, w_ref, op_ref, os_ref, wb_s):
    i = pl.program_id(1)
    last = pl.num_programs(1) - 1
    ka = ap_ref.shape[1]

    @pl.when(i == 0)
    def _():
        wb_s[...] = w_ref[...].astype(BF16)

    @pl.when(i < last)
    def _():
        op_ref[...] = _dot(ap_ref[...], wb_s[:ka, :]) + _dot(dp_ref[...], wb_s[ka:, :])

    @pl.when(i == last)
    def _():
        os_ref[...] = _dot(as_ref[...], wb_s[:ka, :]) + _dot(ds_ref[...], wb_s[ka:, :])


def _out_proj(a_p, d_p, a_s, d_s, w, bm, bn):
    m, ka = a_p.shape
    kd = d_p.shape[1]
    ms = a_s.shape[0]
    n = w.shape[1]
    nt = m // bm
    tile = lambda width: pl.BlockSpec((bm, width), lambda j, i: (jnp.minimum(i, nt - 1), 0))
    dec = lambda width: pl.BlockSpec((ms, width), lambda j, i: (0, 0))
    return pl.pallas_call(
        _out_proj_kernel,
        grid=(n // bn, nt + 1),
        in_specs=[tile(ka), tile(kd), dec(ka), dec(kd),
                  pl.BlockSpec((ka + kd, bn), lambda j, i: (0, j), pipeline_mode=pl.Buffered(1))],
        out_specs=[pl.BlockSpec((bm, bn), lambda j, i: (jnp.minimum(i, nt - 1), j)),
                   pl.BlockSpec((ms, bn), lambda j, i: (0, j))],
        out_shape=[jax.ShapeDtypeStruct((m, n), F32), jax.ShapeDtypeStruct((ms, n), F32)],
        scratch_shapes=[pltpu.VMEM((ka + kd, bn), BF16)],
        compiler_params=_cparams(("parallel", "arbitrary"), VMEM_LIMIT_BIG),
        name="out_proj",
    )(a_p, d_p, a_s, d_s, w)


FFN_NCHUNK = 1024
FFN_ROW_ALIGN = 16


def _ffn_kernel(h_ref, wg_ref, wu_ref, wd_ref, o_ref):
    f = pl.program_id(1)

    @pl.when(f == 0)
    def _():
        o_ref[...] = jnp.zeros_like(o_ref)

    bm = h_ref.shape[0]
    cut = (bm // 2) // FFN_ROW_ALIGN * FFN_ROW_ALIGN
    groups = ((0, cut), (cut, bm))
    wg = wg_ref[...].astype(BF16)
    wu = wu_ref[...].astype(BF16)
    gates = [_dot(h_ref[a:b, :], wg) for a, b in groups]
    ups = [_dot(h_ref[a:b, :], wu) for a, b in groups]
    acts = [(_silu(g) * u).astype(BF16) for g, u in zip(gates, ups)]
    for n in range(0, o_ref.shape[1], FFN_NCHUNK):
        wd = wd_ref[:, n:n + FFN_NCHUNK].astype(BF16)
        for (a, b), act in zip(groups, acts):
            o_ref[a:b, n:n + FFN_NCHUNK] += _dot(act, wd)


def _ffn(h, w_gate, w_up, w_down, bm, bf):
    m, d = h.shape
    hidden = w_gate.shape[1]
    once = pl.Buffered(1)
    return pl.pallas_call(
        _ffn_kernel,
        grid=(m // bm, hidden // bf),
        in_specs=[pl.BlockSpec((bm, d), lambda i, f: (i, 0), pipeline_mode=once),
                  pl.BlockSpec((d, bf), lambda i, f: (0, f)),
                  pl.BlockSpec((d, bf), lambda i, f: (0, f)),
                  pl.BlockSpec((bf, d), lambda i, f: (f, 0))],
        out_specs=pl.BlockSpec((bm, d), lambda i, f: (i, 0), pipeline_mode=once),
        out_shape=jax.ShapeDtypeStruct((m, d), F32),
        compiler_params=_cparams(("parallel", "arbitrary"), VMEM_LIMIT_BIG),
        name="ffn",
    )(h, w_gate, w_up, w_down)


BR = WINDOW
NB_PROMPT = SEQ // BR


def _deepnorm(x, mixed, g, b):
    v = DEEPNORM_ALPHA * x + mixed
    mu = jnp.mean(v, axis=-1, keepdims=True)
    c = v - mu
    var = jnp.mean(c * c, axis=-1, keepdims=True)
    return c * lax.rsqrt(var + LN_EPS) * g + b


def _prompt_rows(width):
    return pl.BlockSpec((BR, width), lambda i: (jnp.minimum(i, NB_PROMPT - 1), 0))


def _decode_rows(width):
    return pl.BlockSpec((BR, width), lambda i: (0, 0))


def _xcast_kernel(xp_ref, xs_ref, o_ref):
    i = pl.program_id(0)
    o_ref[...] = jnp.where(i < NB_PROMPT, xp_ref[...], xs_ref[...]).astype(o_ref.dtype)


def _xcast(x_prompt, x_sample):
    d = x_prompt.shape[1]
    return pl.pallas_call(
        _xcast_kernel,
        grid=(NB_PROMPT + 1,),
        in_specs=[_prompt_rows(d), _decode_rows(d)],
        out_specs=pl.BlockSpec((BR, d), lambda i: (i, 0)),
        out_shape=jax.ShapeDtypeStruct((ROWS, d), BF16),
        compiler_params=_cparams(("arbitrary",)),
        name="xcast",
    )(x_prompt, x_sample)


def _ln1_kernel(xp_ref, xs_ref, mp_ref, ms_ref, g_ref, b_ref, o_ref, ob_ref):
    i = pl.program_id(0)
    is_prompt = i < NB_PROMPT
    x = jnp.where(is_prompt, xp_ref[...], xs_ref[...])
    mixed = jnp.where(is_prompt, mp_ref[...], ms_ref[...])
    y = _deepnorm(x, mixed, g_ref[...], b_ref[...])
    o_ref[...] = y
    ob_ref[...] = y.astype(BF16)


def _ln1(x_prompt, x_sample, mixed_p, mixed_s, g, b):
    d = mixed_p.shape[1]
    row = pl.BlockSpec((BR, d), lambda i: (i, 0))
    vec = pl.BlockSpec((1, d), lambda i: (0, 0))
    return pl.pallas_call(
        _ln1_kernel,
        grid=(NB_PROMPT + 1,),
        in_specs=[_prompt_rows(d), _decode_rows(d), _prompt_rows(d), _decode_rows(d), vec, vec],
        out_specs=[row, row],
        out_shape=[jax.ShapeDtypeStruct((ROWS, d), F32), jax.ShapeDtypeStruct((ROWS, d), BF16)],
        compiler_params=_cparams(("arbitrary",)),
        name="ln1",
    )(x_prompt, x_sample, mixed_p, mixed_s, g.reshape(1, d), b.reshape(1, d))


def _ln2_kernel(h_ref, f_ref, g_ref, b_ref, yp_ref, ys_ref):
    i = pl.program_id(0)
    y = _deepnorm(h_ref[...], f_ref[...], g_ref[...], b_ref[...])

    @pl.when(i < NB_PROMPT)
    def _():
        yp_ref[...] = y

    @pl.when(i >= NB_PROMPT)
    def _():
        ys_ref[...] = y


def _ln2(h32, ffn, g, b):
    d = h32.shape[1]
    row = pl.BlockSpec((BR, d), lambda i: (i, 0))
    vec = pl.BlockSpec((1, d), lambda i: (0, 0))
    return pl.pallas_call(
        _ln2_kernel,
        grid=(NB_PROMPT + 1,),
        in_specs=[row, row, vec, vec],
        out_specs=[_prompt_rows(d), _decode_rows(d)],
        out_shape=[jax.ShapeDtypeStruct((SEQ, d), F32), jax.ShapeDtypeStruct((DEC_BATCH, d), F32)],
        compiler_params=_cparams(("arbitrary",)),
        name="ln2",
    )(h32, ffn, g.reshape(1, d), b.reshape(1, d))


def _attn_prompt_kernel(sink_ref, q_ref, kc_ref, kp_ref, vc_ref, vp_ref, b_ref,
                        cc_ref, sc_ref, cp_ref, sp_ref, o_ref, ko_ref, vo_ref):
    i = pl.program_id(0)
    w = WINDOW
    cos_c, sin_c = cc_ref[...], sc_ref[...]
    cos_p, sin_p = cp_ref[...], sp_ref[...]
    n_kchunk = KV_WIDTH // LANES

    k_cur, k_prev, v_cur, v_prev = [], [], [], []
    for c in range(n_kchunk):
        sl = slice(c * LANES, (c + 1) * LANES)
        bk = b_ref[:, ATTN_WIDTH + c * LANES:ATTN_WIDTH + (c + 1) * LANES]
        bv = b_ref[:, ATTN_WIDTH + KV_WIDTH + c * LANES:ATTN_WIDTH + KV_WIDTH + (c + 1) * LANES]
        kc = _rope(kc_ref[:, sl] + bk, cos_c, sin_c)
        kp = _rope(kp_ref[:, sl] + bk, cos_p, sin_p)
        vc = vc_ref[:, sl] + bv
        vp = vp_ref[:, sl] + bv
        ko_ref[:, sl] = kc
        vo_ref[:, sl] = vc
        k_cur.append(kc)
        k_prev.append(kp)
        v_cur.append(vc)
        v_prev.append(vp)

    rows = ATTN_GROUP * w
    r = lax.broadcasted_iota(jnp.int32, (rows, 2 * w), 0) % w
    col = lax.broadcasted_iota(jnp.int32, (rows, 2 * w), 1)
    valid = (col > r) & (col <= r + w) & ((col >= w) | (i > 0))
    row_id = lax.broadcasted_iota(jnp.int32, (rows, 1), 0)
    lane_k = lax.broadcasted_iota(jnp.int32, (2 * w, LANES), 1)
    lane_o = lax.broadcasted_iota(jnp.int32, (w, LANES), 1)
    scale = ATTN_HEAD_DIM ** -0.5

    k2s, v2s, q4s = [], [], []
    for hk in range(ATTN_KV_HEADS):
        kchunk, khalf = hk // 2, hk % 2
        k2 = jnp.concatenate([k_prev[kchunk], k_cur[kchunk]], axis=0)
        in_half = (lane_k // ATTN_HEAD_DIM) == khalf
        k2s.append(jnp.where(in_half, k2, 0.0).astype(BF16))
        v2s.append(jnp.concatenate([v_prev[kchunk], v_cur[kchunk]], axis=0).astype(BF16))
        qs = []
        for g in range(ATTN_GROUP):
            hq = hk * ATTN_GROUP + g
            qchunk, qhalf = hq // 2, hq % 2
            sl = slice(qchunk * LANES, (qchunk + 1) * LANES)
            qc = _rope(q_ref[:, sl] + b_ref[:, sl], cos_c, sin_c) * scale
            if qhalf != khalf:
                qc = pltpu.roll(qc, ATTN_HEAD_DIM, 1)
            qs.append(qc)
        q4s.append(jnp.concatenate(qs, axis=0).astype(BF16))
    def softmax_with_sink(hk, s):
        s = jnp.where(valid, s, -jnp.inf)
        sink = jnp.zeros((rows, 1), F32)
        for g in range(ATTN_GROUP):
            sink = jnp.where(row_id // w == g, sink_ref[hk * ATTN_GROUP + g], sink)
        m = jnp.maximum(jnp.max(jnp.maximum(s[:, :w], s[:, w:]), axis=-1, keepdims=True), sink)
        e = jnp.exp(s - m)
        den = jnp.sum(e[:, :w] + e[:, w:], axis=-1, keepdims=True) + jnp.exp(sink - m)
        return (e / den).astype(BF16)

    pvs = []
    scores = _dot_nt(q4s[0], k2s[0])
    for hk in range(ATTN_KV_HEADS):
        nxt = _dot_nt(q4s[hk + 1], k2s[hk + 1]) if hk + 1 < ATTN_KV_HEADS else None
        pvs.append(_dot(softmax_with_sink(hk, scores), v2s[hk]))
        scores = nxt
    for hk, pv in enumerate(pvs):
        khalf = hk % 2
        outs = []
        for g in range(ATTN_GROUP):
            og = pv[g * w:(g + 1) * w, :]
            if (g % 2) != khalf:
                og = pltpu.roll(og, ATTN_HEAD_DIM, 1)
            outs.append(og)
        for j in range(ATTN_GROUP // 2):
            oc = jnp.where(lane_o < ATTN_HEAD_DIM, outs[2 * j], outs[2 * j + 1])
            c = hk * (ATTN_GROUP // 2) + j
            o_ref[:, c * LANES:(c + 1) * LANES] = oc.astype(o_ref.dtype)


def _attn_prompt(proj, b_attn, sinks, cos_t, sin_t):
    nb = SEQ // WINDOW
    kcol = ATTN_WIDTH // KV_WIDTH
    prev = lambda i: jnp.maximum(i - 1, 0)
    return pl.pallas_call(
        _attn_prompt_kernel,
        grid=(nb,),
        in_specs=[pl.BlockSpec(memory_space=pltpu.SMEM),
                  pl.BlockSpec((WINDOW, ATTN_WIDTH), lambda i: (i, 0)),
                  pl.BlockSpec((WINDOW, KV_WIDTH), lambda i: (i, kcol)),
                  pl.BlockSpec((WINDOW, KV_WIDTH), lambda i: (prev(i), kcol)),
                  pl.BlockSpec((WINDOW, KV_WIDTH), lambda i: (i, kcol + 1)),
                  pl.BlockSpec((WINDOW, KV_WIDTH), lambda i: (prev(i), kcol + 1)),
                  pl.BlockSpec((1, ATTN_QKV_WIDTH), lambda i: (0, 0)),
                  pl.BlockSpec((WINDOW, LANES), lambda i: (i, 0)),
                  pl.BlockSpec((WINDOW, LANES), lambda i: (i, 0)),
                  pl.BlockSpec((WINDOW, LANES), lambda i: (prev(i), 0)),
                  pl.BlockSpec((WINDOW, LANES), lambda i: (prev(i), 0))],
        out_specs=[pl.BlockSpec((WINDOW, ATTN_WIDTH), lambda i: (i, 0)),
                   pl.BlockSpec((WINDOW, KV_WIDTH), lambda i: (0, 0)),
                   pl.BlockSpec((WINDOW, KV_WIDTH), lambda i: (0, 0))],
        out_shape=[jax.ShapeDtypeStruct((SEQ, ATTN_WIDTH), BF16),
                   jax.ShapeDtypeStruct((WINDOW, KV_WIDTH), F32),
                   jax.ShapeDtypeStruct((WINDOW, KV_WIDTH), F32)],
        compiler_params=_cparams(("arbitrary",)),
        name="attn_prompt",
    )(sinks, proj, proj, proj, proj, proj, b_attn.reshape(1, ATTN_QKV_WIDTH),
      cos_t, sin_t, cos_t, sin_t)


ATTN_SB = 8


def _attn_sample_kernel(q_ref, k_ref, v_ref, bq_ref, bk_ref, bv_ref, cos_ref, sin_ref,
                        sink_ref, e_ref, et_ref, ck_ref, cv_ref, o_ref, ko_ref, vo_ref):
    cos, sin = cos_ref[...], sin_ref[...]
    scale = ATTN_HEAD_DIM ** -0.5
    qr = []
    for c in range(ATTN_WIDTH // LANES):
        sl = slice(c * LANES, (c + 1) * LANES)
        qr.append(_rope(q_ref[:, sl] + bq_ref[:, sl], cos, sin) * scale)
    q = jnp.concatenate(qr, axis=1)
    kn = jnp.concatenate(
        [_rope(k_ref[:, c * LANES:(c + 1) * LANES] + bk_ref[:, c * LANES:(c + 1) * LANES], cos, sin)
         for c in range(KV_WIDTH // LANES)], axis=1)
    vn = v_ref[...] + bv_ref[...]
    lb = WINDOW
    for b in range(ATTN_SB):
        ko_ref[b, 0:lb - 1, :] = ck_ref[b, 1:lb, :]
        ko_ref[b, lb - 1:lb, :] = kn[b:b + 1, :]
        vo_ref[b, 0:lb - 1, :] = cv_ref[b, 1:lb, :]
        vo_ref[b, lb - 1:lb, :] = vn[b:b + 1, :]
        kb = ko_ref[b]
        vb = vo_ref[b]
        prod = jnp.concatenate(
            [kb * q[b:b + 1, g * KV_WIDTH:(g + 1) * KV_WIDTH] for g in range(ATTN_GROUP)], axis=0)
        s = _dot(prod.astype(BF16), e_ref[...])
        ps = []
        for g in range(ATTN_GROUP):
            sg = s[g * lb:(g + 1) * lb, :]
            sink = sink_ref[g:g + 1, :]
            m = jnp.maximum(jnp.max(sg, axis=0, keepdims=True), sink)
            e = jnp.exp(sg - m)
            den = jnp.sum(e, axis=0, keepdims=True) + jnp.exp(sink - m)
            ps.append(e / den)
        pe = _dot(jnp.concatenate(ps, axis=0).astype(BF16), et_ref[...])
        for g in range(ATTN_GROUP):
            og = jnp.sum(pe[g * lb:(g + 1) * lb, :] * vb, axis=0, keepdims=True)
            o_ref[b:b + 1, g * KV_WIDTH:(g + 1) * KV_WIDTH] = og


def _attn_sample(q_perm, k_new, v_new, bq_perm, bk, bv, cos_row, sin_row, sinks_gk, cache_k, cache_v):
    n = DEC_BATCH
    sb = ATTN_SB
    head_of_lane = jnp.arange(KV_WIDTH) // ATTN_HEAD_DIM
    e_mat = (head_of_lane[:, None] == jnp.arange(LANES)[None, :]).astype(BF16)
    row = lambda w: pl.BlockSpec((sb, w), lambda i: (i, 0))
    vec = lambda w: pl.BlockSpec((1, w), lambda i: (0, 0))
    full = lambda a: pl.BlockSpec(a.shape, lambda i: (0,) * a.ndim)
    cache = pl.BlockSpec((sb, WINDOW, KV_WIDTH), lambda i: (i, 0, 0))
    return pl.pallas_call(
        _attn_sample_kernel,
        grid=(n // sb,),
        in_specs=[row(ATTN_WIDTH), row(KV_WIDTH), row(KV_WIDTH),
                  vec(ATTN_WIDTH), vec(KV_WIDTH), vec(KV_WIDTH), vec(LANES), vec(LANES),
                  full(sinks_gk), full(e_mat), full(e_mat.T), cache, cache],
        out_specs=[row(ATTN_WIDTH), cache, cache],
        out_shape=[jax.ShapeDtypeStruct((n, ATTN_WIDTH), F32),
                   jax.ShapeDtypeStruct(cache_k.shape, F32),
                   jax.ShapeDtypeStruct(cache_v.shape, F32)],
        compiler_params=_cparams(("parallel",)),
        name="attn_sample",
    )(q_perm, k_new, v_new, bq_perm, bk, bv, cos_row, sin_row, sinks_gk, e_mat, e_mat.T,
      cache_k, cache_v)


DN_ROWS = 512
DN_PAIR = 2 * DN_CHUNK
CONV_PAD = 8
DN_HPS = 4


def _dn_gates_kernel(ab_ref, alog_ref, dtb_ref, cum_ref, beta_ref):
    ab = ab_ref[...]
    g = -jnp.exp(alog_ref[...]) * _softplus(ab + dtb_ref[...])
    row = lax.broadcasted_iota(jnp.int32, ab.shape, 0) % DN_CHUNK
    shift = 1
    while shift < DN_CHUNK:
        g = g + jnp.where(row >= shift, pltpu.roll(g, shift, 0), 0.0)
        shift *= 2
    cum_ref[...] = g
    beta_ref[...] = pltpu.roll(_sigmoid(ab), LANES - DN_HEADS, 1)


def _dn_gates(proj_ab, alog_pad, dtb_pad):
    blk = pl.BlockSpec((DN_ROWS, LANES), lambda i: (i, 0))
    vec = pl.BlockSpec((1, LANES), lambda i: (0, 0))
    out = jax.ShapeDtypeStruct((SEQ, LANES), F32)
    return pl.pallas_call(
        _dn_gates_kernel,
        grid=(SEQ // DN_ROWS,),
        in_specs=[blk, vec, vec],
        out_specs=[blk, blk],
        out_shape=[out, out],
        compiler_params=_cparams(("parallel",)),
        name="dn_gates",
    )(proj_ab, alog_pad, dtb_pad)


def _dn_prompt_kernel(q_ref, qp_ref, k_ref, kp_ref, v_ref, vp_ref, z_ref, cum_ref, beta_ref,
                      wq_ref, wk_ref, wv_ref, nw_ref, o_ref, so_ref, xq_s, xk_s, xv_s, s_s):
    hp = pl.program_id(0)
    i = pl.program_id(1)
    pr = DN_PAIR
    cl = DN_CHUNK

    @pl.when(i == 0)
    def _():
        s_s[...] = jnp.zeros_like(s_s)

    for x_ref, prev_ref, pad_s in ((q_ref, qp_ref, xq_s), (k_ref, kp_ref, xk_s), (v_ref, vp_ref, xv_s)):
        pad_s[0:CONV_PAD, :] = jnp.where(i > 0, prev_ref[...], 0.0)
        pad_s[CONV_PAD:, :] = x_ref[...]

    def lanes_of(hh):
        return slice(hh * DN_DIM, (hh + 1) * DN_DIM)

    def conv(pad_s, w_ref, hh, r0):
        win = pad_s[r0:r0 + CONV_PAD + pr, lanes_of(hh)]
        acc = w_ref[0:1, lanes_of(hh)] * win
        for j in range(1, CONV_WIDTH):
            acc = pltpu.roll(acc, 1, 0) + w_ref[j:j + 1, lanes_of(hh)] * win
        return _silu(acc[CONV_PAD:])

    def l2n(x):
        return x * lax.rsqrt(jnp.sum(x * x, axis=-1, keepdims=True) + RMS_EPS)

    def gate(ref, hh, r0):
        pick = ci == hp * DN_HPS + hh
        col = jnp.sum(jnp.where(pick, ref[r0:r0 + pr, :], 0.0), axis=1, keepdims=True)
        return jnp.broadcast_to(col, (pr, pr))

    ri = lax.broadcasted_iota(jnp.int32, (pr, pr), 0)
    ci = lax.broadcasted_iota(jnp.int32, (pr, pr), 1)
    same_chunk = (ri // cl) == (ci // cl)
    causal = same_chunk & (ri >= ci)
    strict = same_chunk & (ri > ci)
    eye = (ri == ci).astype(F32)
    first_rows = ri < cl
    first_rows2 = lax.broadcasted_iota(jnp.int32, (pr, 2 * DN_DIM), 0) < cl
    nw = nw_ref[...]
    bf = lambda a: a.astype(BF16)
    cat0 = lambda *a: jnp.concatenate(a, axis=0)
    cat1 = lambda *a: jnp.concatenate(a, axis=1)

    n_pairs = DN_ROWS // pr
    probs = [(hh, a * pr) for a in range(n_pairs) for hh in range(DN_HPS)]

    q_p = [l2n(conv(xq_s, wq_ref, hh, r0)) * (DN_DIM ** -0.5) for hh, r0 in probs]
    k_p = [l2n(conv(xk_s, wk_ref, hh, r0)) for hh, r0 in probs]
    v_p = [conv(xv_s, wv_ref, hh, r0) for hh, r0 in probs]
    cum = [gate(cum_ref, hh, r0) for hh, r0 in probs]
    beta = [gate(beta_ref, hh, r0) for hh, r0 in probs]
    gamma = [jnp.exp(c) for c in cum]
    decay = [jnp.where(causal, jnp.exp(c - c.T), 0.0) for c in cum]
    kq = [_dot_nt(bf(cat0(k, q)), bf(k)) for k, q in zip(k_p, q_p)]
    x = [jnp.where(strict, -(b * m[:pr] * d), 0.0) for b, m, d in zip(beta, kq, decay)]
    att = [bf(m[pr:] * d) for m, d in zip(kq, decay)]
    t = [eye + m for m in x]
    xp = [_dot(bf(m), bf(m)) for m in x]
    for _ in range(int(math.log2(cl)) - 2):
        y = [_dot(bf(p), bf(cat1(p, m))) for p, m in zip(xp, t)]
        xp = [m[:, :pr] for m in y]
        t = [m + n[:, pr:] for m, n in zip(t, y)]
    t = [m + _dot(bf(p), bf(m)) for p, m in zip(xp, t)]
    wu = [_dot(bf(m), bf(cat1(k * (b * g), v * b)))
          for m, k, v, b, g in zip(t, k_p, v_p, beta, gamma)]
    awu = [_dot(a, bf(m)) for a, m in zip(att, wu)]
    q_prime = [bf(q * g - m[:, :DN_DIM]) for q, g, m in zip(q_p, gamma, awu)]
    au = [m[:, DN_DIM:] for m in awu]
    k_dec_t = [bf((k * jnp.exp(jnp.where(first_rows, c[cl - 1:cl, :], c[pr - 1:pr, :]) - c)).T)
               for k, c in zip(k_p, cum)]
    kwu1 = [_dot(kt, bf(jnp.where(first_rows2, m, 0.0))) for kt, m in zip(k_dec_t, wu)]
    kwu2 = [_dot(kt, bf(jnp.where(first_rows2, 0.0, m))) for kt, m in zip(k_dec_t, wu)]
    g1 = [jnp.exp(c[cl - 1:cl, :]) for c in cum]
    g2 = [jnp.exp(c[pr - 1:pr, :]) for c in cum]
    comp = [_dot(bf(-m2[:, :DN_DIM]), bf(cat1(-m1[:, :DN_DIM], m1[:, DN_DIM:]))) for m1, m2 in zip(kwu1, kwu2)]
    mp12 = [-(b * m1[:, :DN_DIM]) - a * m2[:, :DN_DIM] + c[:, :DN_DIM]
            for a, b, m1, m2, c in zip(g1, g2, kwu1, kwu2, comp)]
    n12 = [b * m1[:, DN_DIM:] + c[:, DN_DIM:] + m2[:, DN_DIM:] for b, m1, m2, c in zip(g2, kwu1, kwu2, comp)]
    lhs = [cat0(qp[:cl], bf(-m1[:, :DN_DIM]), bf(m)) for qp, m1, m in zip(q_prime, kwu1, mp12)]

    s = [s_s[hh] for hh in range(DN_HPS)]
    for a in range(n_pairs):
        js = [a * DN_HPS + hh for hh in range(DN_HPS)]
        r = [_dot(lhs[j], bf(s[hh])) for hh, j in enumerate(js)]
        s_mid = [g1[j] * s[hh] + r[hh][cl:cl + DN_DIM] + kwu1[j][:, DN_DIM:] for hh, j in enumerate(js)]
        s_new = [(g1[j] * g2[j]) * s[hh] + r[hh][cl + DN_DIM:] + n12[j] for hh, j in enumerate(js)]
        o2 = [_dot(q_prime[j][cl:], bf(s_mid[hh])) for hh, j in enumerate(js)]
        for hh, j in enumerate(js):
            o = cat0(r[hh][:cl], o2[hh]) + au[j]
            o = o * lax.rsqrt(jnp.mean(o * o, axis=-1, keepdims=True) + RMS_EPS) * nw
            r0 = a * pr
            o_ref[r0:r0 + pr, lanes_of(hh)] = (o * _silu(z_ref[r0:r0 + pr, lanes_of(hh)])).astype(o_ref.dtype)
        s = s_new

    for hh in range(DN_HPS):
        s_s[hh] = s[hh]

    @pl.when(i == pl.num_programs(1) - 1)
    def _():
        for hh in range(DN_HPS):
            so_ref[hh] = s[hh]


def _dn_prompt(proj, cum, beta, w_conv, norm_w):
    rows = DN_ROWS
    width = DN_HPS * DN_DIM
    q0 = ATTN_QKV_WIDTH // width
    k0 = q0 + DN_WIDTH // width
    v0 = k0 + DN_WIDTH // width
    z0 = v0 + DN_WIDTH // width
    per = rows // CONV_PAD

    def blk(c0):
        return pl.BlockSpec((rows, width), lambda h, i: (i, c0 + h))

    def prev(c0):
        return pl.BlockSpec((CONV_PAD, width), lambda h, i: (jnp.maximum(i * per - 1, 0), c0 + h))

    def wblk(c0):
        return pl.BlockSpec((CONV_WIDTH, width), lambda h, i: (0, c0 + h))

    gate = pl.BlockSpec((rows, LANES), lambda h, i: (i, 0))
    vec = pl.BlockSpec((1, DN_DIM), lambda h, i: (0, 0))
    pad = pltpu.VMEM((rows + CONV_PAD, width), F32)
    return pl.pallas_call(
        _dn_prompt_kernel,
        grid=(DN_HEADS // DN_HPS, SEQ // rows),
        in_specs=[blk(q0), prev(q0), blk(k0), prev(k0), blk(v0), prev(v0), blk(z0), gate, gate,
                  wblk(0), wblk(DN_WIDTH // width), wblk(2 * DN_WIDTH // width), vec],
        out_specs=[pl.BlockSpec((rows, width), lambda h, i: (i, h)),
                   pl.BlockSpec((DN_HPS, DN_DIM, DN_DIM), lambda h, i: (h, 0, 0))],
        out_shape=[jax.ShapeDtypeStruct((SEQ, DN_WIDTH), BF16),
                   jax.ShapeDtypeStruct((DN_HEADS, DN_DIM, DN_DIM), F32)],
        scratch_shapes=[pad, pad, pad, pltpu.VMEM((DN_HPS, DN_DIM, DN_DIM), F32)],
        compiler_params=_cparams(("parallel", "arbitrary")),
        name="dn_prompt",
    )(proj, proj, proj, proj, proj, proj, proj, cum, beta, w_conv, w_conv, w_conv,
      norm_w.reshape(1, DN_DIM))


DN_SB = 8


def _dn_gates_of_head(ab, alog, dtb, h):
    lane = lax.broadcasted_iota(jnp.int32, ab.shape, 1)
    g_all = -jnp.exp(alog) * _softplus(ab + dtb)
    g = jnp.sum(jnp.where(lane == h, g_all, 0.0), axis=1, keepdims=True)
    beta = jnp.sum(jnp.where(lane == h + DN_HEADS, _sigmoid(ab), 0.0), axis=1, keepdims=True)
    return jnp.broadcast_to(g, ab.shape), jnp.broadcast_to(beta, ab.shape)


def _dn_sample_prep_kernel(x_ref, sc_ref, ab_ref, w_ref, alog_ref, dtb_ref,
                           q_ref, k_ref, vb_ref, bg_ref, gam_ref):
    ab = ab_ref[...]
    alog, dtb = alog_ref[...], dtb_ref[...]

    def conv(col):
        sl = slice(col, col + LANES)
        y = w_ref[CONV_WIDTH - 1:CONV_WIDTH, sl] * x_ref[:, sl]
        for j in range(CONV_WIDTH - 1):
            y = y + w_ref[j:j + 1, sl] * sc_ref[j, :, sl]
        return _silu(y)

    def l2n(x):
        return x * lax.rsqrt(jnp.sum(x * x, axis=-1, keepdims=True) + RMS_EPS)

    for h in range(DN_HEADS):
        q = l2n(conv(h * DN_DIM)) * (DN_DIM ** -0.5)
        k = l2n(conv(DN_WIDTH + h * DN_DIM))
        v = conv(2 * DN_WIDTH + h * DN_DIM)
        g_b, beta_b = _dn_gates_of_head(ab, alog, dtb, h)
        gamma = jnp.exp(g_b)
        q_ref[h] = q
        k_ref[h] = k
        vb_ref[h] = v * beta_b
        bg_ref[h] = beta_b * gamma
        gam_ref[h] = gamma


def _dn_sample_prep(qkv_s, conv_state, ab_s, w_conv, alog_pad, dtb_pad):
    n = DEC_BATCH
    full = lambda a: pl.BlockSpec(a.shape, lambda i: (0,) * a.ndim)
    hm = jax.ShapeDtypeStruct((DN_HEADS, n, DN_DIM), F32)
    hm_spec = pl.BlockSpec((DN_HEADS, n, DN_DIM), lambda i: (0, 0, 0))
    args = (qkv_s, conv_state, ab_s, w_conv, alog_pad, dtb_pad)
    return pl.pallas_call(
        _dn_sample_prep_kernel,
        grid=(1,),
        in_specs=[full(a) for a in args],
        out_specs=[hm_spec] * 5,
        out_shape=[hm] * 5,
        compiler_params=_cparams(("arbitrary",)),
        name="dn_sample_prep",
    )(*args)


def _dn_sample_kernel(s_ref, q_ref, k_ref, vb_ref, bg_ref, gam_ref, z_ref, nw_ref,
                      so_ref, o_ref, o_s):
    step = pl.program_id(0)
    pad_rows = 16
    row0 = lax.broadcasted_iota(jnp.int32, (pad_rows, DN_DIM), 0) == 0

    def head(h, carry):
        bs = [step * DN_SB + bb for bb in range(DN_SB)]
        s = [s_ref[bb, h] for bb in range(DN_SB)]
        k = [k_ref[h, pl.ds(b, 1), :] for b in bs]
        ks = [_dot(jnp.broadcast_to(kr, (pad_rows, DN_DIM)).astype(BF16), sm.astype(BF16))[0:1]
              for kr, sm in zip(k, s)]
        u = [vb_ref[h, pl.ds(b, 1), :] - bg_ref[h, pl.ds(b, 1), :] * m for b, m in zip(bs, ks)]
        outer = [lax.dot_general(jnp.where(row0, kr, 0.0), jnp.broadcast_to(ur, (pad_rows, DN_DIM)),
                                 (((0,), (0,)), ((), ())), preferred_element_type=F32)
                 for kr, ur in zip(k, u)]
        s_new = [gam_ref[h, pl.ds(b, 1), :] * sm + m for b, sm, m in zip(bs, s, outer)]
        o = [_dot(jnp.broadcast_to(q_ref[h, pl.ds(b, 1), :], (pad_rows, DN_DIM)).astype(BF16),
                  sm.astype(BF16))[0:1] for b, sm in zip(bs, s_new)]
        for bb in range(DN_SB):
            so_ref[bb, h] = s_new[bb]
            o_s[h, pl.ds(bb, 1), :] = o[bb]
        return carry

    lax.fori_loop(0, DN_HEADS, head, 0)
    nw = nw_ref[...]
    for h in range(DN_HEADS):
        o = o_s[h]
        o = o * lax.rsqrt(jnp.mean(o * o, axis=-1, keepdims=True) + RMS_EPS) * nw
        sl = slice(h * DN_DIM, (h + 1) * DN_DIM)
        o_ref[:, sl] = o * _silu(z_ref[:, sl])


def _dn_sample(state, qt, kt, vb, bg, gam, z_s, norm_w):
    n = DEC_BATCH
    sb = DN_SB
    sblk = pl.BlockSpec((sb, DN_HEADS, DN_DIM, DN_DIM), lambda i: (i, 0, 0, 0))
    hm = pl.BlockSpec((DN_HEADS, n, DN_DIM), lambda i: (0, 0, 0))
    row = pl.BlockSpec((sb, DN_WIDTH), lambda i: (i, 0))
    return pl.pallas_call(
        _dn_sample_kernel,
        grid=(n // sb,),
        in_specs=[sblk, hm, hm, hm, hm, hm, row, pl.BlockSpec((1, DN_DIM), lambda i: (0, 0))],
        out_specs=[sblk, row],
        out_shape=[jax.ShapeDtypeStruct(state.shape, F32), jax.ShapeDtypeStruct((n, DN_WIDTH), F32)],
        scratch_shapes=[pltpu.VMEM((DN_HEADS, sb, DN_DIM), F32)],
        compiler_params=_cparams(("parallel",)),
        name="dn_sample",
    )(state, qt, kt, vb, bg, gam, z_s, norm_w.reshape(1, DN_DIM))


BM = 1040
BM_IN = 2080
BN_IN = 512
BM_OUT = 1024
BN_OUT = 1024
BF_FFN = 256


def _rope_tables(pos):
    half = ATTN_HEAD_DIM // 2
    lane = jnp.arange(LANES)
    inv_freq = ROPE_THETA ** (-(lane % half).astype(F32) / half)
    sign = jnp.where((lane % ATTN_HEAD_DIM) < half, -1.0, 1.0).astype(F32)
    ang = pos.astype(F32)[:, None] * inv_freq
    return jnp.cos(ang), jnp.sin(ang) * sign


def _group_major(a):
    lead = a.shape[:-1]
    a = a.reshape(lead + (ATTN_KV_HEADS, ATTN_GROUP, ATTN_HEAD_DIM))
    return jnp.swapaxes(a, -3, -2).reshape(lead + (ATTN_WIDTH,))


def _layer(x_prompt, x_sample, cache_k, cache_v, state_conv, state_delta, w_in, b_attn, attn_sinks,
           w_conv, dn_a_log, dn_dt_bias, dn_norm_w, w_out, ln1_g, ln1_b, w_gate, w_up, w_down,
           ln2_g, ln2_b):
    n_s = DEC_BATCH
    xb = _xcast(x_prompt, x_sample)
    w_in_t = w_in.T
    w_ab = jnp.pad(w_in_t[MAIN_WIDTH:], ((0, LANES - 2 * DN_HEADS), (0, 0)))
    proj, proj_ab = _matmul_nt(xb, w_in_t, w_ab, BM_IN, BN_IN, MAIN_WIDTH // BN_IN)

    pad16 = lambda v: jnp.pad(v, (0, LANES - DN_HEADS)).reshape(1, LANES)
    alog_pad, dtb_pad = pad16(dn_a_log), pad16(dn_dt_bias)

    cos_t, sin_t = _rope_tables(jnp.arange(SEQ, dtype=jnp.int32))
    attn_p, pk, pv = _attn_prompt(proj, b_attn, attn_sinks, cos_t, sin_t)
    cum, beta = _dn_gates(proj_ab, alog_pad, dtb_pad)
    dn_p, ps = _dn_prompt(proj, cum, beta, w_conv, dn_norm_w)
    pc = proj[SEQ - (CONV_WIDTH - 1):SEQ, ATTN_QKV_WIDTH:ATTN_QKV_WIDTH + CONV_DIM]

    proj_s = proj[SEQ:]
    cos_r, sin_r = _rope_tables(jnp.full((1,), PAST_LEN, jnp.int32))
    sinks_gk = jnp.pad(attn_sinks.reshape(ATTN_KV_HEADS, ATTN_GROUP).T,
                       ((0, 0), (0, LANES - ATTN_KV_HEADS)))
    attn_s_perm, sk, sv = _attn_sample(
        _group_major(proj_s[:, :ATTN_WIDTH]),
        proj_s[:, ATTN_WIDTH:ATTN_WIDTH + KV_WIDTH],
        proj_s[:, ATTN_WIDTH + KV_WIDTH:ATTN_QKV_WIDTH],
        _group_major(b_attn[:ATTN_WIDTH]).reshape(1, ATTN_WIDTH),
        b_attn[ATTN_WIDTH:ATTN_WIDTH + KV_WIDTH].reshape(1, KV_WIDTH),
        b_attn[ATTN_WIDTH + KV_WIDTH:].reshape(1, KV_WIDTH),
        cos_r, sin_r, sinks_gk, cache_k.reshape(n_s, WINDOW, KV_WIDTH),
        cache_v.reshape(n_s, WINDOW, KV_WIDTH))
    attn_s = jnp.swapaxes(attn_s_perm.reshape(n_s, ATTN_GROUP, ATTN_KV_HEADS, ATTN_HEAD_DIM), 1, 2)
    attn_s = attn_s.reshape(n_s, ATTN_WIDTH)

    qkv_s = proj_s[:, ATTN_QKV_WIDTH:ATTN_QKV_WIDTH + CONV_DIM]
    z_s = proj_s[:, ATTN_QKV_WIDTH + CONV_DIM:]
    qt, kt, vb, bg, gam = _dn_sample_prep(
        qkv_s, jnp.swapaxes(state_conv, 0, 1), proj_ab[SEQ:],
        w_conv, alog_pad, dtb_pad)
    ss, dn_s = _dn_sample(state_delta, qt, kt, vb, bg, gam, z_s, dn_norm_w)
    sc = jnp.concatenate([state_conv[:, 1:], qkv_s[:, None, :]], axis=1)

    mixed_p, mixed_s = _out_proj(attn_p, dn_p, attn_s.astype(BF16), dn_s.astype(BF16), w_out,
                                 BM_OUT, BN_OUT)
    h32, hb = _ln1(x_prompt, x_sample, mixed_p, mixed_s, ln1_g, ln1_b)
    ffn = _ffn(hb, w_gate, w_up, w_down, BM, BF_FFN)
    y_p, y_s = _ln2(h32, ffn, ln2_g, ln2_b)
    return (y_p, y_s, pk.reshape(WINDOW, ATTN_KV_HEADS, ATTN_HEAD_DIM),
            pv.reshape(WINDOW, ATTN_KV_HEADS, ATTN_HEAD_DIM), pc, ps,
            sk.reshape(n_s, WINDOW, ATTN_KV_HEADS, ATTN_HEAD_DIM),
            sv.reshape(n_s, WINDOW, ATTN_KV_HEADS, ATTN_HEAD_DIM), sc, ss)


def kernel(x_prompt, x_sample, cache_swa_k, cache_swa_v, state_conv, state_delta, w_in, b_attn,
           attn_sinks, w_conv, dn_a_log, dn_dt_bias, dn_norm_w, w_out, ln1_g, ln1_b, w_gate, w_up,
           w_down, ln2_g, ln2_b):
    assert x_prompt.shape == (1, SEQ, D_MODEL) and x_sample.shape == (DEC_BATCH, 1, D_MODEL)
    assert w_in.shape[0] == 1, "one layer"
    y_p, y_s, pk, pv, pc, ps, sk, sv, sc, ss = _layer(
        x_prompt[0], x_sample[:, 0], cache_swa_k[0], cache_swa_v[0], state_conv[0], state_delta[0],
        w_in[0], b_attn[0], attn_sinks[0], w_conv[0], dn_a_log[0], dn_dt_bias[0], dn_norm_w[0],
        w_out[0], ln1_g[0], ln1_b[0], w_gate[0], w_up[0], w_down[0], ln2_g[0], ln2_b[0])
    return (y_p[None], y_s[:, None], pk[None, None], pv[None, None], pc[None, None],
            ps[None, None], sk[None], sv[None], sc[None], ss[None])
```
